```python
import math
import jax, jax.numpy as jnp
from jax import lax
import numpy as np

D_MODEL = 1024
BATCH = 8
SEQ = 2048
DEPTH = 4

GROUP = D_MODEL // 4
D_MIX = 4 * GROUP
A_HEADS = 4
A_QK = GROUP // (2 * A_HEADS)
A_V = GROUP // A_HEADS
B_HEADS = 4
B_HD = GROUP // B_HEADS
C_BLOCKS = 4
C_BD = GROUP // C_BLOCKS
C_CONV = 4
C_EXP = 8.0
D_HEADS = 4
D_HD = GROUP // D_HEADS
D_W_RANK = 32
D_A_RANK = 32
D_V_RANK = 32
D_G_RANK = 64
D_FF = 256 * ((8 * D_MODEL // 3 + 255) // 256)
FF_CONV = 3
Q_BLOCK = 128
RMS_EPS = 1e-6
LNX_EPS = 64e-5
ADA_CHUNKS = 6

N_A = 3 * GROUP
N_B = 3 * GROUP
N_C = 2 * GROUP
N_D = 3 * GROUP + D_W_RANK + D_A_RANK + D_G_RANK
N_IN = N_A + N_B + N_C + N_D
GROUP_SPLITS = (N_A, N_A + N_B, N_A + N_B + N_C)
D_SPLITS = (GROUP, 2 * GROUP, 3 * GROUP, 3 * GROUP + D_W_RANK, 3 * GROUP + D_W_RANK + D_A_RANK)

kernel_name = 'hybrid_parallel_head_group_trunk'


def rms_last(x, g, eps=RMS_EPS):
    xf = x.astype(jnp.float32)
    y = xf * lax.rsqrt(jnp.mean(jnp.square(xf), axis=-1, keepdims=True) + eps)
    return (y * g.astype(jnp.float32)).astype(x.dtype)


def token_shift(x):
    return jnp.pad(x, ((0, 0), (1, 0), (0, 0)))[:, :-1]


def causal_dwconv(x, w, b):
    k_w, ch = w.shape
    y = lax.conv_general_dilated(x, w[:, None, :].astype(x.dtype), window_strides=(1,),
                                 padding=[(k_w - 1, 0)], dimension_numbers=('NWC', 'WIO', 'NWC'),
                                 feature_group_count=ch)
    return y + b.astype(x.dtype)


def sweep_query_blocks(fn, q):
    bsz, seq = q.shape[:2]
    nb = seq // Q_BLOCK
    qb = jnp.moveaxis(q.reshape((bsz, nb, Q_BLOCK) + q.shape[2:]), 1, 0)
    out = lax.map(lambda args: fn(args[0], args[1]), (qb, jnp.arange(nb) * Q_BLOCK))
    out = jnp.moveaxis(out, 0, 1)
    return out.reshape((bsz, seq) + out.shape[3:])


def diff_attention(q, k, v, lam):
    kpos = jnp.arange(k.shape[1])
    scale = A_QK ** -0.5

    def block(qb, start):
        qpos = start + jnp.arange(Q_BLOCK)
        causal = kpos[None, :] <= qpos[:, None]
        s = jnp.einsum('bqhcd,bkhcd->bhcqk', qb, k).astype(jnp.float32) * scale
        p = jax.nn.softmax(jnp.where(causal, s, -jnp.inf), axis=-1)
        wts = p[:, :, 0] - lam * p[:, :, 1]
        return jnp.einsum('bhqk,bkhd->bqhd', wts.astype(v.dtype), v)

    return sweep_query_blocks(block, q)


def stick_breaking_attention(q, k, v):
    kpos = jnp.arange(k.shape[1])
    scale = B_HD ** -0.5

    def block(qb, start):
        qpos = start + jnp.arange(Q_BLOCK)
        strict = kpos[None, :] < qpos[:, None]
        z = jnp.einsum('bqhd,bkhd->bhqk', qb, k).astype(jnp.float32) * scale
        log_beta = jax.nn.log_sigmoid(z)
        log_keep = jnp.where(strict, log_beta - z, 0.0)
        log_after = lax.cumsum(log_keep, axis=3, reverse=True) - log_keep
        wts = jnp.where(strict, jnp.exp(log_beta + log_after), 0.0)
        return jnp.einsum('bhqk,bkhd->bqhd', wts.astype(v.dtype), v)

    return sweep_query_blocks(block, q)


def _linear_combine(lhs, rhs):
    a_l, b_l = lhs
    a_r, b_r = rhs
    return a_l * a_r, a_r * b_l + b_r


def rg_lru(x, ga_w, ga_b, gx_w, gx_b, lam):
    bsz, seq, wd = x.shape
    xb = x.reshape(bsz, seq, C_BLOCKS, C_BD)
    r = jax.nn.sigmoid((jnp.einsum('bsnd,nde->bsne', xb, ga_w).reshape(bsz, seq, wd) + ga_b).astype(jnp.float32))
    i = jax.nn.sigmoid((jnp.einsum('bsnd,nde->bsne', xb, gx_w).reshape(bsz, seq, wd) + gx_b).astype(jnp.float32))
    log_a = -C_EXP * r * jax.nn.softplus(-lam.astype(jnp.float32))
    a = jnp.exp(log_a)
    mult = jnp.sqrt(-jnp.expm1(2.0 * log_a))
    mult = jnp.where((jnp.arange(seq) == 0)[None, :, None], 1.0, mult)
    u = mult * i * x.astype(jnp.float32)
    _, h = lax.associative_scan(_linear_combine, (a, u), axis=1)
    return h.astype(x.dtype)


def rwkv7_scan(r, decay, k, v, kk, a):
    bsz, seq, nh, hd = r.shape
    xs = tuple(jnp.moveaxis(t, 1, 0) for t in (r, decay, k, v, kk, a))

    def step(state, inp):
        r_t, w_t, k_t, v_t, kk_t, a_t = inp
        s_kk = jnp.einsum('bhvk,bhk->bhv', state, kk_t)
        state = (state * w_t[:, :, None, :]
                 - s_kk[..., None] * (kk_t * a_t)[:, :, None, :]
                 + v_t[..., None] * k_t[:, :, None, :])
        return state, jnp.einsum('bhvk,bhk->bhv', state, r_t)

    state0 = jnp.zeros((bsz, nh, hd, hd), jnp.float32)
    _, ys = lax.scan(step, state0, xs)
    return jnp.moveaxis(ys, 0, 1)


def mixer_diff_attn(p, q_g, k_g, lam_q, lam_k, out_g, lam_init):
    bsz, seq, _ = p.shape
    q, k, v = jnp.split(p, 3, axis=-1)
    q = rms_last(q.reshape(bsz, seq, A_HEADS, 2, A_QK), q_g)
    k = rms_last(k.reshape(bsz, seq, A_HEADS, 2, A_QK), k_g)
    v = v.reshape(bsz, seq, A_HEADS, A_V)
    lq = lam_q.astype(jnp.float32)
    lk = lam_k.astype(jnp.float32)
    lam = jnp.exp(jnp.sum(lq[0] * lk[0])) - jnp.exp(jnp.sum(lq[1] * lk[1])) + lam_init
    o = diff_attention(q, k, v, lam)
    o = rms_last(o, out_g.reshape(A_HEADS, A_V)) * (1.0 - lam_init)
    return o.reshape(bsz, seq, GROUP)


def mixer_stick_breaking(p, out_g):
    bsz, seq, _ = p.shape
    q, k, v = (t.reshape(bsz, seq, B_HEADS, B_HD) for t in jnp.split(p, 3, axis=-1))
    o = stick_breaking_attention(q, k, v)
    o = rms_last(o, out_g.reshape(B_HEADS, B_HD))
    return o.reshape(bsz, seq, GROUP)


def mixer_rglru(p, conv_w, conv_b, ga_w, ga_b, gx_w, gx_b, lam, out_g):
    bsz, seq, _ = p.shape
    xr, xg = jnp.split(p, 2, axis=-1)
    xr = causal_dwconv(xr, conv_w, conv_b)
    y = rg_lru(xr, ga_w, ga_b, gx_w, gx_b, lam) * jax.nn.gelu(xg)
    y = rms_last(y.reshape(bsz, seq, C_BLOCKS, C_BD), out_g.reshape(C_BLOCKS, C_BD))
    return y.reshape(bsz, seq, GROUP)


def mixer_rwkv7(p, mu, w0, w_up, a0, a_up, g_up, k_k, k_a, r_k, lnx_w, lnx_b, v_first, v_res):
    bsz, seq, _ = p.shape
    f32 = jnp.float32
    p = p + (token_shift(p) - p) * mu
    r, k, v, w_lr, a_lr, g_lr = jnp.split(p, D_SPLITS, axis=-1)
    w = -jax.nn.softplus(-(w0 + jnp.tanh(w_lr) @ w_up).astype(f32)) - 0.5
    a = jax.nn.sigmoid((a0 + a_lr @ a_up).astype(f32))
    g = jax.nn.sigmoid(g_lr) @ g_up
    if v_res is not None:
        v0, v_down, v_up = v_res
        v = v + (v_first - v) * jax.nn.sigmoid(v0 + (v @ v_down) @ v_up)

    def heads(t):
        return t.astype(f32).reshape(bsz, seq, D_HEADS, D_HD)

    rh, kh, vh, ah = heads(r), heads(k), heads(v), heads(a)
    kk = kh * k_k.astype(f32).reshape(D_HEADS, D_HD)
    kk = kk / jnp.maximum(jnp.sqrt(jnp.sum(kk * kk, axis=-1, keepdims=True)), 1e-12)
    kh = kh * (1.0 + (ah - 1.0) * k_a.astype(f32).reshape(D_HEADS, D_HD))
    decay = jnp.exp(-jnp.exp(heads(w)))
    y = rwkv7_scan(rh, decay, kh, vh, kk, ah)
    mean = jnp.mean(y, axis=-1, keepdims=True)
    var = jnp.mean(jnp.square(y - mean), axis=-1, keepdims=True)
    y = ((y - mean) * lax.rsqrt(var + LNX_EPS)).reshape(bsz, seq, GROUP)
    y = y * lnx_w.astype(f32) + lnx_b.astype(f32)
    bonus = jnp.sum(rh * kh * r_k.astype(f32), axis=-1, keepdims=True) * vh
    y = y + bonus.reshape(bsz, seq, GROUP)
    return y.astype(p.dtype) * g, v


def setup_inputs(seed: int = 0) -> dict:
    key = jax.random.key(seed)
    keys = iter(jax.random.split(key, 48))
    f32 = jnp.float32

    def nrm(shape, std):
        return std * jax.random.normal(next(keys), shape, f32)

    def gain(shape):
        return 1.0 + nrm(shape, 0.02)

    L = DEPTH
    u = jax.random.uniform(next(keys), (L, GROUP), f32, 0.9, 0.999)
    s = u ** (1.0 / C_EXP)
    c_lambda = jnp.log(s) - jnp.log1p(-s)
    return {
        'x': nrm((BATCH, SEQ, D_MODEL), 1.0),
        'c': nrm((BATCH, D_MODEL), 1.0),
        'w_ada': nrm((L, D_MODEL, ADA_CHUNKS * D_MODEL), 0.5 * D_MODEL ** -0.5),
        'b_ada': nrm((L, ADA_CHUNKS * D_MODEL), 0.02),
        'norm1_g': gain((L, D_MODEL)),
        'norm2_g': gain((L, D_MODEL)),
        'w_in': nrm((L, D_MODEL, N_IN), D_MODEL ** -0.5),
        'w_out': nrm((L, D_MIX, D_MODEL), D_MIX ** -0.5),
        'a_qnorm_g': gain((L, A_QK)),
        'a_knorm_g': gain((L, A_QK)),
        'a_lam_q': nrm((L, 2, A_QK), 0.1),
        'a_lam_k': nrm((L, 2, A_QK), 0.1),
        'a_out_g': gain((L, GROUP)),
        'b_out_g': gain((L, GROUP)),
        'c_conv_w': nrm((L, C_CONV, GROUP), C_CONV ** -0.5),
        'c_conv_b': nrm((L, GROUP), 0.02),
        'c_gate_a_w': nrm((L, C_BLOCKS, C_BD, C_BD), C_BD ** -0.5),
        'c_gate_a_b': nrm((L, GROUP), 0.02),
        'c_gate_x_w': nrm((L, C_BLOCKS, C_BD, C_BD), C_BD ** -0.5),
        'c_gate_x_b': nrm((L, GROUP), 0.02),
        'c_lambda': c_lambda,
        'c_out_g': gain((L, GROUP)),
        'd_mu': jax.random.uniform(next(keys), (L, N_D), f32),
        'd_w0': jnp.linspace(-6.0, -1.0, GROUP, dtype=f32)[None, :] + nrm((L, GROUP), 0.1),
        'd_w_up': nrm((L, D_W_RANK, GROUP), 0.1 * D_W_RANK ** -0.5),
        'd_a0': nrm((L, GROUP), 0.1),
        'd_a_up': nrm((L, D_A_RANK, GROUP), 0.5 * D_A_RANK ** -0.5),
        'd_g_up': nrm((L, D_G_RANK, GROUP), D_G_RANK ** -0.5),
        'd_k_k': 0.85 + nrm((L, GROUP), 0.02),
        'd_k_a': gain((L, GROUP)),
        'd_r_k': nrm((L, D_HEADS, D_HD), 0.1),
        'd_lnx_w': gain((L, GROUP)),
        'd_lnx_b': nrm((L, GROUP), 0.02),
        'd_v0': nrm((L - 1, GROUP), 0.1),
        'd_v_down': nrm((L - 1, GROUP, D_V_RANK), GROUP ** -0.5),
        'd_v_up': nrm((L - 1, D_V_RANK, GROUP), 0.5 * D_V_RANK ** -0.5),
        'ff_w_up': nrm((L, D_MODEL, 2 * D_FF), D_MODEL ** -0.5),
        'ff_conv_w': nrm((L, FF_CONV, 2 * D_FF), FF_CONV ** -0.5),
        'ff_conv_b': nrm((L, 2 * D_FF), 0.02),
        'ff_w_down': nrm((L, D_FF, D_MODEL), D_FF ** -0.5),
    }


def reference(x, c, w_ada, b_ada, norm1_g, norm2_g, w_in, w_out,
              a_qnorm_g, a_knorm_g, a_lam_q, a_lam_k, a_out_g, b_out_g,
              c_conv_w, c_conv_b, c_gate_a_w, c_gate_a_b, c_gate_x_w, c_gate_x_b, c_lambda, c_out_g,
              d_mu, d_w0, d_w_up, d_a0, d_a_up, d_g_up, d_k_k, d_k_a, d_r_k, d_lnx_w, d_lnx_b,
              d_v0, d_v_down, d_v_up,
              ff_w_up, ff_conv_w, ff_conv_b, ff_w_down):
    cond = jax.nn.silu(c)
    v_first = None
    for l in range(DEPTH):
        mod = cond @ w_ada[l] + b_ada[l]
        sh1, sc1, gt1, sh2, sc2, gt2 = jnp.split(mod[:, None, :], ADA_CHUNKS, axis=-1)

        h = rms_last(x, norm1_g[l]) * (1.0 + sc1) + sh1
        p_a, p_b, p_c, p_d = jnp.split(h @ w_in[l], GROUP_SPLITS, axis=-1)
        lam_init = 0.8 - 0.6 * math.exp(-0.3 * l)
        y_a = mixer_diff_attn(p_a, a_qnorm_g[l], a_knorm_g[l], a_lam_q[l], a_lam_k[l], a_out_g[l], lam_init)
        y_b = mixer_stick_breaking(p_b, b_out_g[l])
        y_c = mixer_rglru(p_c, c_conv_w[l], c_conv_b[l], c_gate_a_w[l], c_gate_a_b[l],
                          c_gate_x_w[l], c_gate_x_b[l], c_lambda[l], c_out_g[l])
        v_res = None if l == 0 else (d_v0[l - 1], d_v_down[l - 1], d_v_up[l - 1])
        y_d, v_d = mixer_rwkv7(p_d, d_mu[l], d_w0[l], d_w_up[l], d_a0[l], d_a_up[l], d_g_up[l],
                               d_k_k[l], d_k_a[l], d_r_k[l], d_lnx_w[l], d_lnx_b[l], v_first, v_res)
        if l == 0:
            v_first = v_d
        mix = jnp.concatenate([y_a, y_b, y_c, y_d], axis=-1) @ w_out[l]
        x = x + gt1 * mix

        h = rms_last(x, norm2_g[l]) * (1.0 + sc2) + sh2
        u = causal_dwconv(h @ ff_w_up[l], ff_conv_w[l], ff_conv_b[l])
        u_g, u_v = jnp.split(u, 2, axis=-1)
        x = x + gt2 * ((jax.nn.silu(u_g) * u_v) @ ff_w_down[l])
    return x
```

```python
import functools
import math

import numpy as np
import jax
import jax.numpy as jnp
from jax import lax
from jax.experimental import pallas as pl
from jax.experimental.pallas import tpu as pltpu

F32 = jnp.float32
BF16 = jnp.bfloat16

GROUP = 256
A_HEADS = 4
A_QK = 32
HEAD = 64
N_A = 3 * GROUP
N_B = 3 * GROUP
N_C = 2 * GROUP
LORA_W = 32
LORA_A = 32
LORA_G = 64
N_D = 3 * GROUP + LORA_W + LORA_A + LORA_G
C_CONV = 4
C_EXP = 8.0
FF_CONV = 3
RMS_EPS = 1e-6
LNX_EPS = 64e-5
ADA_CHUNKS = 6

ROW_TILE = 512
ATT_TILE = 256
LRU_TILE = 256
RWKV_CHUNK = 128
FF_CHUNK = 256
INV_BASE = 8
HIST = 8
VMEM_LIMIT = 56 * 1024 * 1024


def _dot(a, b):
    return jnp.dot(a.astype(BF16), b.astype(BF16), preferred_element_type=F32)


def _dot_nt(a, b):
    return lax.dot_general(a.astype(BF16), b.astype(BF16), (((1,), (1,)), ((), ())),
                           preferred_element_type=F32)


def _split2(x):
    hi = x.astype(BF16)
    lo = (x - hi.astype(F32)).astype(BF16)
    return hi, lo


def _dot_split2(x, m):
    hi, lo = _split2(x)
    return (jnp.dot(hi, m, preferred_element_type=F32)
            + jnp.dot(lo, m, preferred_element_type=F32))


def _dot_split3_lhs_exact(m, x):
    hi = x.astype(BF16)
    r1 = x - hi.astype(F32)
    mid = r1.astype(BF16)
    lo = (r1 - mid.astype(F32)).astype(BF16)
    return (jnp.dot(m, hi, preferred_element_type=F32)
            + jnp.dot(m, mid, preferred_element_type=F32)
            + jnp.dot(m, lo, preferred_element_type=F32))


def _sigmoid(x):
    return 1.0 / (1.0 + jnp.exp(-x))


def _softplus(x):
    return jnp.maximum(x, 0.0) + jnp.log1p(jnp.exp(-jnp.abs(x)))


def _lane_mask(width, lo, hi):
    lane = lax.broadcasted_iota(jnp.int32, (1, width), 1)
    return (lane >= lo) & (lane < hi)


def _shift_rows(hist, x, d):
    ext = jnp.concatenate([hist, x], axis=0)
    return pltpu.roll(ext, d, axis=0)[HIST:, :]


def _full(shape):
    nd = len(shape)
    return pl.BlockSpec(shape, lambda *_: (0,) * nd)


def _resident(shape):
    nd = len(shape)
    return pl.BlockSpec(shape, lambda *_: (0,) * nd, pipeline_mode=pl.Buffered(1))


def _block_diag_ones(width, block):
    idx = np.arange(width) // block
    return jnp.asarray((idx[:, None] == idx[None, :]).astype(np.float32), dtype=BF16)


def _ada_kernel(c_ref, w_ref, b_ref, o_ref):
    c = c_ref[...]
    cond = c * _sigmoid(c)
    o_ref[0] = _dot(cond, w_ref[0]) + b_ref[0]


def _ada_modulation(c, w_ada, b_ada):
    depth, d, n = w_ada.shape
    bsz = c.shape[0]
    tn = 1536
    return pl.pallas_call(
        _ada_kernel,
        out_shape=jax.ShapeDtypeStruct((depth, bsz, n), F32),
        grid=(depth, n // tn),
        in_specs=[pl.BlockSpec((bsz, d), lambda l, j: (0, 0)),
                  pl.BlockSpec((1, d, tn), lambda l, j: (l, 0, j)),
                  pl.BlockSpec((1, 1, tn), lambda l, j: (l, 0, j))],
        out_specs=pl.BlockSpec((1, bsz, tn), lambda l, j: (l, 0, j)),
        compiler_params=pltpu.CompilerParams(dimension_semantics=("arbitrary", "arbitrary"),
                                             vmem_limit_bytes=VMEM_LIMIT),
        name="ada_modulation",
    )(c, w_ada, b_ada.reshape(depth, 1, n))


def _modulated_norm(x, g, shift, scale):
    ms = jnp.mean(x * x, axis=-1, keepdims=True)
    return (x * lax.rsqrt(ms + RMS_EPS) * g) * (1.0 + scale) + shift


def _in_proj_kernel(x_ref, mod_ref, g_ref, w_ref, pa_ref, pb_ref, pc_ref, pd_ref):
    d = x_ref.shape[-1]
    mod = mod_ref[0]
    h = _modulated_norm(x_ref[0], g_ref[...], mod[:, 0:d], mod[:, d:2 * d]).astype(BF16)
    pa_ref[0] = jnp.dot(h, w_ref[:, 0:N_A], preferred_element_type=F32)
    pb_ref[0] = jnp.dot(h, w_ref[:, N_A:N_A + N_B], preferred_element_type=F32)
    pc_ref[0] = jnp.dot(h, w_ref[:, N_A + N_B:N_A + N_B + N_C], preferred_element_type=F32)
    pd_ref[0] = jnp.dot(h, w_ref[:, N_A + N_B + N_C:], preferred_element_type=F32)


def _in_proj(x, mod, g, w_bf16):
    bsz, seq, d = x.shape
    n_in = w_bf16.shape[1]
    tm = min(ROW_TILE, seq)
    widths = (N_A, N_B, N_C, N_D)
    return pl.pallas_call(
        _in_proj_kernel,
        out_shape=[jax.ShapeDtypeStruct((bsz, seq, n), F32) for n in widths],
        grid=(bsz, seq // tm),
        in_specs=[pl.BlockSpec((1, tm, d), lambda b, i: (b, i, 0)),
                  pl.BlockSpec((1, 1, mod.shape[-1]), lambda b, i: (b, 0, 0)),
                  _full((1, d)),
                  _resident((d, n_in))],
        out_specs=[pl.BlockSpec((1, tm, n), lambda b, i: (b, i, 0)) for n in widths],
        compiler_params=pltpu.CompilerParams(dimension_semantics=("arbitrary", "arbitrary"),
                                             vmem_limit_bytes=VMEM_LIMIT),
        name="in_proj",
    )(x, mod, g.reshape(1, d), w_bf16)


def _group_rms(x, bd, group, gain):
    ms = _dot_split2(x * x, bd) * (1.0 / group)
    return x * lax.rsqrt(ms + RMS_EPS) * gain


def _attn_a_kernel(p_ref, qg_ref, kg_ref, lq_ref, lk_ref, og_ref, bd32_ref, bd64_ref,
                   o_ref, kn_ref, *, lam_init):
    tq = o_ref.shape[1]
    qi = pl.program_id(1)
    bd32 = bd32_ref[...]

    @pl.when(qi == 0)
    def _():
        k = p_ref[0, :, GROUP:2 * GROUP]
        kn_ref[...] = _group_rms(k, bd32, A_QK, kg_ref[...]).astype(BF16)

    lq = lq_ref[...]
    lk = lk_ref[...]
    lam = (jnp.exp(jnp.sum(lq[0:1] * lk[0:1], axis=-1, keepdims=True))
           - jnp.exp(jnp.sum(lq[1:2] * lk[1:2], axis=-1, keepdims=True)) + lam_init)

    q0 = pl.multiple_of(qi * tq, tq)
    q = p_ref[0, pl.ds(q0, tq), 0:GROUP]
    qn = _group_rms(q, bd32, A_QK, qg_ref[...]) * (A_QK ** -0.5)

    row = lax.broadcasted_iota(jnp.int32, (tq, tq), 0)
    col = lax.broadcasted_iota(jnp.int32, (tq, tq), 1)
    causal = col <= row

    def scores_update(qm, vmask, k0, carry, diag):
        m, l, acc = carry
        kb = kn_ref[pl.ds(k0, tq), :]
        s = _dot_nt(qm, kb)
        if diag:
            s = jnp.where(causal, s, -jnp.inf)
        m_new = jnp.maximum(m, jnp.max(s, axis=-1, keepdims=True))
        alpha = jnp.exp(m - m_new)
        p = jnp.exp(s - m_new)
        l_new = alpha * l + jnp.sum(p, axis=-1, keepdims=True)
        vb = jnp.where(vmask, p_ref[0, pl.ds(k0, tq), 2 * GROUP:3 * GROUP], 0.0)
        acc_new = alpha * acc + _dot(p, vb)
        return m_new, l_new, acc_new

    out = jnp.zeros((tq, GROUP), F32)
    for h in range(A_HEADS):
        vmask = _lane_mask(GROUP, h * HEAD, (h + 1) * HEAD)
        parts = []
        for c in range(2):
            lo = h * HEAD + c * A_QK
            qm = jnp.where(_lane_mask(GROUP, lo, lo + A_QK), qn, 0.0).astype(BF16)
            init = (jnp.full((tq, 1), -jnp.inf, F32), jnp.zeros((tq, 1), F32),
                    jnp.zeros((tq, GROUP), F32))
            body = lambda kj, carry, qm=qm, vmask=vmask: scores_update(
                qm, vmask, pl.multiple_of(kj * tq, tq), carry, False)
            carry = lax.fori_loop(0, qi, body, init)
            m, l, acc = scores_update(qm, vmask, q0, carry, True)
            parts.append(acc / l)
        out = out + parts[0] - lam * parts[1]

    y = _group_rms(out, bd64_ref[...], HEAD, og_ref[...]) * (1.0 - lam_init)
    o_ref[0] = y.astype(o_ref.dtype)


def _mixer_diff_attn(p_a, q_g, k_g, lam_q, lam_k, out_g, lam_init):
    bsz, seq, _ = p_a.shape
    tq = min(ATT_TILE, seq)
    reps = GROUP // A_QK
    return pl.pallas_call(
        functools.partial(_attn_a_kernel, lam_init=lam_init),
        out_shape=jax.ShapeDtypeStruct((bsz, seq, GROUP), BF16),
        grid=(bsz, seq // tq),
        in_specs=[pl.BlockSpec((1, seq, N_A), lambda b, i: (b, 0, 0)),
                  _full((1, GROUP)), _full((1, GROUP)),
                  _full((2, A_QK)), _full((2, A_QK)),
                  _full((1, GROUP)),
                  _full((GROUP, GROUP)), _full((GROUP, GROUP))],
        out_specs=pl.BlockSpec((1, tq, GROUP), lambda b, i: (b, i, 0)),
        scratch_shapes=[pltpu.VMEM((seq, GROUP), BF16)],
        compiler_params=pltpu.CompilerParams(dimension_semantics=("arbitrary", "arbitrary"),
                                             vmem_limit_bytes=VMEM_LIMIT),
        name="diff_attention",
    )(p_a, jnp.tile(q_g, reps).reshape(1, GROUP), jnp.tile(k_g, reps).reshape(1, GROUP),
      lam_q, lam_k, out_g.reshape(1, GROUP),
      _block_diag_ones(GROUP, A_QK), _block_diag_ones(GROUP, HEAD))


def _attn_b_kernel(p_ref, og_ref, bd64_ref, tri_ref, o_ref):
    tq = o_ref.shape[1]
    qi = pl.program_id(1)
    q0 = pl.multiple_of(qi * tq, tq)
    q = p_ref[0, pl.ds(q0, tq), 0:GROUP] * (HEAD ** -0.5)
    tri = tri_ref[...]
    row = lax.broadcasted_iota(jnp.int32, (tq, tq), 0)
    col = lax.broadcasted_iota(jnp.int32, (tq, tq), 1)
    strict = col < row

    def block(qm, vmask, k0, carry, diag):
        later, acc = carry
        z = _dot_nt(qm, p_ref[0, pl.ds(k0, tq), GROUP:2 * GROUP])
        sp = _softplus(z)
        log_keep = -sp
        log_beta = z - sp
        if diag:
            log_keep = jnp.where(strict, log_keep, 0.0)
        incl = _dot_split2(log_keep, tri)
        w = jnp.exp(log_beta + (incl - log_keep) + later)
        if diag:
            w = jnp.where(strict, w, 0.0)
        vb = jnp.where(vmask, p_ref[0, pl.ds(k0, tq), 2 * GROUP:3 * GROUP], 0.0)
        return later + incl[:, 0:1], acc + _dot(w, vb)

    acc = jnp.zeros((tq, GROUP), F32)
    for h in range(GROUP // HEAD):
        vmask = _lane_mask(GROUP, h * HEAD, (h + 1) * HEAD)
        qm = jnp.where(vmask, q, 0.0).astype(BF16)
        carry = block(qm, vmask, q0, (jnp.zeros((tq, 1), F32), acc), True)
        body = lambda i, carry, qm=qm, vmask=vmask: block(
            qm, vmask, pl.multiple_of((qi - 1 - i) * tq, tq), carry, False)
        _, acc = lax.fori_loop(0, qi, body, carry)

    o_ref[0] = _group_rms(acc, bd64_ref[...], HEAD, og_ref[...]).astype(o_ref.dtype)


def _mixer_stick_breaking(p_b, out_g):
    bsz, seq, _ = p_b.shape
    tq = min(ATT_TILE, seq)
    tri = jnp.asarray(np.tril(np.ones((tq, tq), np.float32)), dtype=BF16)
    return pl.pallas_call(
        _attn_b_kernel,
        out_shape=jax.ShapeDtypeStruct((bsz, seq, GROUP), BF16),
        grid=(bsz, seq // tq),
        in_specs=[pl.BlockSpec((1, seq, N_B), lambda b, i: (b, 0, 0)),
                  _full((1, GROUP)), _full((GROUP, GROUP)), _full((tq, tq))],
        out_specs=pl.BlockSpec((1, tq, GROUP), lambda b, i: (b, i, 0)),
        compiler_params=pltpu.CompilerParams(dimension_semantics=("arbitrary", "arbitrary"),
                                             vmem_limit_bytes=VMEM_LIMIT),
        name="stick_breaking_attention",
    )(p_b, out_g.reshape(1, GROUP), _block_diag_ones(GROUP, HEAD), tri)


def _gelu_tanh(x):
    return 0.5 * x * (1.0 + jnp.tanh(math.sqrt(2.0 / math.pi) * (x + 0.044715 * (x * x * x))))


def _rglru_kernel(p_ref, cw_ref, cb_ref, gaw_ref, gab_ref, gxw_ref, gxb_ref, lam_ref, og_ref,
                  bd64_ref, o_ref, hist_ref, h_ref):
    tt = o_ref.shape[1]
    ti = pl.program_id(1)

    @pl.when(ti == 0)
    def _():
        hist_ref[...] = jnp.zeros_like(hist_ref)
        h_ref[...] = jnp.zeros_like(h_ref)

    x_raw = p_ref[0, :, 0:GROUP]
    x_gate = p_ref[0, :, GROUP:2 * GROUP]
    hist = hist_ref[...]
    cw = cw_ref[...]
    x = cw[C_CONV - 1:C_CONV] * x_raw + cb_ref[...]
    for d in range(1, C_CONV):
        x = x + cw[C_CONV - 1 - d:C_CONV - d] * _shift_rows(hist, x_raw, d)
    hist_ref[...] = x_raw[tt - HIST:, :]

    r = _sigmoid(_dot(x, gaw_ref[...]) + gab_ref[...])
    i = _sigmoid(_dot(x, gxw_ref[...]) + gxb_ref[...])
    log_a = (-C_EXP) * r * _softplus(-lam_ref[...])
    a = jnp.exp(log_a)
    mult = jnp.sqrt(jnp.tanh(-log_a) * (a * a + 1.0))
    row = lax.broadcasted_iota(jnp.int32, (tt, 1), 0)
    mult = jnp.where((row == 0) & (ti == 0), 1.0, mult)
    u = mult * i * x

    d = 1
    while d < tt:
        a_prev = pltpu.roll(a, d, axis=0)
        u_prev = pltpu.roll(u, d, axis=0)
        keep = row >= d
        u = jnp.where(keep, a * u_prev + u, u)
        a = jnp.where(keep, a * a_prev, a)
        d *= 2
    h = u + a * h_ref[0:1, :]
    h_ref[...] = jnp.broadcast_to(h[tt - 1:tt, :], h_ref.shape)

    y = h * _gelu_tanh(x_gate)
    o_ref[0] = _group_rms(y, bd64_ref[...], HEAD, og_ref[...]).astype(o_ref.dtype)


def _block_diag_weight(w):
    nb, d, _ = w.shape
    eye = jnp.eye(nb, dtype=w.dtype)
    return (eye[:, None, :, None] * w[:, :, None, :]).reshape(nb * d, nb * d)


def _mixer_rglru(p_c, conv_w, conv_b, ga_w, ga_b, gx_w, gx_b, lam, out_g):
    bsz, seq, _ = p_c.shape
    tt = min(LRU_TILE, seq)
    row = lambda v: v.reshape(1, GROUP)
    return pl.pallas_call(
        _rglru_kernel,
        out_shape=jax.ShapeDtypeStruct((bsz, seq, GROUP), BF16),
        grid=(bsz, seq // tt),
        in_specs=[pl.BlockSpec((1, tt, N_C), lambda b, i: (b, i, 0)),
                  _full((C_CONV, GROUP)), _full((1, GROUP)),
                  _full((GROUP, GROUP)), _full((1, GROUP)),
                  _full((GROUP, GROUP)), _full((1, GROUP)),
                  _full((1, GROUP)), _full((1, GROUP)), _full((GROUP, GROUP))],
        out_specs=pl.BlockSpec((1, tt, GROUP), lambda b, i: (b, i, 0)),
        scratch_shapes=[pltpu.VMEM((HIST, GROUP), F32), pltpu.VMEM((HIST, GROUP), F32)],
        compiler_params=pltpu.CompilerParams(dimension_semantics=("arbitrary", "arbitrary"),
                                             vmem_limit_bytes=VMEM_LIMIT),
        name="rg_lru",
    )(p_c, conv_w, row(conv_b), _block_diag_weight(ga_w).astype(BF16), row(ga_b),
      _block_diag_weight(gx_w).astype(BF16), row(gx_b), row(lam), row(out_g),
      _block_diag_ones(GROUP, HEAD))


def _seg_sum(x, bd):
    return _dot_split2(x, bd)


def _unit_lower_inverse(a, row, col):
    n = a.shape[0]

    def same_block(size):
        shift = size.bit_length() - 1
        return (row >> shift) == (col >> shift)

    power = jnp.where(same_block(INV_BASE), -a, 0.0)
    t = (row == col).astype(F32) + power
    k = 1
    while 2 * k < INV_BASE:
        power = _dot(power, power)
        t = t + _dot(t, power)
        k *= 2
    size = INV_BASE
    while size < n:
        off = jnp.where(same_block(2 * size) & jnp.logical_not(same_block(size)), a, 0.0)
        t = t - _dot(_dot(t, off), t)
        size *= 2
    return t


def _rwkv_kernel(*refs, has_vres):
    if has_vres:
        (p_ref, vf_ref, mu_ref, w0_ref, wup_ref, a0_ref, aup_ref, gup_ref, kk_ref, ka_ref, rk_ref,
         lw_ref, lb_ref, bd64_ref, v0_ref, vdn_ref, vup_ref, y_ref, state_ref, hist_ref) = refs
    else:
        (p_ref, mu_ref, w0_ref, wup_ref, a0_ref, aup_ref, gup_ref, kk_ref, ka_ref, rk_ref,
         lw_ref, lb_ref, bd64_ref, y_ref, v_ref, state_ref, hist_ref) = refs
    cs = y_ref.shape[1]
    heads = GROUP // HEAD
    ci = pl.program_id(1)

    @pl.when(ci == 0)
    def _():
        state_ref[...] = jnp.zeros_like(state_ref)
        hist_ref[...] = jnp.zeros_like(hist_ref)

    p = p_ref[0]
    xs = p + (_shift_rows(hist_ref[...], p, 1) - p) * mu_ref[...]
    hist_ref[...] = p[cs - HIST:, :]

    r = xs[:, 0:GROUP]
    k = xs[:, GROUP:2 * GROUP]
    v = xs[:, 2 * GROUP:3 * GROUP]
    lora = xs[:, 3 * GROUP:]
    w = -_softplus(-(w0_ref[...] + _dot(jnp.tanh(lora), wup_ref[...]))) - 0.5
    a = _sigmoid(a0_ref[...] + _dot(lora, aup_ref[...]))
    g = _dot(_sigmoid(lora), gup_ref[...])
    if has_vres:
        mix = _sigmoid(v0_ref[...] + _dot(_dot(v, vdn_ref[...]), vup_ref[...]))
        v = v + (vf_ref[0] - v) * mix
    else:
        v_ref[0] = v

    bd64 = bd64_ref[...]
    kk = k * kk_ref[...]
    kk = kk / jnp.maximum(jnp.sqrt(_seg_sum(kk * kk, bd64)), 1e-12)
    kmod = k * (1.0 + (a - 1.0) * ka_ref[...])
    log_decay = -jnp.exp(w)

    row = lax.broadcasted_iota(jnp.int32, (cs, cs), 0)
    col = lax.broadcasted_iota(jnp.int32, (cs, cs), 1)
    lower_incl = col <= row
    lower_strict = col < row
    cum = _dot_split3_lhs_exact(jnp.where(lower_incl, 1.0, 0.0).astype(BF16), log_decay)
    g_inc = jnp.exp(cum)
    g_inv = jnp.exp(-cum)
    kap = kk * jnp.exp(cum - log_decay)
    rt = r * g_inc
    bet = kk * a * g_inv
    kt = kmod * g_inv

    lhs = jnp.concatenate([kap, rt], axis=0)
    rhs = jnp.concatenate([bet, kt], axis=0).astype(BF16)
    masks =[_lane_mask(GROUP, h * HEAD, (h + 1) * HEAD) for h in range(heads)]
    t_inv, a_k, b_r, k_r = [], [], [], []
    for h in range(heads):
        gram = _dot_nt(jnp.where(masks[h], lhs, 0.0), rhs)
        a_b = jnp.where(lower_strict, gram[0:cs, 0:cs], 0.0)
        a_k.append(jnp.where(lower_strict, gram[0:cs, cs:], 0.0))
        b_r.append(jnp.where(lower_incl, gram[cs:, 0:cs], 0.0))
        k_r.append(jnp.where(lower_incl, gram[cs:, cs:], 0.0))
        t_inv.append(_unit_lower_inverse(a_b, row, col))

    def stack_heads(x):
        return jnp.concatenate([jnp.where(m, x, 0.0) for m in masks], axis=0).astype(BF16)

    cat = lambda xs_: jnp.concatenate(xs_, axis=1).astype(BF16)
    state = state_ref[...]
    v_stack = stack_heads(v)
    rhs_u = _dot(kap, state) + jnp.dot(cat(a_k), v_stack, preferred_element_type=F32)
    u = jnp.dot(cat(t_inv), stack_heads(rhs_u), preferred_element_type=F32)
    y = (_dot(rt, state) + jnp.dot(cat(k_r), v_stack, preferred_element_type=F32)
         - jnp.dot(cat(b_r), stack_heads(u), preferred_element_type=F32))

    upd = _dot(jnp.transpose(kt), v) - _dot(jnp.transpose(bet), u)
    bd_mask = bd64 > 0
    g_end = jnp.transpose(g_inc)[:, cs - 1:cs]
    state_ref[...] = g_end * (state + jnp.where(bd_mask, upd, 0.0))

    mean = _seg_sum(y, bd64) * (1.0 / HEAD)
    yc = y - mean
    var = _seg_sum(yc * yc, bd64) * (1.0 / HEAD)
    yn = yc * lax.rsqrt(var + LNX_EPS) * lw_ref[...] + lb_ref[...]
    bonus = _seg_sum(r * kmod * rk_ref[...], bd64) * v
    y_ref[0] = ((yn + bonus) * g).astype(y_ref.dtype)


def _pad_rows(w, start, total):
    out = jnp.zeros((total, w.shape[1]), w.dtype)
    return lax.dynamic_update_slice(out, w, (start, 0))


def _mixer_rwkv7(p_d, mu, w0, w_up, a0, a_up, g_up, k_k, k_a, r_k, lnx_w, lnx_b, v_first, v_res):
    bsz, seq, _ = p_d.shape
    cs = min(RWKV_CHUNK, seq)
    lora = LORA_W + LORA_A + LORA_G
    row = lambda v: v.reshape(1, -1)
    has_vres = v_res is not None
    chunk = lambda n: pl.BlockSpec((1, cs, n), lambda b, i: (b, i, 0))
    args = [p_d]
    specs = [chunk(N_D)]
    if has_vres:
        args.append(v_first)
        specs.append(chunk(GROUP))
    args += [row(mu), row(w0), _pad_rows(w_up, 0, lora).astype(BF16),
             row(a0), _pad_rows(a_up, LORA_W, lora).astype(BF16),
             _pad_rows(g_up, LORA_W + LORA_A, lora).astype(BF16),
             row(k_k), row(k_a), row(r_k), row(lnx_w), row(lnx_b), _block_diag_ones(GROUP, HEAD)]
    specs += [_full((1, N_D)), _full((1, GROUP)), _full((lora, GROUP)),
              _full((1, GROUP)), _full((lora, GROUP)), _full((lora, GROUP)),
              _full((1, GROUP)), _full((1, GROUP)), _full((1, GROUP)), _full((1, GROUP)),
              _full((1, GROUP)), _full((GROUP, GROUP))]
    if has_vres:
        v0, v_down, v_up = v_res
        rank = v_down.shape[1]
        vdn = jnp.zeros((GROUP, lora), F32).at[:, :rank].set(v_down).astype(BF16)
        args += [row(v0), vdn, _pad_rows(v_up, 0, lora).astype(BF16)]
        specs += [_full((1, GROUP)), _full((GROUP, lora)), _full((lora, GROUP))]
        out_shape = jax.ShapeDtypeStruct((bsz, seq, GROUP), BF16)
        out_specs = chunk(GROUP)
    else:
        out_shape = [jax.ShapeDtypeStruct((bsz, seq, GROUP), BF16),
                     jax.ShapeDtypeStruct((bsz, seq, GROUP), F32)]
        out_specs = [chunk(GROUP), chunk(GROUP)]
    out = pl.pallas_call(
        functools.partial(_rwkv_kernel, has_vres=has_vres),
        out_shape=out_shape,
        grid=(bsz, seq // cs),
        in_specs=specs,
        out_specs=out_specs,
        scratch_shapes=[pltpu.VMEM((GROUP, GROUP), F32), pltpu.VMEM((HIST, N_D), F32)],
        compiler_params=pltpu.CompilerParams(dimension_semantics=("arbitrary", "arbitrary"),
                                             vmem_limit_bytes=VMEM_LIMIT),
        name="rwkv7_chunked",
    )(*args)
    if has_vres:
        return out, None
    return out[0], out[1]


def _out_ffn_kernel(x_ref, ya_ref, yb_ref, yc_ref, yd_ref, mod_ref, g_ref, wo_ref, wup_ref,
                    cw_ref, cb_ref, wdn_ref, o_ref, hist_ref):
    tm, d = x_ref.shape[1], x_ref.shape[2]
    d_ff = wdn_ref.shape[0]
    ti = pl.program_id(1)

    @pl.when(ti == 0)
    def _():
        hist_ref[...] = jnp.zeros_like(hist_ref)

    mod = mod_ref[0]
    gate1 = mod[:, 2 * d:3 * d]
    shift2, scale2, gate2 = mod[:, 3 * d:4 * d], mod[:, 4 * d:5 * d], mod[:, 5 * d:6 * d]
    mix = jnp.dot(ya_ref[0], wo_ref[0:GROUP, :], preferred_element_type=F32)
    for j, y_ref in enumerate((yb_ref, yc_ref, yd_ref), start=1):
        mix = mix + jnp.dot(y_ref[0], wo_ref[j * GROUP:(j + 1) * GROUP, :],
                            preferred_element_type=F32)
    x1 = x_ref[0] + gate1 * mix
    h = _modulated_norm(x1, g_ref[...], shift2, scale2).astype(BF16)

    def conv(u, lo):
        cw = cw_ref[:, lo:lo + FF_CHUNK]
        hist = hist_ref[:, lo:lo + FF_CHUNK]
        out = cw[FF_CONV - 1:FF_CONV] * u + cb_ref[:, lo:lo + FF_CHUNK]
        for dly in range(1, FF_CONV):
            out = out + cw[FF_CONV - 1 - dly:FF_CONV - dly] * _shift_rows(hist, u, dly)
        hist_ref[:, lo:lo + FF_CHUNK] = u[tm - HIST:, :]
        return out

    acc = jnp.zeros((tm, d), F32)
    for j in range(d_ff // FF_CHUNK):
        lo_g, lo_v = j * FF_CHUNK, d_ff + j * FF_CHUNK
        u_g = conv(jnp.dot(h, wup_ref[:, lo_g:lo_g + FF_CHUNK], preferred_element_type=F32), lo_g)
        u_v = conv(jnp.dot(h, wup_ref[:, lo_v:lo_v + FF_CHUNK], preferred_element_type=F32), lo_v)
        act = (u_g * _sigmoid(u_g) * u_v).astype(BF16)
        acc = acc + jnp.dot(act, wdn_ref[lo_g:lo_g + FF_CHUNK, :], preferred_element_type=F32)
    o_ref[0] = x1 + gate2 * acc


def _out_ffn(x, ys, mod, g2, w_out, w_up, conv_w, conv_b, w_down):
    bsz, seq, d = x.shape
    d_ff = w_down.shape[0]
    tm = min(ROW_TILE, seq)
    tile = lambda n: pl.BlockSpec((1, tm, n), lambda b, i: (b, i, 0))
    return pl.pallas_call(
        _out_ffn_kernel,
        out_shape=jax.ShapeDtypeStruct((bsz, seq, d), F32),
        grid=(bsz, seq // tm),
        in_specs=[tile(d), tile(GROUP), tile(GROUP), tile(GROUP), tile(GROUP),
                  pl.BlockSpec((1, 1, mod.shape[-1]), lambda b, i: (b, 0, 0)),
                  _full((1, d)), _resident((4 * GROUP, d)), _resident((d, 2 * d_ff)),
                  _full((FF_CONV, 2 * d_ff)), _full((1, 2 * d_ff)), _resident((d_ff, d))],
        out_specs=tile(d),
        scratch_shapes=[pltpu.VMEM((HIST, 2 * d_ff), F32)],
        compiler_params=pltpu.CompilerParams(dimension_semantics=("arbitrary", "arbitrary"),
                                             vmem_limit_bytes=VMEM_LIMIT),
        name="out_proj_ffn",
    )(x, *ys, mod, g2.reshape(1, d), w_out, w_up, conv_w, conv_b.reshape(1, -1), w_down)


def kernel(x, c, w_ada, b_ada, norm1_g, norm2_g, w_in, w_out, a_qnorm_g, a_knorm_g, a_lam_q, a_lam_k, a_out_g, b_out_g, c_conv_w, c_conv_b, c_gate_a_w, c_gate_a_b, c_gate_x_w, c_gate_x_b, c_lambda, c_out_g, d_mu, d_w0, d_w_up, d_a0, d_a_up, d_g_up, d_k_k, d_k_a, d_r_k, d_lnx_w, d_lnx_b, d_v0, d_v_down, d_v_up, ff_w_up, ff_conv_w, ff_conv_b, ff_w_down):
    depth = w_in.shape[0]
    bsz = x.shape[0]
    mods = _ada_modulation(c, w_ada, b_ada)
    v_first = None
    for l in range(depth):
        mod = mods[l].reshape(bsz, 1, -1)
        p_a, p_b, p_c, p_d = _in_proj(x, mod, norm1_g[l], w_in[l].astype(BF16))
        lam_init = 0.8 - 0.6 * math.exp(-0.3 * l)
        y_a = _mixer_diff_attn(p_a, a_qnorm_g[l], a_knorm_g[l], a_lam_q[l], a_lam_k[l],
                               a_out_g[l], lam_init)
        y_b = _mixer_stick_breaking(p_b, b_out_g[l])
        y_c = _mixer_rglru(p_c, c_conv_w[l], c_conv_b[l], c_gate_a_w[l], c_gate_a_b[l],
                           c_gate_x_w[l], c_gate_x_b[l], c_lambda[l], c_out_g[l])
        v_res = None if l == 0 else (d_v0[l - 1], d_v_down[l - 1], d_v_up[l - 1])
        y_d, v_d = _mixer_rwkv7(p_d, d_mu[l], d_w0[l], d_w_up[l], d_a0[l], d_a_up[l], d_g_up[l],
                                d_k_k[l], d_k_a[l], d_r_k[l].reshape(-1), d_lnx_w[l], d_lnx_b[l],
                                v_first, v_res)
        if l == 0:
            v_first = v_d
        x = _out_ffn(x, (y_a, y_b, y_c, y_d), mod, norm2_g[l], w_out[l].astype(BF16),
                     ff_w_up[l].astype(BF16), ff_conv_w[l], ff_conv_b[l],
                     ff_w_down[l].astype(BF16))
    return x
```

```python
import functools
import math

import numpy as np
import jax
import jax.numpy as jnp
from jax import lax
from jax.experimental import pallas as pl
from jax.experimental.pallas import tpu as pltpu

F32 = jnp.float32
BF16 = jnp.bfloat16

GROUP = 256
A_HEADS = 4
A_QK = 32
HEAD = 64
N_A = 3 * GROUP
N_B = 3 * GROUP
N_C = 2 * GROUP
LORA_W = 32
LORA_A = 32
LORA_G = 64
N_D = 3 * GROUP + LORA_W + LORA_A + LORA_G
C_CONV = 4
C_EXP = 8.0
FF_CONV = 3
LOG2_E = 1.4426950408889634
RMS_EPS = 1e-6
LNX_EPS = 64e-5
ADA_CHUNKS = 6

ROW_TILE = 512
ATT_TILE = 256
LRU_TILE = 256
RWKV_CHUNK = 128
RWKV_BATCH = 4
FF_CHUNK = 256
INV_BASE = 8
HIST = 8
VMEM_LIMIT = 56 * 1024 * 1024


def _dot(a, b):
    return jnp.dot(a.astype(BF16), b.astype(BF16), preferred_element_type=F32)


def _dot_nt(a, b):
    return lax.dot_general(a.astype(BF16), b.astype(BF16), (((1,), (1,)), ((), ())),
                           preferred_element_type=F32)


def _split2(x):
    hi = x.astype(BF16)
    lo = (x - hi.astype(F32)).astype(BF16)
    return hi, lo


def _dot_split2(x, m):
    hi, lo = _split2(x)
    return (jnp.dot(hi, m, preferred_element_type=F32)
            + jnp.dot(lo, m, preferred_element_type=F32))


def _dot_split3_lhs_exact(m, x):
    hi = x.astype(BF16)
    r1 = x - hi.astype(F32)
    mid = r1.astype(BF16)
    lo = (r1 - mid.astype(F32)).astype(BF16)
    return (jnp.dot(m, hi, preferred_element_type=F32)
            + jnp.dot(m, mid, preferred_element_type=F32)
            + jnp.dot(m, lo, preferred_element_type=F32))


def _sigmoid(x):
    return 1.0 / (1.0 + jnp.exp(-x))


def _softplus(x):
    return jnp.maximum(x, 0.0) + jnp.log1p(jnp.exp(-jnp.abs(x)))


def _lane_mask(width, lo, hi):
    lane = lax.broadcasted_iota(jnp.int32, (1, width), 1)
    return (lane >= lo) & (lane < hi)


def _shift_rows(hist, x, d):
    ext = jnp.concatenate([hist, x], axis=0)
    return pltpu.roll(ext, d, axis=0)[HIST:, :]


def _full(shape):
    nd = len(shape)
    return pl.BlockSpec(shape, lambda *_: (0,) * nd)


def _resident(shape):
    nd = len(shape)
    return pl.BlockSpec(shape, lambda *_: (0,) * nd, pipeline_mode=pl.Buffered(1))


def _block_diag_ones(width, block):
    idx = np.arange(width) // block
    return jnp.asarray((idx[:, None] == idx[None, :]).astype(np.float32), dtype=BF16)


def _ada_kernel(c_ref, w_ref, b_ref, o_ref):
    c = c_ref[...]
    cond = c * _sigmoid(c)
    o_ref[0] = _dot(cond, w_ref[0]) + b_ref[0]


def _ada_modulation(c, w_ada, b_ada):
    depth, d, n = w_ada.shape
    bsz = c.shape[0]
    tn = 1536
    return pl.pallas_call(
        _ada_kernel,
        out_shape=jax.ShapeDtypeStruct((depth, bsz, n), F32),
        grid=(depth, n // tn),
        in_specs=[pl.BlockSpec((bsz, d), lambda l, j: (0, 0)),
                  pl.BlockSpec((1, d, tn), lambda l, j: (l, 0, j)),
                  pl.BlockSpec((1, 1, tn), lambda l, j: (l, 0, j))],
        out_specs=pl.BlockSpec((1, bsz, tn), lambda l, j: (l, 0, j)),
        compiler_params=pltpu.CompilerParams(dimension_semantics=("arbitrary", "arbitrary"),
                                             vmem_limit_bytes=VMEM_LIMIT),
        name="ada_modulation",
    )(c, w_ada, b_ada.reshape(depth, 1, n))


def _modulated_norm(x, g, shift, scale):
    ms = jnp.mean(x * x, axis=-1, keepdims=True)
    return (x * lax.rsqrt(ms + RMS_EPS) * g) * (1.0 + scale) + shift


def _in_proj_kernel(x_ref, mod_ref, g_ref, w_ref, pa_ref, pb_ref, pc_ref, pd_ref):
    d = x_ref.shape[-1]
    mod = mod_ref[0]
    h = _modulated_norm(x_ref[0], g_ref[...], mod[:, 0:d], mod[:, d:2 * d]).astype(BF16)
    pa_ref[0] = jnp.dot(h, w_ref[:, 0:N_A], preferred_element_type=F32)
    pb_ref[0] = jnp.dot(h, w_ref[:, N_A:N_A + N_B], preferred_element_type=F32)
    pc_ref[0] = jnp.dot(h, w_ref[:, N_A + N_B:N_A + N_B + N_C], preferred_element_type=F32)
    pd_ref[0] = jnp.dot(h, w_ref[:, N_A + N_B + N_C:], preferred_element_type=F32)


def _in_proj(x, mod, g, w_bf16):
    bsz, seq, d = x.shape
    n_in = w_bf16.shape[1]
    tm = min(ROW_TILE, seq)
    widths = (N_A, N_B, N_C, N_D)
    return pl.pallas_call(
        _in_proj_kernel,
        out_shape=[jax.ShapeDtypeStruct((bsz, seq, n), F32) for n in widths],
        grid=(bsz, seq // tm),
        in_specs=[pl.BlockSpec((1, tm, d), lambda b, i: (b, i, 0)),
                  pl.BlockSpec((1, 1, mod.shape[-1]), lambda b, i: (b, 0, 0)),
                  _full((1, d)),
                  _resident((d, n_in))],
        out_specs=[pl.BlockSpec((1, tm, n), lambda b, i: (b, i, 0)) for n in widths],
        compiler_params=pltpu.CompilerParams(dimension_semantics=("arbitrary", "arbitrary"),
                                             vmem_limit_bytes=VMEM_LIMIT),
        name="in_proj",
    )(x, mod, g.reshape(1, d), w_bf16)


def _group_rms(x, bd, group, gain):
    ms = _dot_split2(x * x, bd) * (1.0 / group)
    return x * lax.rsqrt(ms + RMS_EPS) * gain


def _attn_a_kernel(p_ref, qg_ref, kg_ref, lq_ref, lk_ref, og_ref, bd32_ref, bd64_ref,
                   o_ref, kn_ref, *, lam_init):
    tq = o_ref.shape[1]
    qi = pl.program_id(1)
    bd32 = bd32_ref[...]

    @pl.when(qi == 0)
    def _():
        k = p_ref[0, :, GROUP:2 * GROUP]
        kn_ref[...] = _group_rms(k, bd32, A_QK, kg_ref[...]).astype(BF16)

    lq = lq_ref[...]
    lk = lk_ref[...]
    lam = (jnp.exp(jnp.sum(lq[0:1] * lk[0:1], axis=-1, keepdims=True))
           - jnp.exp(jnp.sum(lq[1:2] * lk[1:2], axis=-1, keepdims=True)) + lam_init)

    q0 = pl.multiple_of(qi * tq, tq)
    q = p_ref[0, pl.ds(q0, tq), 0:GROUP]
    qn = _group_rms(q, bd32, A_QK, qg_ref[...]) * (A_QK ** -0.5 * LOG2_E)

    row = lax.broadcasted_iota(jnp.int32, (tq, tq), 0)
    col = lax.broadcasted_iota(jnp.int32, (tq, tq), 1)
    causal = col <= row

    vmasks = [_lane_mask(GROUP, h * HEAD, (h + 1) * HEAD) for h in range(A_HEADS)]
    qms = [[jnp.where(_lane_mask(GROUP, h * HEAD + c * A_QK, h * HEAD + (c + 1) * A_QK), qn, 0.0
                      ).astype(BF16) for h in range(A_HEADS)] for c in range(2)]

    def per_head_lanes(cols):
        out = cols[-1]
        for h in range(A_HEADS - 2, -1, -1):
            out = jnp.where(vmasks[h], cols[h], out)
        return out

    def key_tile(k0, carry, diag):
        kb = kn_ref[pl.ds(k0, tq), :]
        v = p_ref[0, pl.ds(k0, tq), 2 * GROUP:3 * GROUP]
        v_stack = jnp.concatenate([jnp.where(vm, v, 0.0) for vm in vmasks], axis=0).astype(BF16)
        new = []
        for c in range(2):
            ms, ls, acc = carry[c]
            ms_new, ls_new, alphas, ps = [], [], [], []
            for h in range(A_HEADS):
                s = _dot_nt(qms[c][h], kb)
                if diag:
                    s = jnp.where(causal, s, -jnp.inf)
                m_new = jnp.maximum(ms[h], jnp.max(s, axis=-1, keepdims=True))
                alpha = jnp.exp2(ms[h] - m_new)
                p = jnp.exp2(s - m_new)
                ms_new.append(m_new)
                ls_new.append(alpha * ls[h] + jnp.sum(p, axis=-1, keepdims=True))
                alphas.append(alpha)
                ps.append(p.astype(BF16))
            acc = per_head_lanes(alphas) * acc + jnp.dot(
                jnp.concatenate(ps, axis=1), v_stack, preferred_element_type=F32)
            new.append((tuple(ms_new), tuple(ls_new), acc))
        return tuple(new)

    init_c = (tuple(jnp.full((tq, 1), -jnp.inf, F32) for _ in range(A_HEADS)),
              tuple(jnp.zeros((tq, 1), F32) for _ in range(A_HEADS)),
              jnp.zeros((tq, GROUP), F32))
    carry = lax.fori_loop(
        0, qi, lambda kj, carry: key_tile(pl.multiple_of(kj * tq, tq), carry, False),
        (init_c, init_c))
    (_, l0, acc0), (_, l1, acc1) = key_tile(q0, carry, True)
    out = (acc0 * per_head_lanes([1.0 / l for l in l0])
           - lam * (acc1 * per_head_lanes([1.0 / l for l in l1])))

    y = _group_rms(out, bd64_ref[...], HEAD, og_ref[...]) * (1.0 - lam_init)
    o_ref[0] = y.astype(o_ref.dtype)


def _mixer_diff_attn(p_a, q_g, k_g, lam_q, lam_k, out_g, lam_init):
    bsz, seq, _ = p_a.shape
    tq = min(ATT_TILE, seq)
    reps = GROUP // A_QK
    return pl.pallas_call(
        functools.partial(_attn_a_kernel, lam_init=lam_init),
        out_shape=jax.ShapeDtypeStruct((bsz, seq, GROUP), BF16),
        grid=(bsz, seq // tq),
        in_specs=[pl.BlockSpec((1, seq, N_A), lambda b, i: (b, 0, 0)),
                  _full((1, GROUP)), _full((1, GROUP)),
                  _full((2, A_QK)), _full((2, A_QK)),
                  _full((1, GROUP)),
                  _full((GROUP, GROUP)), _full((GROUP, GROUP))],
        out_specs=pl.BlockSpec((1, tq, GROUP), lambda b, i: (b, i, 0)),
        scratch_shapes=[pltpu.VMEM((seq, GROUP), BF16)],
        compiler_params=pltpu.CompilerParams(dimension_semantics=("arbitrary", "arbitrary"),
                                             vmem_limit_bytes=VMEM_LIMIT),
        name="diff_attention",
    )(p_a, jnp.tile(q_g, reps).reshape(1, GROUP), jnp.tile(k_g, reps).reshape(1, GROUP),
      lam_q, lam_k, out_g.reshape(1, GROUP),
      _block_diag_ones(GROUP, A_QK), _block_diag_ones(GROUP, HEAD))


def _attn_b_kernel(p_ref, og_ref, bd64_ref, tri_ref, o_ref):
    tq = o_ref.shape[1]
    qi = pl.program_id(1)
    q0 = pl.multiple_of(qi * tq, tq)
    q = p_ref[0, pl.ds(q0, tq), 0:GROUP] * (HEAD ** -0.5 * LOG2_E)
    tri = tri_ref[...]
    row = lax.broadcasted_iota(jnp.int32, (tq, tq), 0)
    col = lax.broadcasted_iota(jnp.int32, (tq, tq), 1)
    strict = col < row

    heads = GROUP // HEAD
    vmasks = [_lane_mask(GROUP, h * HEAD, (h + 1) * HEAD) for h in range(heads)]
    qms = [jnp.where(vm, q, 0.0).astype(BF16) for vm in vmasks]

    def key_tile(k0, carry, diag):
        laters, acc = carry
        kb = p_ref[0, pl.ds(k0, tq), GROUP:2 * GROUP].astype(BF16)
        v = p_ref[0, pl.ds(k0, tq), 2 * GROUP:3 * GROUP]
        v_stack = jnp.concatenate([jnp.where(vm, v, 0.0) for vm in vmasks], axis=0).astype(BF16)
        laters_new, ws = [], []
        for h in range(heads):
            z = _dot_nt(qms[h], kb)
            sp = jnp.maximum(z, 0.0) + jnp.log2(1.0 + jnp.exp2(-jnp.abs(z)))
            log_keep = -sp
            log_beta = z - sp
            if diag:
                log_keep = jnp.where(strict, log_keep, 0.0)
            incl = _dot_split2(log_keep, tri)
            w = jnp.exp2(log_beta + (incl - log_keep) + laters[h])
            if diag:
                w = jnp.where(strict, w, 0.0)
            ws.append(w.astype(BF16))
            laters_new.append(laters[h] + incl[:, 0:1])
        acc = acc + jnp.dot(jnp.concatenate(ws, axis=1), v_stack, preferred_element_type=F32)
        return tuple(laters_new), acc

    carry = key_tile(q0, (tuple(jnp.zeros((tq, 1), F32) for _ in range(heads)),
                          jnp.zeros((tq, GROUP), F32)), True)
    _, acc = lax.fori_loop(
        0, qi, lambda i, carry: key_tile(pl.multiple_of((qi - 1 - i) * tq, tq), carry, False),
        carry)

    o_ref[0] = _group_rms(acc, bd64_ref[...], HEAD, og_ref[...]).astype(o_ref.dtype)


def _mixer_stick_breaking(p_b, out_g):
    bsz, seq, _ = p_b.shape
    tq = min(ATT_TILE, seq)
    tri = jnp.asarray(np.tril(np.ones((tq, tq), np.float32)), dtype=BF16)
    return pl.pallas_call(
        _attn_b_kernel,
        out_shape=jax.ShapeDtypeStruct((bsz, seq, GROUP), BF16),
        grid=(bsz, seq // tq),
        in_specs=[pl.BlockSpec((1, seq, N_B), lambda b, i: (b, 0, 0)),
                  _full((1, GROUP)), _full((GROUP, GROUP)), _full((tq, tq))],
        out_specs=pl.BlockSpec((1, tq, GROUP), lambda b, i: (b, i, 0)),
        compiler_params=pltpu.CompilerParams(dimension_semantics=("arbitrary", "arbitrary"),
                                             vmem_limit_bytes=VMEM_LIMIT),
        name="stick_breaking_attention",
    )(p_b, out_g.reshape(1, GROUP), _block_diag_ones(GROUP, HEAD), tri)


def _gelu_tanh(x):
    return 0.5 * x * (1.0 + jnp.tanh(math.sqrt(2.0 / math.pi) * (x + 0.044715 * (x * x * x))))


def _rglru_kernel(p_ref, cw_ref, cb_ref, gaw_ref, gab_ref, gxw_ref, gxb_ref, lam_ref, og_ref,
                  bd64_ref, o_ref, hist_ref, h_ref):
    tt = o_ref.shape[1]
    ti = pl.program_id(1)

    @pl.when(ti == 0)
    def _():
        hist_ref[...] = jnp.zeros_like(hist_ref)
        h_ref[...] = jnp.zeros_like(h_ref)

    x_raw = p_ref[0, :, 0:GROUP]
    x_gate = p_ref[0, :, GROUP:2 * GROUP]
    hist = hist_ref[...]
    cw = cw_ref[...]
    x = cw[C_CONV - 1:C_CONV] * x_raw + cb_ref[...]
    for d in range(1, C_CONV):
        x = x + cw[C_CONV - 1 - d:C_CONV - d] * _shift_rows(hist, x_raw, d)
    hist_ref[...] = x_raw[tt - HIST:, :]

    r = _sigmoid(_dot(x, gaw_ref[...]) + gab_ref[...])
    i = _sigmoid(_dot(x, gxw_ref[...]) + gxb_ref[...])
    log_a = (-C_EXP) * r * _softplus(-lam_ref[...])
    a = jnp.exp(log_a)
    mult = jnp.sqrt(jnp.tanh(-log_a) * (a * a + 1.0))
    row = lax.broadcasted_iota(jnp.int32, (tt, 1), 0)
    mult = jnp.where((row == 0) & (ti == 0), 1.0, mult)
    u = mult * i * x

    d = 1
    while d < tt:
        a_prev = pltpu.roll(a, d, axis=0)
        u_prev = pltpu.roll(u, d, axis=0)
        keep = row >= d
        u = jnp.where(keep, a * u_prev + u, u)
        a = jnp.where(keep, a * a_prev, a)
        d *= 2
    h = u + a * h_ref[0:1, :]
    h_ref[...] = jnp.broadcast_to(h[tt - 1:tt, :], h_ref.shape)

    y = h * _gelu_tanh(x_gate)
    o_ref[0] = _group_rms(y, bd64_ref[...], HEAD, og_ref[...]).astype(o_ref.dtype)


def _block_diag_weight(w):
    nb, d, _ = w.shape
    eye = jnp.eye(nb, dtype=w.dtype)
    return (eye[:, None, :, None] * w[:, :, None, :]).reshape(nb * d, nb * d)


def _mixer_rglru(p_c, conv_w, conv_b, ga_w, ga_b, gx_w, gx_b, lam, out_g):
    bsz, seq, _ = p_c.shape
    tt = min(LRU_TILE, seq)
    row = lambda v: v.reshape(1, GROUP)
    return pl.pallas_call(
        _rglru_kernel,
        out_shape=jax.ShapeDtypeStruct((bsz, seq, GROUP), BF16),
        grid=(bsz, seq // tt),
        in_specs=[pl.BlockSpec((1, tt, N_C), lambda b, i: (b, i, 0)),
                  _full((C_CONV, GROUP)), _full((1, GROUP)),
                  _full((GROUP, GROUP)), _full((1, GROUP)),
                  _full((GROUP, GROUP)), _full((1, GROUP)),
                  _full((1, GROUP)), _full((1, GROUP)), _full((GROUP, GROUP))],
        out_specs=pl.BlockSpec((1, tt, GROUP), lambda b, i: (b, i, 0)),
        scratch_shapes=[pltpu.VMEM((HIST, GROUP), F32), pltpu.VMEM((HIST, GROUP), F32)],
        compiler_params=pltpu.CompilerParams(dimension_semantics=("arbitrary", "arbitrary"),
                                             vmem_limit_bytes=VMEM_LIMIT),
        name="rg_lru",
    )(p_c, conv_w, row(conv_b), _block_diag_weight(ga_w).astype(BF16), row(ga_b),
      _block_diag_weight(gx_w).astype(BF16), row(gx_b), row(lam), row(out_g),
      _block_diag_ones(GROUP, HEAD))


def _seg_sum(x, bd):
    return _dot_split2(x, bd)


def _unit_lower_inverses(mats, row, col):
    n = mats[0].shape[0]

    def same_block(size):
        shift = size.bit_length() - 1
        return (row >> shift) == (col >> shift)

    base = same_block(INV_BASE)
    eye = (row == col).astype(F32)
    powers = [jnp.where(base, -a, 0.0) for a in mats]
    ts = [eye + p for p in powers]
    k = 1
    while 2 * k < INV_BASE:
        powers = [_dot(p, p) for p in powers]
        ts = [t + _dot(t, p) for t, p in zip(ts, powers)]
        k *= 2
    size = INV_BASE
    while size < n:
        sel = same_block(2 * size) & jnp.logical_not(same_block(size))
        halves = [_dot(t, jnp.where(sel, a, 0.0)) for t, a in zip(ts, mats)]
        ts = [t - _dot(half, t) for t, half in zip(ts, halves)]
        size *= 2
    return ts


def _rwkv_kernel(*refs, has_vres):
    if has_vres:
        (p_ref, vf_ref, mu_ref, w0_ref, wup_ref, a0_ref, aup_ref, gup_ref, kk_ref, ka_ref, rk_ref,
         lw_ref, lb_ref, bd64_ref, v0_ref, vdn_ref, vup_ref, y_ref, state_ref, hist_ref) = refs
    else:
        (p_ref, mu_ref, w0_ref, wup_ref, a0_ref, aup_ref, gup_ref, kk_ref, ka_ref, rk_ref,
         lw_ref, lb_ref, bd64_ref, y_ref, v_ref, state_ref, hist_ref) = refs
    nb, cs = y_ref.shape[0], y_ref.shape[1]
    heads = GROUP // HEAD
    ci = pl.program_id(1)

    @pl.when(ci == 0)
    def _():
        state_ref[...] = jnp.zeros_like(state_ref)
        hist_ref[...] = jnp.zeros_like(hist_ref)

    row = lax.broadcasted_iota(jnp.int32, (cs, cs), 0)
    col = lax.broadcasted_iota(jnp.int32, (cs, cs), 1)
    lower_incl = col <= row
    lower_strict = col < row
    masks = [_lane_mask(GROUP, h * HEAD, (h + 1) * HEAD) for h in range(heads)]
    batch = range(nb)
    rows = lambda x, i: x[i * cs:(i + 1) * cs, :]

    ps = [p_ref[i] for i in batch]
    shifted = [_shift_rows(hist_ref[i], ps[i], 1) for i in batch]
    for i in batch:
        hist_ref[i] = ps[i][cs - HIST:, :]
    p = jnp.concatenate(ps, axis=0)
    xs = p + (jnp.concatenate(shifted, axis=0) - p) * mu_ref[...]

    r = xs[:, 0:GROUP]
    k = xs[:, GROUP:2 * GROUP]
    v = xs[:, 2 * GROUP:3 * GROUP]
    lora = xs[:, 3 * GROUP:]
    w = -_softplus(-(w0_ref[...] + _dot(jnp.tanh(lora), wup_ref[...]))) - 0.5
    a = _sigmoid(a0_ref[...] + _dot(lora, aup_ref[...]))
    g = _dot(_sigmoid(lora), gup_ref[...])
    if has_vres:
        mix = _sigmoid(v0_ref[...] + _dot(_dot(v, vdn_ref[...]), vup_ref[...]))
        v = v + (jnp.concatenate([vf_ref[i] for i in batch], axis=0) - v) * mix
    else:
        for i in batch:
            v_ref[i] = rows(v, i)

    bd64 = bd64_ref[...]
    kk = k * kk_ref[...]
    kk = kk / jnp.maximum(jnp.sqrt(_seg_sum(kk * kk, bd64)), 1e-12)
    kmod = k * (1.0 + (a - 1.0) * ka_ref[...])
    log_decay = -jnp.exp(w)

    cum_wide = _dot_split3_lhs_exact(jnp.where(lower_incl, 1.0, 0.0).astype(BF16),
                                     jnp.concatenate([rows(log_decay, i) for i in batch], axis=1))
    cum = jnp.concatenate([cum_wide[:, i * GROUP:(i + 1) * GROUP] for i in batch], axis=0)
    g_inc = jnp.exp(cum)
    g_inv = jnp.exp(-cum)
    kap = kk * jnp.exp(cum - log_decay)
    rt = r * g_inc
    bet = kk * a * g_inv
    kt = kmod * g_inv

    lhs = [jnp.concatenate([rows(kap, i), rows(rt, i)], axis=0) for i in batch]
    rhs = [jnp.concatenate([rows(bet, i), rows(kt, i)], axis=0).astype(BF16) for i in batch]
    pairs = [(i, h) for i in batch for h in range(heads)]
    grams = [_dot_nt(jnp.where(masks[h], lhs[i], 0.0), rhs[i]) for i, h in pairs]
    a_b = [jnp.where(lower_strict, gm[0:cs, 0:cs], 0.0) for gm in grams]
    a_k = [jnp.where(lower_strict, gm[0:cs, cs:], 0.0).astype(BF16) for gm in grams]
    b_r = [jnp.where(lower_incl, gm[cs:, 0:cs], 0.0).astype(BF16) for gm in grams]
    k_r = [jnp.where(lower_incl, gm[cs:, cs:], 0.0).astype(BF16) for gm in grams]
    t_inv = [t.astype(BF16) for t in _unit_lower_inverses(a_b, row, col)]

    def stack_heads(x):
        return jnp.concatenate([jnp.where(m, x, 0.0) for m in masks], axis=0).astype(BF16)

    per_row = lambda mats, i: jnp.concatenate(mats[i * heads:(i + 1) * heads], axis=1)
    dotf = lambda x, y: jnp.dot(x, y, preferred_element_type=F32)
    states = [state_ref[i] for i in batch]
    states_bf = [s.astype(BF16) for s in states]
    v_stack = [stack_heads(rows(v, i)) for i in batch]
    rhs_u = [_dot(rows(kap, i), states_bf[i]) + dotf(per_row(a_k, i), v_stack[i]) for i in batch]
    u = [dotf(per_row(t_inv, i), stack_heads(rhs_u[i])) for i in batch]
    y = [_dot(rows(rt, i), states_bf[i]) + dotf(per_row(k_r, i), v_stack[i])
         - dotf(per_row(b_r, i), stack_heads(u[i])) for i in batch]

    upd = [_dot(jnp.transpose(rows(kt, i)), rows(v, i)) - _dot(jnp.transpose(rows(bet, i)), u[i])
           for i in batch]
    bd_mask = bd64 > 0
    for i in batch:
        g_end = jnp.transpose(rows(g_inc, i))[:, cs - 1:cs]
        state_ref[i] = g_end * (states[i] + jnp.where(bd_mask, upd[i], 0.0))

    y = jnp.concatenate(y, axis=0)
    mean = _seg_sum(y, bd64) * (1.0 / HEAD)
    yc = y - mean
    var = _seg_sum(yc * yc, bd64) * (1.0 / HEAD)
    yn = yc * lax.rsqrt(var + LNX_EPS) * lw_ref[...] + lb_ref[...]
    bonus = _seg_sum(r * kmod * rk_ref[...], bd64) * v
    out = ((yn + bonus) * g).astype(y_ref.dtype)
    for i in batch:
        y_ref[i] = rows(out, i)


def _pad_rows(w, start, total):
    out = jnp.zeros((total, w.shape[1]), w.dtype)
    return lax.dynamic_update_slice(out, w, (start, 0))


def _mixer_rwkv7(p_d, mu, w0, w_up, a0, a_up, g_up, k_k, k_a, r_k, lnx_w, lnx_b, v_first, v_res):
    bsz, seq, _ = p_d.shape
    cs = min(RWKV_CHUNK, seq)
    lora = LORA_W + LORA_A + LORA_G
    row = lambda v: v.reshape(1, -1)
    has_vres = v_res is not None
    nb = math.gcd(RWKV_BATCH, bsz)
    chunk = lambda n: pl.BlockSpec((nb, cs, n), lambda b, i: (b, i, 0))
    args = [p_d]
    specs = [chunk(N_D)]
    if has_vres:
        args.append(v_first)
        specs.append(chunk(GROUP))
    args += [row(mu), row(w0), _pad_rows(w_up, 0, lora).astype(BF16),
             row(a0), _pad_rows(a_up, LORA_W, lora).astype(BF16),
             _pad_rows(g_up, LORA_W + LORA_A, lora).astype(BF16),
             row(k_k), row(k_a), row(r_k), row(lnx_w), row(lnx_b), _block_diag_ones(GROUP, HEAD)]
    specs += [_full((1, N_D)), _full((1, GROUP)), _full((lora, GROUP)),
              _full((1, GROUP)), _full((lora, GROUP)), _full((lora, GROUP)),
              _full((1, GROUP)), _full((1, GROUP)), _full((1, GROUP)), _full((1, GROUP)),
              _full((1, GROUP)), _full((GROUP, GROUP))]
    if has_vres:
        v0, v_down, v_up = v_res
        rank = v_down.shape[1]
        vdn = jnp.zeros((GROUP, lora), F32).at[:, :rank].set(v_down).astype(BF16)
        args += [row(v0), vdn, _pad_rows(v_up, 0, lora).astype(BF16)]
        specs += [_full((1, GROUP)), _full((GROUP, lora)), _full((lora, GROUP))]
        out_shape = jax.ShapeDtypeStruct((bsz, seq, GROUP), BF16)
        out_specs = chunk(GROUP)
    else:
        out_shape = [jax.ShapeDtypeStruct((bsz, seq, GROUP), BF16),
                     jax.ShapeDtypeStruct((bsz, seq, GROUP), F32)]
        out_specs = [chunk(GROUP), chunk(GROUP)]
    out = pl.pallas_call(
        functools.partial(_rwkv_kernel, has_vres=has_vres),
        out_shape=out_shape,
        grid=(bsz // nb, seq // cs),
        in_specs=specs,
        out_specs=out_specs,
        scratch_shapes=[pltpu.VMEM((nb, GROUP, GROUP), F32), pltpu.VMEM((nb, HIST, N_D), F32)],
        compiler_params=pltpu.CompilerParams(dimension_semantics=("arbitrary", "arbitrary"),
                                             vmem_limit_bytes=VMEM_LIMIT),
        name="rwkv7_chunked",
    )(*args)
    if has_vres:
        return out, None
    return out[0], out[1]


def _out_ffn_kernel(x_ref, ya_ref, yb_ref, yc_ref, yd_ref, mod_ref, g_ref, wo_ref, wup_ref,
                    cw_ref, cb_ref, wdn_ref, o_ref, hist_ref):
    tm, d = x_ref.shape[1], x_ref.shape[2]
    d_ff = wdn_ref.shape[0]
    ti = pl.program_id(1)

    @pl.when(ti == 0)
    def _():
        hist_ref[...] = jnp.zeros_like(hist_ref)

    mod = mod_ref[0]
    gate1 = mod[:, 2 * d:3 * d]
    shift2, scale2, gate2 = mod[:, 3 * d:4 * d], mod[:, 4 * d:5 * d], mod[:, 5 * d:6 * d]
    mix = jnp.dot(ya_ref[0], wo_ref[0:GROUP, :], preferred_element_type=F32)
    for j, y_ref in enumerate((yb_ref, yc_ref, yd_ref), start=1):
        mix = mix + jnp.dot(y_ref[0], wo_ref[j * GROUP:(j + 1) * GROUP, :],
                            preferred_element_type=F32)
    x1 = x_ref[0] + gate1 * mix
    h = _modulated_norm(x1, g_ref[...], shift2, scale2).astype(BF16)

    def conv(u, lo):
        cw = cw_ref[:, lo:lo + FF_CHUNK]
        hist = hist_ref[:, lo:lo + FF_CHUNK]
        out = cw[FF_CONV - 1:FF_CONV] * u + cb_ref[:, lo:lo + FF_CHUNK]
        for dly in range(1, FF_CONV):
            out = out + cw[FF_CONV - 1 - dly:FF_CONV - dly] * _shift_rows(hist, u, dly)
        hist_ref[:, lo:lo + FF_CHUNK] = u[tm - HIST:, :]
        return out

    acc = jnp.zeros((tm, d), F32)
    for j in range(d_ff // FF_CHUNK):
        lo_g, lo_v = j * FF_CHUNK, d_ff + j * FF_CHUNK
        u_g = conv(jnp.dot(h, wup_ref[:, lo_g:lo_g + FF_CHUNK], preferred_element_type=F32), lo_g)
        u_v = conv(jnp.dot(h, wup_ref[:, lo_v:lo_v + FF_CHUNK], preferred_element_type=F32), lo_v)
        act = (u_g * _sigmoid(u_g) * u_v).astype(BF16)
        acc = acc + jnp.dot(act, wdn_ref[lo_g:lo_g + FF_CHUNK, :], preferred_element_type=F32)
    o_ref[0] = x1 + gate2 * acc


def _out_ffn(x, ys, mod, g2, w_out, w_up, conv_w, conv_b, w_down):
    bsz, seq, d = x.shape
    d_ff = w_down.shape[0]
    tm = min(ROW_TILE, seq)
    tile = lambda n: pl.BlockSpec((1, tm, n), lambda b, i: (b, i, 0))
    return pl.pallas_call(
        _out_ffn_kernel,
        out_shape=jax.ShapeDtypeStruct((bsz, seq, d), F32),
        grid=(bsz, seq // tm),
        in_specs=[tile(d), tile(GROUP), tile(GROUP), tile(GROUP), tile(GROUP),
                  pl.BlockSpec((1, 1, mod.shape[-1]), lambda b, i: (b, 0, 0)),
                  _full((1, d)), _resident((4 * GROUP, d)), _resident((d, 2 * d_ff)),
                  _full((FF_CONV, 2 * d_ff)), _full((1, 2 * d_ff)), _resident((d_ff, d))],
        out_specs=tile(d),
        scratch_shapes=[pltpu.VMEM((HIST, 2 * d_ff), F32)],
        compiler_params=pltpu.CompilerParams(dimension_semantics=("arbitrary", "arbitrary"),
                                             vmem_limit_bytes=VMEM_LIMIT),
        name="out_proj_ffn",
    )(x, *ys, mod, g2.reshape(1, d), w_out, w_up, conv_w, conv_b.reshape(1, -1), w_down)


def kernel(x, c, w_ada, b_ada, norm1_g, norm2_g, w_in, w_out, a_qnorm_g, a_knorm_g, a_lam_q, a_lam_k, a_out_g, b_out_g, c_conv_w, c_conv_b, c_gate_a_w, c_gate_a_b, c_gate_x_w, c_gate_x_b, c_lambda, c_out_g, d_mu, d_w0, d_w_up, d_a0, d_a_up, d_g_up, d_k_k, d_k_a, d_r_k, d_lnx_w, d_lnx_b, d_v0, d_v_down, d_v_up, ff_w_up, ff_conv_w, ff_conv_b, ff_w_down):
    depth = w_in.shape[0]
    bsz = x.shape[0]
    mods = _ada_modulation(c, w_ada, b_ada)
    v_first = None
    for l in range(depth):
        mod = mods[l].reshape(bsz, 1, -1)
        p_a, p_b, p_c, p_d = _in_proj(x, mod, norm1_g[l], w_in[l].astype(BF16))
        lam_init = 0.8 - 0.6 * math.exp(-0.3 * l)
        y_a = _mixer_diff_attn(p_a, a_qnorm_g[l], a_knorm_g[l], a_lam_q[l], a_lam_k[l],
                               a_out_g[l], lam_init)
        y_b = _mixer_stick_breaking(p_b, b_out_g[l])
        y_c = _mixer_rglru(p_c, c_conv_w[l], c_conv_b[l], c_gate_a_w[l], c_gate_a_b[l],
                           c_gate_x_w[l], c_gate_x_b[l], c_lambda[l], c_out_g[l])
        v_res = None if l == 0 else (d_v0[l - 1], d_v_down[l - 1], d_v_up[l - 1])
        y_d, v_d = _mixer_rwkv7(p_d, d_mu[l], d_w0[l], d_w_up[l], d_a0[l], d_a_up[l], d_g_up[l],
                                d_k_k[l], d_k_a[l], d_r_k[l].reshape(-1), d_lnx_w[l], d_lnx_b[l],
                                v_first, v_res)
        if l == 0:
            v_first = v_d
        x = _out_ffn(x, (y_a, y_b, y_c, y_d), mod, norm2_g[l], w_out[l].astype(BF16),
                     ff_w_up[l].astype(BF16), ff_conv_w[l], ff_conv_b[l],
                     ff_w_down[l].astype(BF16))
    return x
```

```python
import functools
import math

import numpy as np
import jax
import jax.numpy as jnp
from jax import lax
from jax.experimental import pallas as pl
from jax.experimental.pallas import tpu as pltpu

F32 = jnp.float32
BF16 = jnp.bfloat16

GROUP = 256
A_HEADS = 4
A_QK = 32
HEAD = 64
N_A = 3 * GROUP
N_B = 3 * GROUP
N_C = 2 * GROUP
LORA_W = 32
LORA_A = 32
LORA_G = 64
N_D = 3 * GROUP + LORA_W + LORA_A + LORA_G
C_CONV = 4
C_EXP = 8.0
FF_CONV = 3
LOG2_E = 1.4426950408889634
RMS_EPS = 1e-6
LNX_EPS = 64e-5
ADA_CHUNKS = 6

ROW_TILE = 512
ATT_TILE = 256
LRU_TILE = 256
RWKV_CHUNK = 128
RWKV_BATCH = 4
FF_CHUNK = 256
SUM_ROWS = 16
INV_BASE = 8
HIST = 8
VMEM_LIMIT = 56 * 1024 * 1024


def _dot(a, b):
    return jnp.dot(a.astype(BF16), b.astype(BF16), preferred_element_type=F32)


def _dot_nt(a, b):
    return lax.dot_general(a.astype(BF16), b.astype(BF16), (((1,), (1,)), ((), ())),
                           preferred_element_type=F32)


def _split2(x):
    hi = x.astype(BF16)
    lo = (x - hi.astype(F32)).astype(BF16)
    return hi, lo


def _dot_split2(x, m):
    hi, lo = _split2(x)
    return (jnp.dot(hi, m, preferred_element_type=F32)
            + jnp.dot(lo, m, preferred_element_type=F32))


def _dot_split3_lhs_exact(m, x):
    hi = x.astype(BF16)
    r1 = x - hi.astype(F32)
    mid = r1.astype(BF16)
    lo = (r1 - mid.astype(F32)).astype(BF16)
    return (jnp.dot(m, hi, preferred_element_type=F32)
            + jnp.dot(m, mid, preferred_element_type=F32)
            + jnp.dot(m, lo, preferred_element_type=F32))


def _sigmoid(x):
    return 1.0 / (1.0 + jnp.exp(-x))


def _softplus(x):
    return jnp.maximum(x, 0.0) + jnp.log1p(jnp.exp(-jnp.abs(x)))


def _lane_mask(width, lo, hi):
    lane = lax.broadcasted_iota(jnp.int32, (1, width), 1)
    return (lane >= lo) & (lane < hi)


def _shift_rows(hist, x, d):
    ext = jnp.concatenate([hist, x], axis=0)
    return pltpu.roll(ext, d, axis=0)[HIST:, :]


def _full(shape):
    nd = len(shape)
    return pl.BlockSpec(shape, lambda *_: (0,) * nd)


def _resident(shape):
    nd = len(shape)
    return pl.BlockSpec(shape, lambda *_: (0,) * nd, pipeline_mode=pl.Buffered(1))


def _block_diag_ones(width, block):
    idx = np.arange(width) // block
    return jnp.asarray((idx[:, None] == idx[None, :]).astype(np.float32), dtype=BF16)


def _ada_kernel(c_ref, w_ref, b_ref, o_ref):
    c = c_ref[...]
    cond = c * _sigmoid(c)
    o_ref[0] = _dot(cond, w_ref[0]) + b_ref[0]


def _ada_modulation(c, w_ada, b_ada):
    depth, d, n = w_ada.shape
    bsz = c.shape[0]
    tn = 1536
    return pl.pallas_call(
        _ada_kernel,
        out_shape=jax.ShapeDtypeStruct((depth, bsz, n), F32),
        grid=(depth, n // tn),
        in_specs=[pl.BlockSpec((bsz, d), lambda l, j: (0, 0)),
                  pl.BlockSpec((1, d, tn), lambda l, j: (l, 0, j)),
                  pl.BlockSpec((1, 1, tn), lambda l, j: (l, 0, j))],
        out_specs=pl.BlockSpec((1, bsz, tn), lambda l, j: (l, 0, j)),
        compiler_params=pltpu.CompilerParams(dimension_semantics=("arbitrary", "arbitrary"),
                                             vmem_limit_bytes=VMEM_LIMIT),
        name="ada_modulation",
    )(c, w_ada, b_ada.reshape(depth, 1, n))


def _modulated_norm(x, g, shift, scale):
    ms = jnp.mean(x * x, axis=-1, keepdims=True)
    return (x * lax.rsqrt(ms + RMS_EPS) * g) * (1.0 + scale) + shift


def _in_proj_kernel(x_ref, mod_ref, g_ref, w_ref, pa_ref, pb_ref, pc_ref, pd_ref):
    d = x_ref.shape[-1]
    mod = mod_ref[0]
    h = _modulated_norm(x_ref[0], g_ref[...], mod[:, 0:d], mod[:, d:2 * d]).astype(BF16)
    pa_ref[0] = jnp.dot(h, w_ref[:, 0:N_A], preferred_element_type=F32)
    pb_ref[0] = jnp.dot(h, w_ref[:, N_A:N_A + N_B], preferred_element_type=F32)
    pc_ref[0] = jnp.dot(h, w_ref[:, N_A + N_B:N_A + N_B + N_C], preferred_element_type=F32)
    pd_ref[0] = jnp.dot(h, w_ref[:, N_A + N_B + N_C:], preferred_element_type=F32)


def _in_proj(x, mod, g, w_bf16):
    bsz, seq, d = x.shape
    n_in = w_bf16.shape[1]
    tm = min(ROW_TILE, seq)
    widths = (N_A, N_B, N_C, N_D)
    return pl.pallas_call(
        _in_proj_kernel,
        out_shape=[jax.ShapeDtypeStruct((bsz, seq, n), F32) for n in widths],
        grid=(bsz, seq // tm),
        in_specs=[pl.BlockSpec((1, tm, d), lambda b, i: (b, i, 0)),
                  pl.BlockSpec((1, 1, mod.shape[-1]), lambda b, i: (b, 0, 0)),
                  _full((1, d)),
                  _resident((d, n_in))],
        out_specs=[pl.BlockSpec((1, tm, n), lambda b, i: (b, i, 0)) for n in widths],
        compiler_params=pltpu.CompilerParams(dimension_semantics=("arbitrary", "arbitrary"),
                                             vmem_limit_bytes=VMEM_LIMIT),
        name="in_proj",
    )(x, mod, g.reshape(1, d), w_bf16)


def _group_rms(x, bd, group, gain):
    ms = _dot_split2(x * x, bd) * (1.0 / group)
    return x * lax.rsqrt(ms + RMS_EPS) * gain


def _attn_a_kernel(p_ref, qg_ref, kg_ref, lq_ref, lk_ref, og_ref, bd32_ref,
                   o_ref, kn_ref, vt_ref, *, lam_init):
    tq = o_ref.shape[1]
    n_kt = vt_ref.shape[0]
    qi = pl.program_id(1)
    bd32 = bd32_ref[...]

    @pl.when(qi == 0)
    def _():
        k = p_ref[0, :, GROUP:2 * GROUP]
        kn_ref[...] = _group_rms(k, bd32, A_QK, kg_ref[...]).astype(BF16)
        ones_rows = jnp.ones((SUM_ROWS, tq), BF16)
        for j in range(n_kt):
            vt = jnp.transpose(p_ref[0, j * tq:(j + 1) * tq, 2 * GROUP:3 * GROUP])
            for h in range(A_HEADS):
                vt_ref[j, h, 0:HEAD, :] = vt[h * HEAD:(h + 1) * HEAD, :].astype(BF16)
                vt_ref[j, h, HEAD:, :] = ones_rows

    lq = lq_ref[...]
    lk = lk_ref[...]
    lam = (jnp.exp(jnp.sum(lq[0:1] * lk[0:1], axis=-1, keepdims=True))
           - jnp.exp(jnp.sum(lq[1:2] * lk[1:2], axis=-1, keepdims=True)) + lam_init)

    q0 = pl.multiple_of(qi * tq, tq)
    q = p_ref[0, pl.ds(q0, tq), 0:GROUP]
    qn = _group_rms(q, bd32, A_QK, qg_ref[...]) * (A_QK ** -0.5 * LOG2_E)

    key_idx = lax.broadcasted_iota(jnp.int32, (tq, tq), 0)
    query_idx = lax.broadcasted_iota(jnp.int32, (tq, tq), 1)
    causal = key_idx <= query_idx

    qt = jnp.transpose(qn)
    feat = lax.broadcasted_iota(jnp.int32, (GROUP, 1), 0)
    qms = [[jnp.where((feat >= h * HEAD + c * A_QK) & (feat < h * HEAD + (c + 1) * A_QK), qt, 0.0
                      ).astype(BF16) for h in range(A_HEADS)] for c in range(2)]

    def key_tile(kj, carry, diag):
        kb = kn_ref[pl.ds(pl.multiple_of(kj * tq, tq), tq), :]
        dotf = lambda x, y: jnp.dot(x, y, preferred_element_type=F32)
        chains = [(c, h) for c in range(2) for h in range(A_HEADS)]
        ms = [carry[c][0][h] for c, h in chains]
        ls = [carry[c][1][h] for c, h in chains]
        accs = [carry[c][2][h] for c, h in chains]
        ss = [dotf(kb, qms[c][h]) for c, h in chains]
        if diag:
            ss = [jnp.where(causal, s, -jnp.inf) for s in ss]
        ms_new = [jnp.maximum(m, jnp.max(s, axis=0, keepdims=True)) for m, s in zip(ms, ss)]
        alphas = [jnp.exp2(m - m_new) for m, m_new in zip(ms, ms_new)]
        ps = [jnp.exp2(s - m_new).astype(BF16) for s, m_new in zip(ss, ms_new)]
        res = [dotf(vt_ref[kj, h], p) for (c, h), p in zip(chains, ps)]
        accs_new = [alpha * acc + r[0:HEAD, :] for alpha, acc, r in zip(alphas, accs, res)]
        ls_new = [alpha * l + r[HEAD:HEAD + 1, :] for alpha, l, r in zip(alphas, ls, res)]
        n = A_HEADS
        return tuple((tuple(ms_new[c * n:(c + 1) * n]), tuple(ls_new[c * n:(c + 1) * n]),
                      tuple(accs_new[c * n:(c + 1) * n])) for c in range(2))

    init_c = (tuple(jnp.full((1, tq), -jnp.inf, F32) for _ in range(A_HEADS)),
              tuple(jnp.zeros((1, tq), F32) for _ in range(A_HEADS)),
              tuple(jnp.zeros((HEAD, tq), F32) for _ in range(A_HEADS)))
    carry = lax.fori_loop(0, qi, lambda kj, carry: key_tile(kj, carry, False), (init_c, init_c))
    (_, l0, acc0), (_, l1, acc1) = key_tile(qi, carry, True)

    og = og_ref[...]
    ys = []
    for h in range(A_HEADS):
        o = acc0[h] * (1.0 / l0[h]) - lam * (acc1[h] * (1.0 / l1[h]))
        ms = jnp.mean(o * o, axis=0, keepdims=True)
        ys.append(o * lax.rsqrt(ms + RMS_EPS) * og[h * HEAD:(h + 1) * HEAD, :])
    y = jnp.transpose(jnp.concatenate(ys, axis=0)) * (1.0 - lam_init)
    o_ref[0] = y.astype(o_ref.dtype)


def _mixer_diff_attn(p_a, q_g, k_g, lam_q, lam_k, out_g, lam_init):
    bsz, seq, _ = p_a.shape
    tq = min(ATT_TILE, seq)
    reps = GROUP // A_QK
    return pl.pallas_call(
        functools.partial(_attn_a_kernel, lam_init=lam_init),
        out_shape=jax.ShapeDtypeStruct((bsz, seq, GROUP), BF16),
        grid=(bsz, seq // tq),
        in_specs=[pl.BlockSpec((1, seq, N_A), lambda b, i: (b, 0, 0)),
                  _full((1, GROUP)), _full((1, GROUP)),
                  _full((2, A_QK)), _full((2, A_QK)),
                  _full((GROUP, 1)),
                  _full((GROUP, GROUP))],
        out_specs=pl.BlockSpec((1, tq, GROUP), lambda b, i: (b, i, 0)),
        scratch_shapes=[pltpu.VMEM((seq, GROUP), BF16),
                        pltpu.VMEM((seq // tq, A_HEADS, HEAD + SUM_ROWS, tq), BF16)],
        compiler_params=pltpu.CompilerParams(dimension_semantics=("arbitrary", "arbitrary"),
                                             vmem_limit_bytes=VMEM_LIMIT),
        name="diff_attention",
    )(p_a, jnp.tile(q_g, reps).reshape(1, GROUP), jnp.tile(k_g, reps).reshape(1, GROUP),
      lam_q, lam_k, out_g.reshape(GROUP, 1), _block_diag_ones(GROUP, A_QK))


def _attn_b_kernel(p_ref, og_ref, tri_ref, o_ref, kb_ref, vt_ref):
    tq = o_ref.shape[1]
    n_kt = vt_ref.shape[0]
    heads = GROUP // HEAD
    qi = pl.program_id(1)
    feat = lax.broadcasted_iota(jnp.int32, (GROUP, 1), 0)
    head_rows = [(feat >= h * HEAD) & (feat < (h + 1) * HEAD) for h in range(heads)]

    @pl.when(qi == 0)
    def _():
        kb_ref[...] = p_ref[0, :, GROUP:2 * GROUP].astype(BF16)
        for j in range(n_kt):
            vt = jnp.transpose(p_ref[0, j * tq:(j + 1) * tq, 2 * GROUP:3 * GROUP])
            vt_ref[j] = vt.astype(BF16)

    q0 = pl.multiple_of(qi * tq, tq)
    qt = jnp.transpose(p_ref[0, pl.ds(q0, tq), 0:GROUP] * (HEAD ** -0.5 * LOG2_E))
    qms = [jnp.where(hr, qt, 0.0).astype(BF16) for hr in head_rows]
    tri = tri_ref[...]
    key_idx = lax.broadcasted_iota(jnp.int32, (tq, tq), 0)
    query_idx = lax.broadcasted_iota(jnp.int32, (tq, tq), 1)
    strict = key_idx < query_idx
    dotf = lambda x, y: jnp.dot(x, y, preferred_element_type=F32)

    def key_tile(kj, carry, diag):
        laters, accs = carry
        kb = kb_ref[pl.ds(pl.multiple_of(kj * tq, tq), tq), :]
        zs = [dotf(kb, qm) for qm in qms]
        neg_keeps = [jnp.maximum(z, 0.0) + jnp.log2(1.0 + jnp.exp2(-jnp.abs(z))) for z in zs]
        if diag:
            neg_keeps = [jnp.where(strict, nk, 0.0) for nk in neg_keeps]
        incls = [dotf(tri, jnp.concatenate(_split2(nk), axis=0)) for nk in neg_keeps]
        ws = [jnp.exp2(z + incl + later) for z, incl, later in zip(zs, incls, laters)]
        if diag:
            ws = [jnp.where(strict, w, 0.0) for w in ws]
        accs = tuple(acc + dotf(vt_ref[kj, h * HEAD:(h + 1) * HEAD, :], w.astype(BF16))
                     for h, (acc, w) in enumerate(zip(accs, ws)))
        return tuple(later + incl[0:1, :] for later, incl in zip(laters, incls)), accs

    carry = key_tile(qi, (tuple(jnp.zeros((1, tq), F32) for _ in range(heads)),
                          tuple(jnp.zeros((HEAD, tq), F32) for _ in range(heads))), True)
    _, accs = lax.fori_loop(0, qi, lambda i, carry: key_tile(qi - 1 - i, carry, False), carry)

    og = og_ref[...]
    ys = []
    for h in range(heads):
        o = accs[h]
        ms = jnp.mean(o * o, axis=0, keepdims=True)
        ys.append(o * lax.rsqrt(ms + RMS_EPS) * og[h * HEAD:(h + 1) * HEAD, :])
    o_ref[0] = jnp.transpose(jnp.concatenate(ys, axis=0)).astype(o_ref.dtype)


def _mixer_stick_breaking(p_b, out_g):
    bsz, seq, _ = p_b.shape
    tq = min(ATT_TILE, seq)
    neg_upper = -np.triu(np.ones((tq, tq), np.float32))
    tri = jnp.asarray(np.concatenate([neg_upper, neg_upper], axis=1), dtype=BF16)
    return pl.pallas_call(
        _attn_b_kernel,
        out_shape=jax.ShapeDtypeStruct((bsz, seq, GROUP), BF16),
        grid=(bsz, seq // tq),
        in_specs=[pl.BlockSpec((1, seq, N_B), lambda b, i: (b, 0, 0)),
                  _full((GROUP, 1)), _full((tq, 2 * tq))],
        out_specs=pl.BlockSpec((1, tq, GROUP), lambda b, i: (b, i, 0)),
        scratch_shapes=[pltpu.VMEM((seq, GROUP), BF16),
                        pltpu.VMEM((seq // tq, GROUP, tq), BF16)],
        compiler_params=pltpu.CompilerParams(dimension_semantics=("arbitrary", "arbitrary"),
                                             vmem_limit_bytes=VMEM_LIMIT),
        name="stick_breaking_attention",
    )(p_b, out_g.reshape(GROUP, 1), tri)


def _gelu_tanh(x):
    return 0.5 * x * (1.0 + jnp.tanh(math.sqrt(2.0 / math.pi) * (x + 0.044715 * (x * x * x))))


def _rglru_kernel(p_ref, cw_ref, cb_ref, gaw_ref, gab_ref, gxw_ref, gxb_ref, lam_ref, og_ref,
                  bd64_ref, o_ref, hist_ref, h_ref):
    tt = o_ref.shape[1]
    ti = pl.program_id(1)

    @pl.when(ti == 0)
    def _():
        hist_ref[...] = jnp.zeros_like(hist_ref)
        h_ref[...] = jnp.zeros_like(h_ref)

    x_raw = p_ref[0, :, 0:GROUP]
    x_gate = p_ref[0, :, GROUP:2 * GROUP]
    hist = hist_ref[...]
    cw = cw_ref[...]
    x = cw[C_CONV - 1:C_CONV] * x_raw + cb_ref[...]
    for d in range(1, C_CONV):
        x = x + cw[C_CONV - 1 - d:C_CONV - d] * _shift_rows(hist, x_raw, d)
    hist_ref[...] = x_raw[tt - HIST:, :]

    r = _sigmoid(_dot(x, gaw_ref[...]) + gab_ref[...])
    i = _sigmoid(_dot(x, gxw_ref[...]) + gxb_ref[...])
    log_a = (-C_EXP) * r * _softplus(-lam_ref[...])
    a = jnp.exp(log_a)
    mult = jnp.sqrt(jnp.tanh(-log_a) * (a * a + 1.0))
    row = lax.broadcasted_iota(jnp.int32, (tt, 1), 0)
    mult = jnp.where((row == 0) & (ti == 0), 1.0, mult)
    u = mult * i * x

    d = 1
    while d < tt:
        a_prev = pltpu.roll(a, d, axis=0)
        u_prev = pltpu.roll(u, d, axis=0)
        keep = row >= d
        u = jnp.where(keep, a * u_prev + u, u)
        a = jnp.where(keep, a * a_prev, a)
        d *= 2
    h = u + a * h_ref[0:1, :]
    h_ref[...] = jnp.broadcast_to(h[tt - 1:tt, :], h_ref.shape)

    y = h * _gelu_tanh(x_gate)
    o_ref[0] = _group_rms(y, bd64_ref[...], HEAD, og_ref[...]).astype(o_ref.dtype)


def _block_diag_weight(w):
    nb, d, _ = w.shape
    eye = jnp.eye(nb, dtype=w.dtype)
    return (eye[:, None, :, None] * w[:, :, None, :]).reshape(nb * d, nb * d)


def _mixer_rglru(p_c, conv_w, conv_b, ga_w, ga_b, gx_w, gx_b, lam, out_g):
    bsz, seq, _ = p_c.shape
    tt = min(LRU_TILE, seq)
    row = lambda v: v.reshape(1, GROUP)
    return pl.pallas_call(
        _rglru_kernel,
        out_shape=jax.ShapeDtypeStruct((bsz, seq, GROUP), BF16),
        grid=(bsz, seq // tt),
        in_specs=[pl.BlockSpec((1, tt, N_C), lambda b, i: (b, i, 0)),
                  _full((C_CONV, GROUP)), _full((1, GROUP)),
                  _full((GROUP, GROUP)), _full((1, GROUP)),
                  _full((GROUP, GROUP)), _full((1, GROUP)),
                  _full((1, GROUP)), _full((1, GROUP)), _full((GROUP, GROUP))],
        out_specs=pl.BlockSpec((1, tt, GROUP), lambda b, i: (b, i, 0)),
        scratch_shapes=[pltpu.VMEM((HIST, GROUP), F32), pltpu.VMEM((HIST, GROUP), F32)],
        compiler_params=pltpu.CompilerParams(dimension_semantics=("arbitrary", "arbitrary"),
                                             vmem_limit_bytes=VMEM_LIMIT),
        name="rg_lru",
    )(p_c, conv_w, row(conv_b), _block_diag_weight(ga_w).astype(BF16), row(ga_b),
      _block_diag_weight(gx_w).astype(BF16), row(gx_b), row(lam), row(out_g),
      _block_diag_ones(GROUP, HEAD))


def _seg_sum(x, bd):
    return _dot_split2(x, bd)


def _unit_lower_inverses(mats, row, col):
    n = mats[0].shape[0]

    def same_block(size):
        shift = size.bit_length() - 1
        return (row >> shift) == (col >> shift)

    base = same_block(INV_BASE)
    eye = (row == col).astype(F32)
    powers = [jnp.where(base, -a, 0.0) for a in mats]
    ts = [eye + p for p in powers]
    k = 1
    while 2 * k < INV_BASE:
        powers = [_dot(p, p) for p in powers]
        ts = [t + _dot(t, p) for t, p in zip(ts, powers)]
        k *= 2
    size = INV_BASE
    while size < n:
        sel = same_block(2 * size) & jnp.logical_not(same_block(size))
        halves = [_dot(t, jnp.where(sel, a, 0.0)) for t, a in zip(ts, mats)]
        ts = [t - _dot(half, t) for t, half in zip(ts, halves)]
        size *= 2
    return ts


def _rwkv_kernel(*refs, has_vres):
    if has_vres:
        (p_ref, vf_ref, mu_ref, w0_ref, wup_ref, a0_ref, aup_ref, gup_ref, kk_ref, ka_ref, rk_ref,
         lw_ref, lb_ref, bd64_ref, v0_ref, vdn_ref, vup_ref, y_ref, state_ref, hist_ref) = refs
    else:
        (p_ref, mu_ref, w0_ref, wup_ref, a0_ref, aup_ref, gup_ref, kk_ref, ka_ref, rk_ref,
         lw_ref, lb_ref, bd64_ref, y_ref, v_ref, state_ref, hist_ref) = refs
    nb, cs = y_ref.shape[0], y_ref.shape[1]
    heads = GROUP // HEAD
    ci = pl.program_id(1)

    @pl.when(ci == 0)
    def _():
        state_ref[...] = jnp.zeros_like(state_ref)
        hist_ref[...] = jnp.zeros_like(hist_ref)

    row = lax.broadcasted_iota(jnp.int32, (cs, cs), 0)
    col = lax.broadcasted_iota(jnp.int32, (cs, cs), 1)
    lower_incl = col <= row
    lower_strict = col < row
    masks = [_lane_mask(GROUP, h * HEAD, (h + 1) * HEAD) for h in range(heads)]
    batch = range(nb)
    rows = lambda x, i: x[i * cs:(i + 1) * cs, :]

    ps = [p_ref[i] for i in batch]
    shifted = [_shift_rows(hist_ref[i], ps[i], 1) for i in batch]
    for i in batch:
        hist_ref[i] = ps[i][cs - HIST:, :]
    p = jnp.concatenate(ps, axis=0)
    xs = p + (jnp.concatenate(shifted, axis=0) - p) * mu_ref[...]

    r = xs[:, 0:GROUP]
    k = xs[:, GROUP:2 * GROUP]
    v = xs[:, 2 * GROUP:3 * GROUP]
    lora = xs[:, 3 * GROUP:]
    w = -_softplus(-(w0_ref[...] + _dot(jnp.tanh(lora), wup_ref[...]))) - 0.5
    a = _sigmoid(a0_ref[...] + _dot(lora, aup_ref[...]))
    g = _dot(_sigmoid(lora), gup_ref[...])
    if has_vres:
        mix = _sigmoid(v0_ref[...] + _dot(_dot(v, vdn_ref[...]), vup_ref[...]))
        v = v + (jnp.concatenate([vf_ref[i] for i in batch], axis=0) - v) * mix
    else:
        for i in batch:
            v_ref[i] = rows(v, i)

    bd64 = bd64_ref[...]
    kk = k * kk_ref[...]
    kk = kk / jnp.maximum(jnp.sqrt(_seg_sum(kk * kk, bd64)), 1e-12)
    kmod = k * (1.0 + (a - 1.0) * ka_ref[...])
    log_decay = -jnp.exp(w)

    cum_wide = _dot_split3_lhs_exact(jnp.where(lower_incl, 1.0, 0.0).astype(BF16),
                                     jnp.concatenate([rows(log_decay, i) for i in batch], axis=1))
    cum = jnp.concatenate([cum_wide[:, i * GROUP:(i + 1) * GROUP] for i in batch], axis=0)
    g_inc = jnp.exp(cum)
    g_inv = jnp.exp(-cum)
    kap = kk * jnp.exp(cum - log_decay)
    rt = r * g_inc
    bet = kk * a * g_inv
    kt = kmod * g_inv

    lhs = [jnp.concatenate([rows(kap, i), rows(rt, i)], axis=0) for i in batch]
    rhs = [jnp.concatenate([rows(bet, i), rows(kt, i)], axis=0).astype(BF16) for i in batch]
    pairs = [(i, h) for i in batch for h in range(heads)]
    grams = [_dot_nt(jnp.where(masks[h], lhs[i], 0.0), rhs[i]) for i, h in pairs]
    a_b = [jnp.where(lower_strict, gm[0:cs, 0:cs], 0.0) for gm in grams]
    a_k = [jnp.where(lower_strict, gm[0:cs, cs:], 0.0).astype(BF16) for gm in grams]
    b_r = [jnp.where(lower_incl, gm[cs:, 0:cs], 0.0).astype(BF16) for gm in grams]
    k_r = [jnp.where(lower_incl, gm[cs:, cs:], 0.0).astype(BF16) for gm in grams]
    t_inv = [t.astype(BF16) for t in _unit_lower_inverses(a_b, row, col)]

    def stack_heads(x):
        return jnp.concatenate([jnp.where(m, x, 0.0) for m in masks], axis=0).astype(BF16)

    per_row = lambda mats, i: jnp.concatenate(mats[i * heads:(i + 1) * heads], axis=1)
    dotf = lambda x, y: jnp.dot(x, y, preferred_element_type=F32)
    states = [state_ref[i] for i in batch]
    states_bf = [s.astype(BF16) for s in states]
    v_stack = [stack_heads(rows(v, i)) for i in batch]
    rhs_u = [_dot(rows(kap, i), states_bf[i]) + dotf(per_row(a_k, i), v_stack[i]) for i in batch]
    u = [dotf(per_row(t_inv, i), stack_heads(rhs_u[i])) for i in batch]
    y = [_dot(rows(rt, i), states_bf[i]) + dotf(per_row(k_r, i), v_stack[i])
         - dotf(per_row(b_r, i), stack_heads(u[i])) for i in batch]

    upd = [_dot(jnp.transpose(rows(kt, i)), rows(v, i)) - _dot(jnp.transpose(rows(bet, i)), u[i])
           for i in batch]
    bd_mask = bd64 > 0
    for i in batch:
        g_end = jnp.transpose(rows(g_inc, i))[:, cs - 1:cs]
        state_ref[i] = g_end * (states[i] + jnp.where(bd_mask, upd[i], 0.0))

    y = jnp.concatenate(y, axis=0)
    mean = _seg_sum(y, bd64) * (1.0 / HEAD)
    yc = y - mean
    var = _seg_sum(yc * yc, bd64) * (1.0 / HEAD)
    yn = yc * lax.rsqrt(var + LNX_EPS) * lw_ref[...] + lb_ref[...]
    bonus = _seg_sum(r * kmod * rk_ref[...], bd64) * v
    out = ((yn + bonus) * g).astype(y_ref.dtype)
    for i in batch:
        y_ref[i] = rows(out, i)


def _pad_rows(w, start, total):
    out = jnp.zeros((total, w.shape[1]), w.dtype)
    return lax.dynamic_update_slice(out, w, (start, 0))


def _mixer_rwkv7(p_d, mu, w0, w_up, a0, a_up, g_up, k_k, k_a, r_k, lnx_w, lnx_b, v_first, v_res):
    bsz, seq, _ = p_d.shape
    cs = min(RWKV_CHUNK, seq)
    lora = LORA_W + LORA_A + LORA_G
    row = lambda v: v.reshape(1, -1)
    has_vres = v_res is not None
    nb = math.gcd(RWKV_BATCH, bsz)
    chunk = lambda n: pl.BlockSpec((nb, cs, n), lambda b, i: (b, i, 0))
    args = [p_d]
    specs = [chunk(N_D)]
    if has_vres:
        args.append(v_first)
        specs.append(chunk(GROUP))
    args += [row(mu), row(w0), _pad_rows(w_up, 0, lora).astype(BF16),
             row(a0), _pad_rows(a_up, LORA_W, lora).astype(BF16),
             _pad_rows(g_up, LORA_W + LORA_A, lora).astype(BF16),
             row(k_k), row(k_a), row(r_k), row(lnx_w), row(lnx_b), _block_diag_ones(GROUP, HEAD)]
    specs += [_full((1, N_D)), _full((1, GROUP)), _full((lora, GROUP)),
              _full((1, GROUP)), _full((lora, GROUP)), _full((lora, GROUP)),
              _full((1, GROUP)), _full((1, GROUP)), _full((1, GROUP)), _full((1, GROUP)),
              _full((1, GROUP)), _full((GROUP, GROUP))]
    if has_vres:
        v0, v_down, v_up = v_res
        rank = v_down.shape[1]
        vdn = jnp.zeros((GROUP, lora), F32).at[:, :rank].set(v_down).astype(BF16)
        args += [row(v0), vdn, _pad_rows(v_up, 0, lora).astype(BF16)]
        specs += [_full((1, GROUP)), _full((GROUP, lora)), _full((lora, GROUP))]
        out_shape = jax.ShapeDtypeStruct((bsz, seq, GROUP), BF16)
        out_specs = chunk(GROUP)
    else:
        out_shape = [jax.ShapeDtypeStruct((bsz, seq, GROUP), BF16),
                     jax.ShapeDtypeStruct((bsz, seq, GROUP), F32)]
        out_specs = [chunk(GROUP), chunk(GROUP)]
    out = pl.pallas_call(
        functools.partial(_rwkv_kernel, has_vres=has_vres),
        out_shape=out_shape,
        grid=(bsz // nb, seq // cs),
        in_specs=specs,
        out_specs=out_specs,
        scratch_shapes=[pltpu.VMEM((nb, GROUP, GROUP), F32), pltpu.VMEM((nb, HIST, N_D), F32)],
        compiler_params=pltpu.CompilerParams(dimension_semantics=("arbitrary", "arbitrary"),
                                             vmem_limit_bytes=VMEM_LIMIT),
        name="rwkv7_chunked",
    )(*args)
    if has_vres:
        return out, None
    return out[0], out[1]


def _out_ffn_kernel(x_ref, ya_ref, yb_ref, yc_ref, yd_ref, mod_ref, g_ref, wo_ref, wup_ref,
                    cw_ref, cb_ref, wdn_ref, o_ref, hist_ref):
    tm, d = x_ref.shape[1], x_ref.shape[2]
    d_ff = wdn_ref.shape[0]
    ti = pl.program_id(1)

    @pl.when(ti == 0)
    def _():
        hist_ref[...] = jnp.zeros_like(hist_ref)

    mod = mod_ref[0]
    gate1 = mod[:, 2 * d:3 * d]
    shift2, scale2, gate2 = mod[:, 3 * d:4 * d], mod[:, 4 * d:5 * d], mod[:, 5 * d:6 * d]
    mix = jnp.dot(ya_ref[0], wo_ref[0:GROUP, :], preferred_element_type=F32)
    for j, y_ref in enumerate((yb_ref, yc_ref, yd_ref), start=1):
        mix = mix + jnp.dot(y_ref[0], wo_ref[j * GROUP:(j + 1) * GROUP, :],
                            preferred_element_type=F32)
    x1 = x_ref[0] + gate1 * mix
    h = _modulated_norm(x1, g_ref[...], shift2, scale2).astype(BF16)

    def conv(u, lo):
        cw = cw_ref[:, lo:lo + FF_CHUNK]
        hist = hist_ref[:, lo:lo + FF_CHUNK]
        out = cw[FF_CONV - 1:FF_CONV] * u + cb_ref[:, lo:lo + FF_CHUNK]
        for dly in range(1, FF_CONV):
            out = out + cw[FF_CONV - 1 - dly:FF_CONV - dly] * _shift_rows(hist, u, dly)
        hist_ref[:, lo:lo + FF_CHUNK] = u[tm - HIST:, :]
        return out

    acc = jnp.zeros((tm, d), F32)
    for j in range(d_ff // FF_CHUNK):
        lo_g, lo_v = j * FF_CHUNK, d_ff + j * FF_CHUNK
        u_g = conv(jnp.dot(h, wup_ref[:, lo_g:lo_g + FF_CHUNK], preferred_element_type=F32), lo_g)
        u_v = conv(jnp.dot(h, wup_ref[:, lo_v:lo_v + FF_CHUNK], preferred_element_type=F32), lo_v)
        act = (u_g * _sigmoid(u_g) * u_v).astype(BF16)
        acc = acc + jnp.dot(act, wdn_ref[lo_g:lo_g + FF_CHUNK, :], preferred_element_type=F32)
    o_ref[0] = x1 + gate2 * acc


def _out_ffn(x, ys, mod, g2, w_out, w_up, conv_w, conv_b, w_down):
    bsz, seq, d = x.shape
    d_ff = w_down.shape[0]
    tm = min(ROW_TILE, seq)
    tile = lambda n: pl.BlockSpec((1, tm, n), lambda b, i: (b, i, 0))
    return pl.pallas_call(
        _out_ffn_kernel,
        out_shape=jax.ShapeDtypeStruct((bsz, seq, d), F32),
        grid=(bsz, seq // tm),
        in_specs=[tile(d), tile(GROUP), tile(GROUP), tile(GROUP), tile(GROUP),
                  pl.BlockSpec((1, 1, mod.shape[-1]), lambda b, i: (b, 0, 0)),
                  _full((1, d)), _resident((4 * GROUP, d)), _resident((d, 2 * d_ff)),
                  _full((FF_CONV, 2 * d_ff)), _full((1, 2 * d_ff)), _resident((d_ff, d))],
        out_specs=tile(d),
        scratch_shapes=[pltpu.VMEM((HIST, 2 * d_ff), F32)],
        compiler_params=pltpu.CompilerParams(dimension_semantics=("arbitrary", "arbitrary"),
                                             vmem_limit_bytes=VMEM_LIMIT),
        name="out_proj_ffn",
    )(x, *ys, mod, g2.reshape(1, d), w_out, w_up, conv_w, conv_b.reshape(1, -1), w_down)


def kernel(x, c, w_ada, b_ada, norm1_g, norm2_g, w_in, w_out, a_qnorm_g, a_knorm_g, a_lam_q, a_lam_k, a_out_g, b_out_g, c_conv_w, c_conv_b, c_gate_a_w, c_gate_a_b, c_gate_x_w, c_gate_x_b, c_lambda, c_out_g, d_mu, d_w0, d_w_up, d_a0, d_a_up, d_g_up, d_k_k, d_k_a, d_r_k, d_lnx_w, d_lnx_b, d_v0, d_v_down, d_v_up, ff_w_up, ff_conv_w, ff_conv_b, ff_w_down):
    depth = w_in.shape[0]
    bsz = x.shape[0]
    mods = _ada_modulation(c, w_ada, b_ada)
    v_first = None
    for l in range(depth):
        mod = mods[l].reshape(bsz, 1, -1)
        p_a, p_b, p_c, p_d = _in_proj(x, mod, norm1_g[l], w_in[l].astype(BF16))
        lam_init = 0.8 - 0.6 * math.exp(-0.3 * l)
        y_a = _mixer_diff_attn(p_a, a_qnorm_g[l], a_knorm_g[l], a_lam_q[l], a_lam_k[l],
                               a_out_g[l], lam_init)
        y_b = _mixer_stick_breaking(p_b, b_out_g[l])
        y_c = _mixer_rglru(p_c, c_conv_w[l], c_conv_b[l], c_gate_a_w[l], c_gate_a_b[l],
                           c_gate_x_w[l], c_gate_x_b[l], c_lambda[l], c_out_g[l])
        v_res = None if l == 0 else (d_v0[l - 1], d_v_down[l - 1], d_v_up[l - 1])
        y_d, v_d = _mixer_rwkv7(p_d, d_mu[l], d_w0[l], d_w_up[l], d_a0[l], d_a_up[l], d_g_up[l],
                                d_k_k[l], d_k_a[l], d_r_k[l].reshape(-1), d_lnx_w[l], d_lnx_b[l],
                                v_first, v_res)
        if l == 0:
            v_first = v_d
        x = _out_ffn(x, (y_a, y_b, y_c, y_d), mod, norm2_g[l], w_out[l].astype(BF16),
                     ff_w_up[l].astype(BF16), ff_conv_w[l], ff_conv_b[l],
                     ff_w_down[l].astype(BF16))
    return x
```

```python
import functools
import math

import numpy as np
import jax
import jax.numpy as jnp
from jax import lax
from jax.experimental import pallas as pl
from jax.experimental.pallas import tpu as pltpu

F32 = jnp.float32
BF16 = jnp.bfloat16

GROUP = 256
A_HEADS = 4
A_QK = 32
HEAD = 64
N_A = 3 * GROUP
N_B = 3 * GROUP
N_C = 2 * GROUP
LORA_W = 32
LORA_A = 32
LORA_G = 64
N_D = 3 * GROUP + LORA_W + LORA_A + LORA_G
C_CONV = 4
C_EXP = 8.0
FF_CONV = 3
LOG2_E = 1.4426950408889634
RMS_EPS = 1e-6
LNX_EPS = 64e-5
ADA_CHUNKS = 6

ROW_TILE = 512
ATT_TILE = 256
LRU_TILE = 256
RWKV_CHUNK = 128
RWKV_BATCH = 4
FF_CHUNK = 256
SUM_ROWS = 16
INV_BASE = 8
HIST = 8
VMEM_LIMIT = 56 * 1024 * 1024


def _dot(a, b):
    return jnp.dot(a.astype(BF16), b.astype(BF16), preferred_element_type=F32)


def _dot_nt(a, b):
    return lax.dot_general(a.astype(BF16), b.astype(BF16), (((1,), (1,)), ((), ())),
                           preferred_element_type=F32)


def _split2(x):
    hi = x.astype(BF16)
    lo = (x - hi.astype(F32)).astype(BF16)
    return hi, lo


def _dot_split2(x, m):
    hi, lo = _split2(x)
    return (jnp.dot(hi, m, preferred_element_type=F32)
            + jnp.dot(lo, m, preferred_element_type=F32))


def _dot_split3_lhs_exact(m, x):
    hi = x.astype(BF16)
    r1 = x - hi.astype(F32)
    mid = r1.astype(BF16)
    lo = (r1 - mid.astype(F32)).astype(BF16)
    return (jnp.dot(m, hi, preferred_element_type=F32)
            + jnp.dot(m, mid, preferred_element_type=F32)
            + jnp.dot(m, lo, preferred_element_type=F32))


def _sigmoid(x):
    return 1.0 / (1.0 + jnp.exp(-x))


def _softplus(x):
    return jnp.maximum(x, 0.0) + jnp.log1p(jnp.exp(-jnp.abs(x)))


def _lane_mask(width, lo, hi):
    lane = lax.broadcasted_iota(jnp.int32, (1, width), 1)
    return (lane >= lo) & (lane < hi)


def _shift_rows(hist, x, d):
    ext = jnp.concatenate([hist, x], axis=0)
    return pltpu.roll(ext, d, axis=0)[HIST:, :]


def _full(shape):
    nd = len(shape)
    return pl.BlockSpec(shape, lambda *_: (0,) * nd)


def _resident(shape):
    nd = len(shape)
    return pl.BlockSpec(shape, lambda *_: (0,) * nd, pipeline_mode=pl.Buffered(1))


def _block_diag_ones(width, block):
    idx = np.arange(width) // block
    return jnp.asarray((idx[:, None] == idx[None, :]).astype(np.float32), dtype=BF16)


def _ada_kernel(c_ref, w_ref, b_ref, o_ref):
    c = c_ref[...]
    cond = c * _sigmoid(c)
    o_ref[0] = _dot(cond, w_ref[0]) + b_ref[0]


def _ada_modulation(c, w_ada, b_ada):
    depth, d, n = w_ada.shape
    bsz = c.shape[0]
    tn = 1536
    return pl.pallas_call(
        _ada_kernel,
        out_shape=jax.ShapeDtypeStruct((depth, bsz, n), F32),
        grid=(depth, n // tn),
        in_specs=[pl.BlockSpec((bsz, d), lambda l, j: (0, 0)),
                  pl.BlockSpec((1, d, tn), lambda l, j: (l, 0, j)),
                  pl.BlockSpec((1, 1, tn), lambda l, j: (l, 0, j))],
        out_specs=pl.BlockSpec((1, bsz, tn), lambda l, j: (l, 0, j)),
        compiler_params=pltpu.CompilerParams(dimension_semantics=("arbitrary", "arbitrary"),
                                             vmem_limit_bytes=VMEM_LIMIT),
        name="ada_modulation",
    )(c, w_ada, b_ada.reshape(depth, 1, n))


def _modulated_norm(x, g, shift, scale):
    ms = jnp.mean(x * x, axis=-1, keepdims=True)
    return (x * lax.rsqrt(ms + RMS_EPS) * g) * (1.0 + scale) + shift


def _in_proj_kernel(x_ref, mod_ref, g_ref, w_ref, pa_ref, pb_ref, pc_ref, pd_ref):
    d = x_ref.shape[-1]
    mod = mod_ref[0]
    h = _modulated_norm(x_ref[0], g_ref[...], mod[:, 0:d], mod[:, d:2 * d]).astype(BF16)
    pa_ref[0] = jnp.dot(h, w_ref[:, 0:N_A], preferred_element_type=F32)
    pb_ref[0] = jnp.dot(h, w_ref[:, N_A:N_A + N_B], preferred_element_type=F32)
    pc_ref[0] = jnp.dot(h, w_ref[:, N_A + N_B:N_A + N_B + N_C], preferred_element_type=F32)
    pd_ref[0] = jnp.dot(h, w_ref[:, N_A + N_B + N_C:], preferred_element_type=F32)


def _in_proj(x, mod, g, w_bf16):
    bsz, seq, d = x.shape
    n_in = w_bf16.shape[1]
    tm = min(ROW_TILE, seq)
    widths = (N_A, N_B, N_C, N_D)
    return pl.pallas_call(
        _in_proj_kernel,
        out_shape=[jax.ShapeDtypeStruct((bsz, seq, n), F32) for n in widths],
        grid=(bsz, seq // tm),
        in_specs=[pl.BlockSpec((1, tm, d), lambda b, i: (b, i, 0)),
                  pl.BlockSpec((1, 1, mod.shape[-1]), lambda b, i: (b, 0, 0)),
                  _full((1, d)),
                  _resident((d, n_in))],
        out_specs=[pl.BlockSpec((1, tm, n), lambda b, i: (b, i, 0)) for n in widths],
        compiler_params=pltpu.CompilerParams(dimension_semantics=("arbitrary", "arbitrary"),
                                             vmem_limit_bytes=VMEM_LIMIT),
        name="in_proj",
    )(x, mod, g.reshape(1, d), w_bf16)


def _group_rms(x, bd, group, gain):
    ms = _dot_split2(x * x, bd) * (1.0 / group)
    return x * lax.rsqrt(ms + RMS_EPS) * gain


def _attn_a_setup(p_ref, qg_ref, kg_ref, lq_ref, lk_ref, og_ref, bd32_ref,
                  o_ref, kn_ref, vt_ref, lam_init):
    tq = o_ref.shape[1]
    n_kt = vt_ref.shape[0]
    qi = pl.program_id(1)
    bd32 = bd32_ref[...]

    @pl.when(qi == 0)
    def _():
        k = p_ref[0, :, GROUP:2 * GROUP]
        kn_ref[...] = _group_rms(k, bd32, A_QK, kg_ref[...]).astype(BF16)
        ones_rows = jnp.ones((SUM_ROWS, tq), BF16)
        for j in range(n_kt):
            vt = jnp.transpose(p_ref[0, j * tq:(j + 1) * tq, 2 * GROUP:3 * GROUP])
            for h in range(A_HEADS):
                vt_ref[j, h, 0:HEAD, :] = vt[h * HEAD:(h + 1) * HEAD, :].astype(BF16)
                vt_ref[j, h, HEAD:, :] = ones_rows

    lq = lq_ref[...]
    lk = lk_ref[...]
    lam = (jnp.exp(jnp.sum(lq[0:1] * lk[0:1], axis=-1, keepdims=True))
           - jnp.exp(jnp.sum(lq[1:2] * lk[1:2], axis=-1, keepdims=True)) + lam_init)

    q0 = pl.multiple_of(qi * tq, tq)
    q = p_ref[0, pl.ds(q0, tq), 0:GROUP]
    qn = _group_rms(q, bd32, A_QK, qg_ref[...]) * (A_QK ** -0.5 * LOG2_E)

    key_idx = lax.broadcasted_iota(jnp.int32, (tq, tq), 0)
    query_idx = lax.broadcasted_iota(jnp.int32, (tq, tq), 1)
    causal = key_idx <= query_idx

    qt = jnp.transpose(qn)
    feat = lax.broadcasted_iota(jnp.int32, (GROUP, 1), 0)
    qms = [[jnp.where((feat >= h * HEAD + c * A_QK) & (feat < h * HEAD + (c + 1) * A_QK), qt, 0.0
                      ).astype(BF16) for h in range(A_HEADS)] for c in range(2)]

    def key_tile(kj, carry, diag):
        kb = kn_ref[pl.ds(pl.multiple_of(kj * tq, tq), tq), :]
        dotf = lambda x, y: jnp.dot(x, y, preferred_element_type=F32)
        chains = [(c, h) for c in range(2) for h in range(A_HEADS)]
        ms = [carry[c][0][h] for c, h in chains]
        ls = [carry[c][1][h] for c, h in chains]
        accs = [carry[c][2][h] for c, h in chains]
        ss = [dotf(kb, qms[c][h]) for c, h in chains]
        yield
        if diag:
            ss = [jnp.where(causal, s, -jnp.inf) for s in ss]
        ms_new = [jnp.maximum(m, jnp.max(s, axis=0, keepdims=True)) for m, s in zip(ms, ss)]
        alphas = [jnp.exp2(m - m_new) for m, m_new in zip(ms, ms_new)]
        ps = [jnp.exp2(s - m_new).astype(BF16) for s, m_new in zip(ss, ms_new)]
        yield
        res = [dotf(vt_ref[kj, h], p) for (c, h), p in zip(chains, ps)]
        yield
        accs_new = [alpha * acc + r[0:HEAD, :] for alpha, acc, r in zip(alphas, accs, res)]
        ls_new = [alpha * l + r[HEAD:HEAD + 1, :] for alpha, l, r in zip(alphas, ls, res)]
        n = A_HEADS
        return tuple((tuple(ms_new[c * n:(c + 1) * n]), tuple(ls_new[c * n:(c + 1) * n]),
                      tuple(accs_new[c * n:(c + 1) * n])) for c in range(2))

    init_c = (tuple(jnp.full((1, tq), -jnp.inf, F32) for _ in range(A_HEADS)),
              tuple(jnp.zeros((1, tq), F32) for _ in range(A_HEADS)),
              tuple(jnp.zeros((HEAD, tq), F32) for _ in range(A_HEADS)))
    def finish(carry):
        (_, l0, acc0), (_, l1, acc1) = carry
        og = og_ref[...]
        ys = []
        for h in range(A_HEADS):
            o = acc0[h] * (1.0 / l0[h]) - lam * (acc1[h] * (1.0 / l1[h]))
            ms = jnp.mean(o * o, axis=0, keepdims=True)
            ys.append(o * lax.rsqrt(ms + RMS_EPS) * og[h * HEAD:(h + 1) * HEAD, :])
        y = jnp.transpose(jnp.concatenate(ys, axis=0)) * (1.0 - lam_init)
        o_ref[0] = y.astype(o_ref.dtype)

    return (init_c, init_c), key_tile, finish


def _run_steps(*gens):
    results = [None] * len(gens)
    live = list(range(len(gens)))
    while live:
        for idx in list(live):
            try:
                next(gens[idx])
            except StopIteration as done:
                results[idx] = done.value
                live.remove(idx)
    return results


def _attn_a_kernel(*refs, lam_init):
    qi = pl.program_id(1)
    init, key_tile, finish = _attn_a_setup(*refs, lam_init)
    carry = lax.fori_loop(0, qi, lambda kj, carry: _run_steps(key_tile(kj, carry, False))[0], init)
    finish(_run_steps(key_tile(qi, carry, True))[0])


def _mixer_diff_attn(p_a, q_g, k_g, lam_q, lam_k, out_g, lam_init):
    bsz, seq, _ = p_a.shape
    tq = min(ATT_TILE, seq)
    reps = GROUP // A_QK
    return pl.pallas_call(
        functools.partial(_attn_a_kernel, lam_init=lam_init),
        out_shape=jax.ShapeDtypeStruct((bsz, seq, GROUP), BF16),
        grid=(bsz, seq // tq),
        in_specs=[pl.BlockSpec((1, seq, N_A), lambda b, i: (b, 0, 0)),
                  _full((1, GROUP)), _full((1, GROUP)),
                  _full((2, A_QK)), _full((2, A_QK)),
                  _full((GROUP, 1)),
                  _full((GROUP, GROUP))],
        out_specs=pl.BlockSpec((1, tq, GROUP), lambda b, i: (b, i, 0)),
        scratch_shapes=[pltpu.VMEM((seq, GROUP), BF16),
                        pltpu.VMEM((seq // tq, A_HEADS, HEAD + SUM_ROWS, tq), BF16)],
        compiler_params=pltpu.CompilerParams(dimension_semantics=("arbitrary", "arbitrary"),
                                             vmem_limit_bytes=VMEM_LIMIT),
        name="diff_attention",
    )(p_a, jnp.tile(q_g, reps).reshape(1, GROUP), jnp.tile(k_g, reps).reshape(1, GROUP),
      lam_q, lam_k, out_g.reshape(GROUP, 1), _block_diag_ones(GROUP, A_QK))


def _attn_b_setup(p_ref, og_ref, tri_ref, o_ref, kb_ref, vt_ref):
    tq = o_ref.shape[1]
    n_kt = vt_ref.shape[0]
    heads = GROUP // HEAD
    qi = pl.program_id(1)
    feat = lax.broadcasted_iota(jnp.int32, (GROUP, 1), 0)
    head_rows = [(feat >= h * HEAD) & (feat < (h + 1) * HEAD) for h in range(heads)]

    @pl.when(qi == 0)
    def _():
        kb_ref[...] = p_ref[0, :, GROUP:2 * GROUP].astype(BF16)
        for j in range(n_kt):
            vt = jnp.transpose(p_ref[0, j * tq:(j + 1) * tq, 2 * GROUP:3 * GROUP])
            vt_ref[j] = vt.astype(BF16)

    q0 = pl.multiple_of(qi * tq, tq)
    qt = jnp.transpose(p_ref[0, pl.ds(q0, tq), 0:GROUP] * (HEAD ** -0.5 * LOG2_E))
    qms = [jnp.where(hr, qt, 0.0).astype(BF16) for hr in head_rows]
    tri = tri_ref[...]
    key_idx = lax.broadcasted_iota(jnp.int32, (tq, tq), 0)
    query_idx = lax.broadcasted_iota(jnp.int32, (tq, tq), 1)
    strict = key_idx < query_idx
    dotf = lambda x, y: jnp.dot(x, y, preferred_element_type=F32)

    def key_tile(kj, carry, diag):
        laters, accs = carry
        kb = kb_ref[pl.ds(pl.multiple_of(kj * tq, tq), tq), :]
        zs = [dotf(kb, qm) for qm in qms]
        yield
        neg_keeps = [jnp.maximum(z, 0.0) + jnp.log2(1.0 + jnp.exp2(-jnp.abs(z))) for z in zs]
        if diag:
            neg_keeps = [jnp.where(strict, nk, 0.0) for nk in neg_keeps]
        splits = [jnp.concatenate(_split2(nk), axis=0) for nk in neg_keeps]
        yield
        incls = [dotf(tri, sp) for sp in splits]
        yield
        ws = [jnp.exp2(z + incl + later) for z, incl, later in zip(zs, incls, laters)]
        if diag:
            ws = [jnp.where(strict, w, 0.0) for w in ws]
        yield
        accs = tuple(acc + dotf(vt_ref[kj, h * HEAD:(h + 1) * HEAD, :], w.astype(BF16))
                     for h, (acc, w) in enumerate(zip(accs, ws)))
        return tuple(later + incl[0:1, :] for later, incl in zip(laters, incls)), accs

    init = (tuple(jnp.zeros((1, tq), F32) for _ in range(heads)),
            tuple(jnp.zeros((HEAD, tq), F32) for _ in range(heads)))

    def finish(carry):
        _, accs = carry
        og = og_ref[...]
        ys = []
        for h in range(heads):
            o = accs[h]
            ms = jnp.mean(o * o, axis=0, keepdims=True)
            ys.append(o * lax.rsqrt(ms + RMS_EPS) * og[h * HEAD:(h + 1) * HEAD, :])
        o_ref[0] = jnp.transpose(jnp.concatenate(ys, axis=0)).astype(o_ref.dtype)

    return init, key_tile, finish


def _attn_b_kernel(*refs):
    qi = pl.program_id(1)
    init, key_tile, finish = _attn_b_setup(*refs)
    carry = _run_steps(key_tile(qi, init, True))[0]
    finish(lax.fori_loop(
        0, qi, lambda i, carry: _run_steps(key_tile(qi - 1 - i, carry, False))[0], carry))


def _attn_ab_kernel(pa_ref, qg_ref, kg_ref, lq_ref, lk_ref, oga_ref, bd32_ref, pb_ref, ogb_ref, tri_ref,
                    oa_ref, ob_ref, kn_ref, vta_ref, kb_ref, vtb_ref, *, lam_init):
    qi = pl.program_id(1)
    a_init, a_tile, a_finish = _attn_a_setup(pa_ref, qg_ref, kg_ref, lq_ref, lk_ref, oga_ref, bd32_ref,
                                             oa_ref, kn_ref, vta_ref, lam_init)
    b_init, b_tile, b_finish = _attn_b_setup(pb_ref, ogb_ref, tri_ref, ob_ref, kb_ref, vtb_ref)
    both = lambda kj, c, diag: tuple(_run_steps(a_tile(kj, c[0], diag), b_tile(kj, c[1], diag)))
    carry = both(qi, (a_init, b_init), True)
    carry = lax.fori_loop(0, qi, lambda i, c: both(qi - 1 - i, c, False), carry)
    a_finish(carry[0])
    b_finish(carry[1])


def _mixer_stick_breaking(p_b, out_g):
    bsz, seq, _ = p_b.shape
    tq = min(ATT_TILE, seq)
    neg_upper = -np.triu(np.ones((tq, tq), np.float32))
    tri = jnp.asarray(np.concatenate([neg_upper, neg_upper], axis=1), dtype=BF16)
    return pl.pallas_call(
        _attn_b_kernel,
        out_shape=jax.ShapeDtypeStruct((bsz, seq, GROUP), BF16),
        grid=(bsz, seq // tq),
        in_specs=[pl.BlockSpec((1, seq, N_B), lambda b, i: (b, 0, 0)),
                  _full((GROUP, 1)), _full((tq, 2 * tq))],
        out_specs=pl.BlockSpec((1, tq, GROUP), lambda b, i: (b, i, 0)),
        scratch_shapes=[pltpu.VMEM((seq, GROUP), BF16),
                        pltpu.VMEM((seq // tq, GROUP, tq), BF16)],
        compiler_params=pltpu.CompilerParams(dimension_semantics=("arbitrary", "arbitrary"),
                                             vmem_limit_bytes=VMEM_LIMIT),
        name="stick_breaking_attention",
    )(p_b, out_g.reshape(GROUP, 1), tri)


def _mixer_attention_pair(p_a, q_g, k_g, lam_q, lam_k, a_out_g, lam_init, p_b, b_out_g):
    bsz, seq, _ = p_a.shape
    tq = min(ATT_TILE, seq)
    reps = GROUP // A_QK
    neg_upper = -np.triu(np.ones((tq, tq), np.float32))
    tri = jnp.asarray(np.concatenate([neg_upper, neg_upper], axis=1), dtype=BF16)
    whole = lambda n: pl.BlockSpec((1, seq, n), lambda b, i: (b, 0, 0))
    tile = pl.BlockSpec((1, tq, GROUP), lambda b, i: (b, i, 0))
    return pl.pallas_call(
        functools.partial(_attn_ab_kernel, lam_init=lam_init),
        out_shape=[jax.ShapeDtypeStruct((bsz, seq, GROUP), BF16)] * 2,
        grid=(bsz, seq // tq),
        in_specs=[whole(N_A), _full((1, GROUP)), _full((1, GROUP)),
                  _full((2, A_QK)), _full((2, A_QK)), _full((GROUP, 1)), _full((GROUP, GROUP)),
                  whole(N_B), _full((GROUP, 1)), _full((tq, 2 * tq))],
        out_specs=[tile, tile],
        scratch_shapes=[pltpu.VMEM((seq, GROUP), BF16),
                        pltpu.VMEM((seq // tq, A_HEADS, HEAD + SUM_ROWS, tq), BF16),
                        pltpu.VMEM((seq, GROUP), BF16),
                        pltpu.VMEM((seq // tq, GROUP, tq), BF16)],
        compiler_params=pltpu.CompilerParams(dimension_semantics=("arbitrary", "arbitrary"),
                                             vmem_limit_bytes=VMEM_LIMIT),
        name="attention_pair",
    )(p_a, jnp.tile(q_g, reps).reshape(1, GROUP), jnp.tile(k_g, reps).reshape(1, GROUP),
      lam_q, lam_k, a_out_g.reshape(GROUP, 1), _block_diag_ones(GROUP, A_QK),
      p_b, b_out_g.reshape(GROUP, 1), tri)


def _gelu_tanh(x):
    return 0.5 * x * (1.0 + jnp.tanh(math.sqrt(2.0 / math.pi) * (x + 0.044715 * (x * x * x))))


def _rglru_kernel(p_ref, cw_ref, cb_ref, gaw_ref, gab_ref, gxw_ref, gxb_ref, lam_ref, og_ref,
                  bd64_ref, o_ref, hist_ref, h_ref):
    tt = o_ref.shape[1]
    ti = pl.program_id(1)

    @pl.when(ti == 0)
    def _():
        hist_ref[...] = jnp.zeros_like(hist_ref)
        h_ref[...] = jnp.zeros_like(h_ref)

    x_raw = p_ref[0, :, 0:GROUP]
    x_gate = p_ref[0, :, GROUP:2 * GROUP]
    hist = hist_ref[...]
    cw = cw_ref[...]
    x = cw[C_CONV - 1:C_CONV] * x_raw + cb_ref[...]
    for d in range(1, C_CONV):
        x = x + cw[C_CONV - 1 - d:C_CONV - d] * _shift_rows(hist, x_raw, d)
    hist_ref[...] = x_raw[tt - HIST:, :]

    r = _sigmoid(_dot(x, gaw_ref[...]) + gab_ref[...])
    i = _sigmoid(_dot(x, gxw_ref[...]) + gxb_ref[...])
    log_a = (-C_EXP) * r * _softplus(-lam_ref[...])
    a = jnp.exp(log_a)
    mult = jnp.sqrt(jnp.tanh(-log_a) * (a * a + 1.0))
    row = lax.broadcasted_iota(jnp.int32, (tt, 1), 0)
    mult = jnp.where((row == 0) & (ti == 0), 1.0, mult)
    u = mult * i * x

    d = 1
    while d < tt:
        a_prev = pltpu.roll(a, d, axis=0)
        u_prev = pltpu.roll(u, d, axis=0)
        keep = row >= d
        u = jnp.where(keep, a * u_prev + u, u)
        a = jnp.where(keep, a * a_prev, a)
        d *= 2
    h = u + a * h_ref[0:1, :]
    h_ref[...] = jnp.broadcast_to(h[tt - 1:tt, :], h_ref.shape)

    y = h * _gelu_tanh(x_gate)
    o_ref[0] = _group_rms(y, bd64_ref[...], HEAD, og_ref[...]).astype(o_ref.dtype)


def _block_diag_weight(w):
    nb, d, _ = w.shape
    eye = jnp.eye(nb, dtype=w.dtype)
    return (eye[:, None, :, None] * w[:, :, None, :]).reshape(nb * d, nb * d)


def _mixer_rglru(p_c, conv_w, conv_b, ga_w, ga_b, gx_w, gx_b, lam, out_g):
    bsz, seq, _ = p_c.shape
    tt = min(LRU_TILE, seq)
    row = lambda v: v.reshape(1, GROUP)
    return pl.pallas_call(
        _rglru_kernel,
        out_shape=jax.ShapeDtypeStruct((bsz, seq, GROUP), BF16),
        grid=(bsz, seq // tt),
        in_specs=[pl.BlockSpec((1, tt, N_C), lambda b, i: (b, i, 0)),
                  _full((C_CONV, GROUP)), _full((1, GROUP)),
                  _full((GROUP, GROUP)), _full((1, GROUP)),
                  _full((GROUP, GROUP)), _full((1, GROUP)),
                  _full((1, GROUP)), _full((1, GROUP)), _full((GROUP, GROUP))],
        out_specs=pl.BlockSpec((1, tt, GROUP), lambda b, i: (b, i, 0)),
        scratch_shapes=[pltpu.VMEM((HIST, GROUP), F32), pltpu.VMEM((HIST, GROUP), F32)],
        compiler_params=pltpu.CompilerParams(dimension_semantics=("arbitrary", "arbitrary"),
                                             vmem_limit_bytes=VMEM_LIMIT),
        name="rg_lru",
    )(p_c, conv_w, row(conv_b), _block_diag_weight(ga_w).astype(BF16), row(ga_b),
      _block_diag_weight(gx_w).astype(BF16), row(gx_b), row(lam), row(out_g),
      _block_diag_ones(GROUP, HEAD))


def _seg_sum(x, bd):
    return _dot_split2(x, bd)


def _unit_lower_inverses(mats, row, col):
    n = mats[0].shape[0]

    def same_block(size):
        shift = size.bit_length() - 1
        return (row >> shift) == (col >> shift)

    base = same_block(INV_BASE)
    eye = (row == col).astype(F32)
    powers = [jnp.where(base, -a, 0.0) for a in mats]
    ts = [eye + p for p in powers]
    k = 1
    while 2 * k < INV_BASE:
        powers = [_dot(p, p) for p in powers]
        ts = [t + _dot(t, p) for t, p in zip(ts, powers)]
        k *= 2
    size = INV_BASE
    while size < n:
        sel = same_block(2 * size) & jnp.logical_not(same_block(size))
        halves = [_dot(t, jnp.where(sel, a, 0.0)) for t, a in zip(ts, mats)]
        ts = [t - _dot(half, t) for t, half in zip(ts, halves)]
        size *= 2
    return ts


def _rwkv_kernel(*refs, has_vres):
    if has_vres:
        (p_ref, vf_ref, mu_ref, w0_ref, wup_ref, a0_ref, aup_ref, gup_ref, kk_ref, ka_ref, rk_ref,
         lw_ref, lb_ref, bd64_ref, v0_ref, vdn_ref, vup_ref, y_ref, state_ref, hist_ref) = refs
    else:
        (p_ref, mu_ref, w0_ref, wup_ref, a0_ref, aup_ref, gup_ref, kk_ref, ka_ref, rk_ref,
         lw_ref, lb_ref, bd64_ref, y_ref, v_ref, state_ref, hist_ref) = refs
    nb, cs = y_ref.shape[0], y_ref.shape[1]
    heads = GROUP // HEAD
    ci = pl.program_id(1)

    @pl.when(ci == 0)
    def _():
        state_ref[...] = jnp.zeros_like(state_ref)
        hist_ref[...] = jnp.zeros_like(hist_ref)

    row = lax.broadcasted_iota(jnp.int32, (cs, cs), 0)
    col = lax.broadcasted_iota(jnp.int32, (cs, cs), 1)
    lower_incl = col <= row
    lower_strict = col < row
    masks = [_lane_mask(GROUP, h * HEAD, (h + 1) * HEAD) for h in range(heads)]
    batch = range(nb)
    rows = lambda x, i: x[i * cs:(i + 1) * cs, :]

    ps = [p_ref[i] for i in batch]
    shifted = [_shift_rows(hist_ref[i], ps[i], 1) for i in batch]
    for i in batch:
        hist_ref[i] = ps[i][cs - HIST:, :]
    p = jnp.concatenate(ps, axis=0)
    xs = p + (jnp.concatenate(shifted, axis=0) - p) * mu_ref[...]

    r = xs[:, 0:GROUP]
    k = xs[:, GROUP:2 * GROUP]
    v = xs[:, 2 * GROUP:3 * GROUP]
    lora = xs[:, 3 * GROUP:]
    w = -_softplus(-(w0_ref[...] + _dot(jnp.tanh(lora), wup_ref[...]))) - 0.5
    a = _sigmoid(a0_ref[...] + _dot(lora, aup_ref[...]))
    g = _dot(_sigmoid(lora), gup_ref[...])
    if has_vres:
        mix = _sigmoid(v0_ref[...] + _dot(_dot(v, vdn_ref[...]), vup_ref[...]))
        v = v + (jnp.concatenate([vf_ref[i] for i in batch], axis=0) - v) * mix
    else:
        for i in batch:
            v_ref[i] = rows(v, i)

    bd64 = bd64_ref[...]
    kk = k * kk_ref[...]
    kk = kk / jnp.maximum(jnp.sqrt(_seg_sum(kk * kk, bd64)), 1e-12)
    kmod = k * (1.0 + (a - 1.0) * ka_ref[...])
    log_decay = -jnp.exp(w)

    cum_wide = _dot_split3_lhs_exact(jnp.where(lower_incl, 1.0, 0.0).astype(BF16),
                                     jnp.concatenate([rows(log_decay, i) for i in batch], axis=1))
    cum = jnp.concatenate([cum_wide[:, i * GROUP:(i + 1) * GROUP] for i in batch], axis=0)
    g_inc = jnp.exp(cum)
    g_inv = jnp.exp(-cum)
    kap = kk * jnp.exp(cum - log_decay)
    rt = r * g_inc
    bet = kk * a * g_inv
    kt = kmod * g_inv

    lhs = [jnp.concatenate([rows(kap, i), rows(rt, i)], axis=0) for i in batch]
    rhs = [jnp.concatenate([rows(bet, i), rows(kt, i)], axis=0).astype(BF16) for i in batch]
    pairs = [(i, h) for i in batch for h in range(heads)]
    grams = [_dot_nt(jnp.where(masks[h], lhs[i], 0.0), rhs[i]) for i, h in pairs]
    a_b = [jnp.where(lower_strict, gm[0:cs, 0:cs], 0.0) for gm in grams]
    a_k = [jnp.where(lower_strict, gm[0:cs, cs:], 0.0).astype(BF16) for gm in grams]
    b_r = [jnp.where(lower_incl, gm[cs:, 0:cs], 0.0).astype(BF16) for gm in grams]
    k_r = [jnp.where(lower_incl, gm[cs:, cs:], 0.0).astype(BF16) for gm in grams]
    t_inv = [t.astype(BF16) for t in _unit_lower_inverses(a_b, row, col)]

    def stack_heads(x):
        return jnp.concatenate([jnp.where(m, x, 0.0) for m in masks], axis=0).astype(BF16)

    per_row = lambda mats, i: jnp.concatenate(mats[i * heads:(i + 1) * heads], axis=1)
    dotf = lambda x, y: jnp.dot(x, y, preferred_element_type=F32)
    states = [state_ref[i] for i in batch]
    states_bf = [s.astype(BF16) for s in states]
    v_stack = [stack_heads(rows(v, i)) for i in batch]
    rhs_u = [_dot(rows(kap, i), states_bf[i]) + dotf(per_row(a_k, i), v_stack[i]) for i in batch]
    u = [dotf(per_row(t_inv, i), stack_heads(rhs_u[i])) for i in batch]
    y = [_dot(rows(rt, i), states_bf[i]) + dotf(per_row(k_r, i), v_stack[i])
         - dotf(per_row(b_r, i), stack_heads(u[i])) for i in batch]

    upd = [_dot(jnp.transpose(rows(kt, i)), rows(v, i)) - _dot(jnp.transpose(rows(bet, i)), u[i])
           for i in batch]
    bd_mask = bd64 > 0
    for i in batch:
        g_end = jnp.transpose(rows(g_inc, i))[:, cs - 1:cs]
        state_ref[i] = g_end * (states[i] + jnp.where(bd_mask, upd[i], 0.0))

    y = jnp.concatenate(y, axis=0)
    mean = _seg_sum(y, bd64) * (1.0 / HEAD)
    yc = y - mean
    var = _seg_sum(yc * yc, bd64) * (1.0 / HEAD)
    yn = yc * lax.rsqrt(var + LNX_EPS) * lw_ref[...] + lb_ref[...]
    bonus = _seg_sum(r * kmod * rk_ref[...], bd64) * v
    out = ((yn + bonus) * g).astype(y_ref.dtype)
    for i in batch:
        y_ref[i] = rows(out, i)


def _pad_rows(w, start, total):
    out = jnp.zeros((total, w.shape[1]), w.dtype)
    return lax.dynamic_update_slice(out, w, (start, 0))


def _mixer_rwkv7(p_d, mu, w0, w_up, a0, a_up, g_up, k_k, k_a, r_k, lnx_w, lnx_b, v_first, v_res):
    bsz, seq, _ = p_d.shape
    cs = min(RWKV_CHUNK, seq)
    lora = LORA_W + LORA_A + LORA_G
    row = lambda v: v.reshape(1, -1)
    has_vres = v_res is not None
    nb = math.gcd(RWKV_BATCH, bsz)
    chunk = lambda n: pl.BlockSpec((nb, cs, n), lambda b, i: (b, i, 0))
    args = [p_d]
    specs = [chunk(N_D)]
    if has_vres:
        args.append(v_first)
        specs.append(chunk(GROUP))
    args += [row(mu), row(w0), _pad_rows(w_up, 0, lora).astype(BF16),
             row(a0), _pad_rows(a_up, LORA_W, lora).astype(BF16),
             _pad_rows(g_up, LORA_W + LORA_A, lora).astype(BF16),
             row(k_k), row(k_a), row(r_k), row(lnx_w), row(lnx_b), _block_diag_ones(GROUP, HEAD)]
    specs += [_full((1, N_D)), _full((1, GROUP)), _full((lora, GROUP)),
              _full((1, GROUP)), _full((lora, GROUP)), _full((lora, GROUP)),
              _full((1, GROUP)), _full((1, GROUP)), _full((1, GROUP)), _full((1, GROUP)),
              _full((1, GROUP)), _full((GROUP, GROUP))]
    if has_vres:
        v0, v_down, v_up = v_res
        rank = v_down.shape[1]
        vdn = jnp.zeros((GROUP, lora), F32).at[:, :rank].set(v_down).astype(BF16)
        args += [row(v0), vdn, _pad_rows(v_up, 0, lora).astype(BF16)]
        specs += [_full((1, GROUP)), _full((GROUP, lora)), _full((lora, GROUP))]
        out_shape = jax.ShapeDtypeStruct((bsz, seq, GROUP), BF16)
        out_specs = chunk(GROUP)
    else:
        out_shape = [jax.ShapeDtypeStruct((bsz, seq, GROUP), BF16),
                     jax.ShapeDtypeStruct((bsz, seq, GROUP), F32)]
        out_specs = [chunk(GROUP), chunk(GROUP)]
    out = pl.pallas_call(
        functools.partial(_rwkv_kernel, has_vres=has_vres),
        out_shape=out_shape,
        grid=(bsz // nb, seq // cs),
        in_specs=specs,
        out_specs=out_specs,
        scratch_shapes=[pltpu.VMEM((nb, GROUP, GROUP), F32), pltpu.VMEM((nb, HIST, N_D), F32)],
        compiler_params=pltpu.CompilerParams(dimension_semantics=("arbitrary", "arbitrary"),
                                             vmem_limit_bytes=VMEM_LIMIT),
        name="rwkv7_chunked",
    )(*args)
    if has_vres:
        return out, None
    return out[0], out[1]


def _out_ffn_kernel(x_ref, ya_ref, yb_ref, yc_ref, yd_ref, mod_ref, g_ref, wo_ref, wup_ref,
                    cw_ref, cb_ref, wdn_ref, o_ref, hist_ref):
    tm, d = x_ref.shape[1], x_ref.shape[2]
    d_ff = wdn_ref.shape[0]
    ti = pl.program_id(1)

    @pl.when(ti == 0)
    def _():
        hist_ref[...] = jnp.zeros_like(hist_ref)

    mod = mod_ref[0]
    gate1 = mod[:, 2 * d:3 * d]
    shift2, scale2, gate2 = mod[:, 3 * d:4 * d], mod[:, 4 * d:5 * d], mod[:, 5 * d:6 * d]
    mix = jnp.dot(ya_ref[0], wo_ref[0:GROUP, :], preferred_element_type=F32)
    for j, y_ref in enumerate((yb_ref, yc_ref, yd_ref), start=1):
        mix = mix + jnp.dot(y_ref[0], wo_ref[j * GROUP:(j + 1) * GROUP, :],
                            preferred_element_type=F32)
    x1 = x_ref[0] + gate1 * mix
    h = _modulated_norm(x1, g_ref[...], shift2, scale2).astype(BF16)

    def conv(u, lo):
        cw = cw_ref[:, lo:lo + FF_CHUNK]
        hist = hist_ref[:, lo:lo + FF_CHUNK]
        out = cw[FF_CONV - 1:FF_CONV] * u + cb_ref[:, lo:lo + FF_CHUNK]
        for dly in range(1, FF_CONV):
            out = out + cw[FF_CONV - 1 - dly:FF_CONV - dly] * _shift_rows(hist, u, dly)
        hist_ref[:, lo:lo + FF_CHUNK] = u[tm - HIST:, :]
        return out

    def up_proj(j):
        lo_g, lo_v = j * FF_CHUNK, d_ff + j * FF_CHUNK
        return (jnp.dot(h, wup_ref[:, lo_g:lo_g + FF_CHUNK], preferred_element_type=F32),
                jnp.dot(h, wup_ref[:, lo_v:lo_v + FF_CHUNK], preferred_element_type=F32))

    n_chunks = d_ff // FF_CHUNK
    acc = jnp.zeros((tm, d), F32)
    ahead = up_proj(0)
    act_prev = None
    for j in range(n_chunks):
        raw_g, raw_v = ahead
        if j + 1 < n_chunks:
            ahead = up_proj(j + 1)
        if act_prev is not None:
            lo_p = (j - 1) * FF_CHUNK
            acc = acc + jnp.dot(act_prev, wdn_ref[lo_p:lo_p + FF_CHUNK, :],
                                preferred_element_type=F32)
        lo_g, lo_v = j * FF_CHUNK, d_ff + j * FF_CHUNK
        u_g = conv(raw_g, lo_g)
        u_v = conv(raw_v, lo_v)
        act_prev = (u_g * _sigmoid(u_g) * u_v).astype(BF16)
    lo_p = (n_chunks - 1) * FF_CHUNK
    acc = acc + jnp.dot(act_prev, wdn_ref[lo_p:lo_p + FF_CHUNK, :], preferred_element_type=F32)
    o_ref[0] = x1 + gate2 * acc


def _out_ffn(x, ys, mod, g2, w_out, w_up, conv_w, conv_b, w_down):
    bsz, seq, d = x.shape
    d_ff = w_down.shape[0]
    tm = min(ROW_TILE, seq)
    tile = lambda n: pl.BlockSpec((1, tm, n), lambda b, i: (b, i, 0))
    return pl.pallas_call(
        _out_ffn_kernel,
        out_shape=jax.ShapeDtypeStruct((bsz, seq, d), F32),
        grid=(bsz, seq // tm),
        in_specs=[tile(d), tile(GROUP), tile(GROUP), tile(GROUP), tile(GROUP),
                  pl.BlockSpec((1, 1, mod.shape[-1]), lambda b, i: (b, 0, 0)),
                  _full((1, d)), _resident((4 * GROUP, d)), _resident((d, 2 * d_ff)),
                  _full((FF_CONV, 2 * d_ff)), _full((1, 2 * d_ff)), _resident((d_ff, d))],
        out_specs=tile(d),
        scratch_shapes=[pltpu.VMEM((HIST, 2 * d_ff), F32)],
        compiler_params=pltpu.CompilerParams(dimension_semantics=("arbitrary", "arbitrary"),
                                             vmem_limit_bytes=VMEM_LIMIT),
        name="out_proj_ffn",
    )(x, *ys, mod, g2.reshape(1, d), w_out, w_up, conv_w, conv_b.reshape(1, -1), w_down)


def kernel(x, c, w_ada, b_ada, norm1_g, norm2_g, w_in, w_out, a_qnorm_g, a_knorm_g, a_lam_q, a_lam_k, a_out_g, b_out_g, c_conv_w, c_conv_b, c_gate_a_w, c_gate_a_b, c_gate_x_w, c_gate_x_b, c_lambda, c_out_g, d_mu, d_w0, d_w_up, d_a0, d_a_up, d_g_up, d_k_k, d_k_a, d_r_k, d_lnx_w, d_lnx_b, d_v0, d_v_down, d_v_up, ff_w_up, ff_conv_w, ff_conv_b, ff_w_down):
    depth = w_in.shape[0]
    bsz = x.shape[0]
    mods = _ada_modulation(c, w_ada, b_ada)
    v_first = None
    for l in range(depth):
        mod = mods[l].reshape(bsz, 1, -1)
        p_a, p_b, p_c, p_d = _in_proj(x, mod, norm1_g[l], w_in[l].astype(BF16))
        lam_init = 0.8 - 0.6 * math.exp(-0.3 * l)
        y_a, y_b = _mixer_attention_pair(p_a, a_qnorm_g[l], a_knorm_g[l], a_lam_q[l], a_lam_k[l],
                                         a_out_g[l], lam_init, p_b, b_out_g[l])
        y_c = _mixer_rglru(p_c, c_conv_w[l], c_conv_b[l], c_gate_a_w[l], c_gate_a_b[l],
                           c_gate_x_w[l], c_gate_x_b[l], c_lambda[l], c_out_g[l])
        v_res = None if l == 0 else (d_v0[l - 1], d_v_down[l - 1], d_v_up[l - 1])
        y_d, v_d = _mixer_rwkv7(p_d, d_mu[l], d_w0[l], d_w_up[l], d_a0[l], d_a_up[l], d_g_up[l],
                                d_k_k[l], d_k_a[l], d_r_k[l].reshape(-1), d_lnx_w[l], d_lnx_b[l],
                                v_first, v_res)
        if l == 0:
            v_first = v_d
        x = _out_ffn(x, (y_a, y_b, y_c, y_d), mod, norm2_g[l], w_out[l].astype(BF16),
                     ff_w_up[l].astype(BF16), ff_conv_w[l], ff_conv_b[l],
                     ff_w_down[l].astype(BF16))
    return x
```

```python
import functools
import math

import numpy as np
import jax
import jax.numpy as jnp
from jax import lax
from jax.experimental import pallas as pl
from jax.experimental.pallas import tpu as pltpu

F32 = jnp.float32
BF16 = jnp.bfloat16

GROUP = 256
A_HEADS = 4
A_QK = 32
HEAD = 64
N_A = 3 * GROUP
N_B = 3 * GROUP
N_C = 2 * GROUP
LORA_W = 32
LORA_A = 32
LORA_G = 64
N_D = 3 * GROUP + LORA_W + LORA_A + LORA_G
C_CONV = 4
C_EXP = 8.0
FF_CONV = 3
LOG2_E = 1.4426950408889634
RMS_EPS = 1e-6
LNX_EPS = 64e-5
ADA_CHUNKS = 6

ROW_TILE = 512
FFN_ROW_TILE = 512
ATT_TILE = 256
LRU_TILE = 256
RWKV_CHUNK = 128
RWKV_BATCH = 4
FF_CHUNK = 256
SUM_ROWS = 16
INV_BASE = 8
HIST = 8
VMEM_LIMIT = 56 * 1024 * 1024


def _dot(a, b):
    return jnp.dot(a.astype(BF16), b.astype(BF16), preferred_element_type=F32)


def _dot_nt(a, b):
    return lax.dot_general(a.astype(BF16), b.astype(BF16), (((1,), (1,)), ((), ())),
                           preferred_element_type=F32)


def _split2(x):
    hi = x.astype(BF16)
    lo = (x - hi.astype(F32)).astype(BF16)
    return hi, lo


def _dot_split2(x, m):
    hi, lo = _split2(x)
    return (jnp.dot(hi, m, preferred_element_type=F32)
            + jnp.dot(lo, m, preferred_element_type=F32))


def _dot_split3_lhs_exact(m, x):
    hi = x.astype(BF16)
    r1 = x - hi.astype(F32)
    mid = r1.astype(BF16)
    lo = (r1 - mid.astype(F32)).astype(BF16)
    return (jnp.dot(m, hi, preferred_element_type=F32)
            + jnp.dot(m, mid, preferred_element_type=F32)
            + jnp.dot(m, lo, preferred_element_type=F32))


def _sigmoid(x):
    return 1.0 / (1.0 + jnp.exp(-x))


def _softplus(x):
    return jnp.maximum(x, 0.0) + jnp.log1p(jnp.exp(-jnp.abs(x)))


def _lane_mask(width, lo, hi):
    lane = lax.broadcasted_iota(jnp.int32, (1, width), 1)
    return (lane >= lo) & (lane < hi)


def _shift_rows(hist, x, d):
    ext = jnp.concatenate([hist, x], axis=0)
    return pltpu.roll(ext, d, axis=0)[HIST:, :]


def _full(shape):
    nd = len(shape)
    return pl.BlockSpec(shape, lambda *_: (0,) * nd)


def _resident(shape):
    nd = len(shape)
    return pl.BlockSpec(shape, lambda *_: (0,) * nd, pipeline_mode=pl.Buffered(1))


def _resident_layer(shape, layer):
    return pl.BlockSpec((None,) + tuple(shape), lambda *_: (layer, 0, 0),
                        pipeline_mode=pl.Buffered(1))


def _stacked(w):
    return w[None] if w.ndim == 2 else w


def _block_diag_ones(width, block):
    idx = np.arange(width) // block
    return jnp.asarray((idx[:, None] == idx[None, :]).astype(np.float32), dtype=BF16)


def _ada_kernel(c_ref, w_ref, b_ref, o_ref):
    c = c_ref[...]
    cond = c * _sigmoid(c)
    o_ref[0] = _dot(cond, w_ref[0]) + b_ref[0]


def _ada_modulation(c, w_ada, b_ada):
    depth, d, n = w_ada.shape
    bsz = c.shape[0]
    tn = 1536
    return pl.pallas_call(
        _ada_kernel,
        out_shape=jax.ShapeDtypeStruct((depth, bsz, n), F32),
        grid=(depth, n // tn),
        in_specs=[pl.BlockSpec((bsz, d), lambda l, j: (0, 0)),
                  pl.BlockSpec((1, d, tn), lambda l, j: (l, 0, j)),
                  pl.BlockSpec((1, 1, tn), lambda l, j: (l, 0, j))],
        out_specs=pl.BlockSpec((1, bsz, tn), lambda l, j: (l, 0, j)),
        compiler_params=pltpu.CompilerParams(dimension_semantics=("arbitrary", "arbitrary"),
                                             vmem_limit_bytes=VMEM_LIMIT),
        name="ada_modulation",
    )(c, w_ada, b_ada.reshape(depth, 1, n))


def _modulated_norm(x, g, shift, scale):
    ms = jnp.mean(x * x, axis=-1, keepdims=True)
    return (x * lax.rsqrt(ms + RMS_EPS) * g) * (1.0 + scale) + shift


def _in_proj_kernel(x_ref, mod_ref, g_ref, w_ref, pa_ref, pb_ref, pc_ref, pd_ref):
    d = x_ref.shape[-1]
    mod = mod_ref[0]
    h = _modulated_norm(x_ref[0], g_ref[...], mod[:, 0:d], mod[:, d:2 * d]).astype(BF16)
    pa_ref[0] = jnp.dot(h, w_ref[:, 0:N_A], preferred_element_type=F32)
    pb_ref[0] = jnp.dot(h, w_ref[:, N_A:N_A + N_B], preferred_element_type=F32)
    pc_ref[0] = jnp.dot(h, w_ref[:, N_A + N_B:N_A + N_B + N_C], preferred_element_type=F32)
    pd_ref[0] = jnp.dot(h, w_ref[:, N_A + N_B + N_C:], preferred_element_type=F32)


def _in_proj(x, mod, g, w_bf16, layer=0):
    bsz, seq, d = x.shape
    w_bf16 = _stacked(w_bf16)
    n_in = w_bf16.shape[-1]
    tm = min(ROW_TILE, seq)
    widths = (N_A, N_B, N_C, N_D)
    return pl.pallas_call(
        _in_proj_kernel,
        out_shape=[jax.ShapeDtypeStruct((bsz, seq, n), F32) for n in widths],
        grid=(bsz, seq // tm),
        in_specs=[pl.BlockSpec((1, tm, d), lambda b, i: (b, i, 0)),
                  pl.BlockSpec((1, 1, mod.shape[-1]), lambda b, i: (b, 0, 0)),
                  _full((1, d)),
                  _resident_layer((d, n_in), layer)],
        out_specs=[pl.BlockSpec((1, tm, n), lambda b, i: (b, i, 0)) for n in widths],
        compiler_params=pltpu.CompilerParams(dimension_semantics=("arbitrary", "arbitrary"),
                                             vmem_limit_bytes=VMEM_LIMIT),
        name="in_proj",
    )(x, mod, g.reshape(1, d), w_bf16)


def _group_rms(x, bd, group, gain):
    ms = _dot_split2(x * x, bd) * (1.0 / group)
    return x * lax.rsqrt(ms + RMS_EPS) * gain


def _attn_a_setup(p_ref, qg_ref, kg_ref, lq_ref, lk_ref, og_ref, bd32_ref,
                  o_ref, kn_ref, vt_ref, lam_init):
    tq = o_ref.shape[1]
    n_kt = vt_ref.shape[0]
    qi = pl.program_id(1)
    bd32 = bd32_ref[...]

    @pl.when(qi == 0)
    def _():
        k = p_ref[0, :, GROUP:2 * GROUP]
        kn_ref[...] = _group_rms(k, bd32, A_QK, kg_ref[...]).astype(BF16)
        ones_rows = jnp.ones((SUM_ROWS, tq), BF16)
        for j in range(n_kt):
            vt = jnp.transpose(p_ref[0, j * tq:(j + 1) * tq, 2 * GROUP:3 * GROUP])
            for h in range(A_HEADS):
                vt_ref[j, h, 0:HEAD, :] = vt[h * HEAD:(h + 1) * HEAD, :].astype(BF16)
                vt_ref[j, h, HEAD:, :] = ones_rows

    lq = lq_ref[...]
    lk = lk_ref[...]
    lam = (jnp.exp(jnp.sum(lq[0:1] * lk[0:1], axis=-1, keepdims=True))
           - jnp.exp(jnp.sum(lq[1:2] * lk[1:2], axis=-1, keepdims=True)) + lam_init)

    q0 = pl.multiple_of(qi * tq, tq)
    q = p_ref[0, pl.ds(q0, tq), 0:GROUP]
    qn = _group_rms(q, bd32, A_QK, qg_ref[...]) * (A_QK ** -0.5 * LOG2_E)

    key_idx = lax.broadcasted_iota(jnp.int32, (tq, tq), 0)
    query_idx = lax.broadcasted_iota(jnp.int32, (tq, tq), 1)
    causal = key_idx <= query_idx

    qt = jnp.transpose(qn)
    feat = lax.broadcasted_iota(jnp.int32, (GROUP, 1), 0)
    qms = [[jnp.where((feat >= h * HEAD + c * A_QK) & (feat < h * HEAD + (c + 1) * A_QK), qt, 0.0
                      ).astype(BF16) for h in range(A_HEADS)] for c in range(2)]

    def key_tile(kj, carry, diag):
        kb = kn_ref[pl.ds(pl.multiple_of(kj * tq, tq), tq), :]
        dotf = lambda x, y: jnp.dot(x, y, preferred_element_type=F32)
        chains = [(c, h) for c in range(2) for h in range(A_HEADS)]
        ms = [carry[c][0][h] for c, h in chains]
        ls = [carry[c][1][h] for c, h in chains]
        accs = [carry[c][2][h] for c, h in chains]
        ss = [dotf(kb, qms[c][h]) for c, h in chains]
        yield
        if diag:
            ss = [jnp.where(causal, s, -jnp.inf) for s in ss]
        ms_new = [jnp.maximum(m, jnp.max(s, axis=0, keepdims=True)) for m, s in zip(ms, ss)]
        alphas = [jnp.exp2(m - m_new) for m, m_new in zip(ms, ms_new)]
        ps = [jnp.exp2(s - m_new).astype(BF16) for s, m_new in zip(ss, ms_new)]
        yield
        res = [dotf(vt_ref[kj, h], p) for (c, h), p in zip(chains, ps)]
        yield
        accs_new = [alpha * acc + r[0:HEAD, :] for alpha, acc, r in zip(alphas, accs, res)]
        ls_new = [alpha * l + r[HEAD:HEAD + 1, :] for alpha, l, r in zip(alphas, ls, res)]
        n = A_HEADS
        return tuple((tuple(ms_new[c * n:(c + 1) * n]), tuple(ls_new[c * n:(c + 1) * n]),
                      tuple(accs_new[c * n:(c + 1) * n])) for c in range(2))

    init_c = (tuple(jnp.full((1, tq), -jnp.inf, F32) for _ in range(A_HEADS)),
              tuple(jnp.zeros((1, tq), F32) for _ in range(A_HEADS)),
              tuple(jnp.zeros((HEAD, tq), F32) for _ in range(A_HEADS)))
    def finish(carry):
        (_, l0, acc0), (_, l1, acc1) = carry
        og = og_ref[...]
        ys = []
        for h in range(A_HEADS):
            o = acc0[h] * (1.0 / l0[h]) - lam * (acc1[h] * (1.0 / l1[h]))
            ms = jnp.mean(o * o, axis=0, keepdims=True)
            ys.append(o * lax.rsqrt(ms + RMS_EPS) * og[h * HEAD:(h + 1) * HEAD, :])
        y = jnp.transpose(jnp.concatenate(ys, axis=0)) * (1.0 - lam_init)
        o_ref[0] = y.astype(o_ref.dtype)

    return (init_c, init_c), key_tile, finish


def _run_steps(*gens):
    results = [None] * len(gens)
    live = list(range(len(gens)))
    while live:
        for idx in list(live):
            try:
                next(gens[idx])
            except StopIteration as done:
                results[idx] = done.value
                live.remove(idx)
    return results


def _attn_a_kernel(*refs, lam_init):
    qi = pl.program_id(1)
    init, key_tile, finish = _attn_a_setup(*refs, lam_init)
    carry = lax.fori_loop(0, qi, lambda kj, carry: _run_steps(key_tile(kj, carry, False))[0], init)
    finish(_run_steps(key_tile(qi, carry, True))[0])


def _mixer_diff_attn(p_a, q_g, k_g, lam_q, lam_k, out_g, lam_init):
    bsz, seq, _ = p_a.shape
    tq = min(ATT_TILE, seq)
    reps = GROUP // A_QK
    return pl.pallas_call(
        functools.partial(_attn_a_kernel, lam_init=lam_init),
        out_shape=jax.ShapeDtypeStruct((bsz, seq, GROUP), BF16),
        grid=(bsz, seq // tq),
        in_specs=[pl.BlockSpec((1, seq, N_A), lambda b, i: (b, 0, 0)),
                  _full((1, GROUP)), _full((1, GROUP)),
                  _full((2, A_QK)), _full((2, A_QK)),
                  _full((GROUP, 1)),
                  _full((GROUP, GROUP))],
        out_specs=pl.BlockSpec((1, tq, GROUP), lambda b, i: (b, i, 0)),
        scratch_shapes=[pltpu.VMEM((seq, GROUP), BF16),
                        pltpu.VMEM((seq // tq, A_HEADS, HEAD + SUM_ROWS, tq), BF16)],
        compiler_params=pltpu.CompilerParams(dimension_semantics=("arbitrary", "arbitrary"),
                                             vmem_limit_bytes=VMEM_LIMIT),
        name="diff_attention",
    )(p_a, jnp.tile(q_g, reps).reshape(1, GROUP), jnp.tile(k_g, reps).reshape(1, GROUP),
      lam_q, lam_k, out_g.reshape(GROUP, 1), _block_diag_ones(GROUP, A_QK))


def _attn_b_setup(p_ref, og_ref, tri_ref, o_ref, kb_ref, vt_ref):
    tq = o_ref.shape[1]
    n_kt = vt_ref.shape[0]
    heads = GROUP // HEAD
    qi = pl.program_id(1)
    feat = lax.broadcasted_iota(jnp.int32, (GROUP, 1), 0)
    head_rows = [(feat >= h * HEAD) & (feat < (h + 1) * HEAD) for h in range(heads)]

    @pl.when(qi == 0)
    def _():
        kb_ref[...] = p_ref[0, :, GROUP:2 * GROUP].astype(BF16)
        for j in range(n_kt):
            vt = jnp.transpose(p_ref[0, j * tq:(j + 1) * tq, 2 * GROUP:3 * GROUP])
            vt_ref[j] = vt.astype(BF16)

    q0 = pl.multiple_of(qi * tq, tq)
    qt = jnp.transpose(p_ref[0, pl.ds(q0, tq), 0:GROUP] * (HEAD ** -0.5 * LOG2_E))
    qms = [jnp.where(hr, qt, 0.0).astype(BF16) for hr in head_rows]
    tri = tri_ref[...]
    key_idx = lax.broadcasted_iota(jnp.int32, (tq, tq), 0)
    query_idx = lax.broadcasted_iota(jnp.int32, (tq, tq), 1)
    strict = key_idx < query_idx
    dotf = lambda x, y: jnp.dot(x, y, preferred_element_type=F32)

    def key_tile(kj, carry, diag):
        laters, accs = carry
        kb = kb_ref[pl.ds(pl.multiple_of(kj * tq, tq), tq), :]
        zs = [dotf(kb, qm) for qm in qms]
        yield
        neg_keeps = [jnp.maximum(z, 0.0) + jnp.log2(1.0 + jnp.exp2(-jnp.abs(z))) for z in zs]
        if diag:
            neg_keeps = [jnp.where(strict, nk, 0.0) for nk in neg_keeps]
        splits = [jnp.concatenate(_split2(nk), axis=0) for nk in neg_keeps]
        yield
        incls = [dotf(tri, sp) for sp in splits]
        yield
        ws = [jnp.exp2(z + incl + later) for z, incl, later in zip(zs, incls, laters)]
        if diag:
            ws = [jnp.where(strict, w, 0.0) for w in ws]
        yield
        accs = tuple(acc + dotf(vt_ref[kj, h * HEAD:(h + 1) * HEAD, :], w.astype(BF16))
                     for h, (acc, w) in enumerate(zip(accs, ws)))
        return tuple(later + incl[0:1, :] for later, incl in zip(laters, incls)), accs

    init = (tuple(jnp.zeros((1, tq), F32) for _ in range(heads)),
            tuple(jnp.zeros((HEAD, tq), F32) for _ in range(heads)))

    def finish(carry):
        _, accs = carry
        og = og_ref[...]
        ys = []
        for h in range(heads):
            o = accs[h]
            ms = jnp.mean(o * o, axis=0, keepdims=True)
            ys.append(o * lax.rsqrt(ms + RMS_EPS) * og[h * HEAD:(h + 1) * HEAD, :])
        o_ref[0] = jnp.transpose(jnp.concatenate(ys, axis=0)).astype(o_ref.dtype)

    return init, key_tile, finish


def _attn_b_kernel(*refs):
    qi = pl.program_id(1)
    init, key_tile, finish = _attn_b_setup(*refs)
    carry = _run_steps(key_tile(qi, init, True))[0]
    finish(lax.fori_loop(
        0, qi, lambda i, carry: _run_steps(key_tile(qi - 1 - i, carry, False))[0], carry))


def _attn_ab_kernel(pa_ref, qg_ref, kg_ref, lq_ref, lk_ref, oga_ref, bd32_ref, pb_ref, ogb_ref, tri_ref,
                    oa_ref, ob_ref, kn_ref, vta_ref, kb_ref, vtb_ref, *, lam_init):
    qi = pl.program_id(1)
    a_init, a_tile, a_finish = _attn_a_setup(pa_ref, qg_ref, kg_ref, lq_ref, lk_ref, oga_ref, bd32_ref,
                                             oa_ref, kn_ref, vta_ref, lam_init)
    b_init, b_tile, b_finish = _attn_b_setup(pb_ref, ogb_ref, tri_ref, ob_ref, kb_ref, vtb_ref)
    both = lambda kj, c, diag: tuple(_run_steps(a_tile(kj, c[0], diag), b_tile(kj, c[1], diag)))
    carry = both(qi, (a_init, b_init), True)
    carry = lax.fori_loop(0, qi, lambda i, c: both(qi - 1 - i, c, False), carry)
    a_finish(carry[0])
    b_finish(carry[1])


def _mixer_stick_breaking(p_b, out_g):
    bsz, seq, _ = p_b.shape
    tq = min(ATT_TILE, seq)
    neg_upper = -np.triu(np.ones((tq, tq), np.float32))
    tri = jnp.asarray(np.concatenate([neg_upper, neg_upper], axis=1), dtype=BF16)
    return pl.pallas_call(
        _attn_b_kernel,
        out_shape=jax.ShapeDtypeStruct((bsz, seq, GROUP), BF16),
        grid=(bsz, seq // tq),
        in_specs=[pl.BlockSpec((1, seq, N_B), lambda b, i: (b, 0, 0)),
                  _full((GROUP, 1)), _full((tq, 2 * tq))],
        out_specs=pl.BlockSpec((1, tq, GROUP), lambda b, i: (b, i, 0)),
        scratch_shapes=[pltpu.VMEM((seq, GROUP), BF16),
                        pltpu.VMEM((seq // tq, GROUP, tq), BF16)],
        compiler_params=pltpu.CompilerParams(dimension_semantics=("arbitrary", "arbitrary"),
                                             vmem_limit_bytes=VMEM_LIMIT),
        name="stick_breaking_attention",
    )(p_b, out_g.reshape(GROUP, 1), tri)


def _mixer_attention_pair(p_a, q_g, k_g, lam_q, lam_k, a_out_g, lam_init, p_b, b_out_g):
    bsz, seq, _ = p_a.shape
    tq = min(ATT_TILE, seq)
    reps = GROUP // A_QK
    neg_upper = -np.triu(np.ones((tq, tq), np.float32))
    tri = jnp.asarray(np.concatenate([neg_upper, neg_upper], axis=1), dtype=BF16)
    whole = lambda n: pl.BlockSpec((1, seq, n), lambda b, i: (b, 0, 0))
    tile = pl.BlockSpec((1, tq, GROUP), lambda b, i: (b, i, 0))
    return pl.pallas_call(
        functools.partial(_attn_ab_kernel, lam_init=lam_init),
        out_shape=[jax.ShapeDtypeStruct((bsz, seq, GROUP), BF16)] * 2,
        grid=(bsz, seq // tq),
        in_specs=[whole(N_A), _full((1, GROUP)), _full((1, GROUP)),
                  _full((2, A_QK)), _full((2, A_QK)), _full((GROUP, 1)), _full((GROUP, GROUP)),
                  whole(N_B), _full((GROUP, 1)), _full((tq, 2 * tq))],
        out_specs=[tile, tile],
        scratch_shapes=[pltpu.VMEM((seq, GROUP), BF16),
                        pltpu.VMEM((seq // tq, A_HEADS, HEAD + SUM_ROWS, tq), BF16),
                        pltpu.VMEM((seq, GROUP), BF16),
                        pltpu.VMEM((seq // tq, GROUP, tq), BF16)],
        compiler_params=pltpu.CompilerParams(dimension_semantics=("arbitrary", "arbitrary"),
                                             vmem_limit_bytes=VMEM_LIMIT),
        name="attention_pair",
    )(p_a, jnp.tile(q_g, reps).reshape(1, GROUP), jnp.tile(k_g, reps).reshape(1, GROUP),
      lam_q, lam_k, a_out_g.reshape(GROUP, 1), _block_diag_ones(GROUP, A_QK),
      p_b, b_out_g.reshape(GROUP, 1), tri)


def _gelu_tanh(x):
    return 0.5 * x * (1.0 + jnp.tanh(math.sqrt(2.0 / math.pi) * (x + 0.044715 * (x * x * x))))


def _rglru_kernel(p_ref, cw_ref, cb_ref, gaw_ref, gab_ref, gxw_ref, gxb_ref, lam_ref, og_ref,
                  bd64_ref, o_ref, hist_ref, h_ref):
    tt = o_ref.shape[1]
    ti = pl.program_id(1)

    @pl.when(ti == 0)
    def _():
        hist_ref[...] = jnp.zeros_like(hist_ref)
        h_ref[...] = jnp.zeros_like(h_ref)

    x_raw = p_ref[0, :, 0:GROUP]
    x_gate = p_ref[0, :, GROUP:2 * GROUP]
    hist = hist_ref[...]
    cw = cw_ref[...]
    x = cw[C_CONV - 1:C_CONV] * x_raw + cb_ref[...]
    for d in range(1, C_CONV):
        x = x + cw[C_CONV - 1 - d:C_CONV - d] * _shift_rows(hist, x_raw, d)
    hist_ref[...] = x_raw[tt - HIST:, :]

    r = _sigmoid(_dot(x, gaw_ref[...]) + gab_ref[...])
    i = _sigmoid(_dot(x, gxw_ref[...]) + gxb_ref[...])
    log_a = (-C_EXP) * r * _softplus(-lam_ref[...])
    a = jnp.exp(log_a)
    mult = jnp.sqrt(jnp.tanh(-log_a) * (a * a + 1.0))
    row = lax.broadcasted_iota(jnp.int32, (tt, 1), 0)
    mult = jnp.where((row == 0) & (ti == 0), 1.0, mult)
    u = mult * i * x

    d = 1
    while d < tt:
        a_prev = pltpu.roll(a, d, axis=0)
        u_prev = pltpu.roll(u, d, axis=0)
        keep = row >= d
        u = jnp.where(keep, a * u_prev + u, u)
        a = jnp.where(keep, a * a_prev, a)
        d *= 2
    h = u + a * h_ref[0:1, :]
    h_ref[...] = jnp.broadcast_to(h[tt - 1:tt, :], h_ref.shape)

    y = h * _gelu_tanh(x_gate)
    o_ref[0] = _group_rms(y, bd64_ref[...], HEAD, og_ref[...]).astype(o_ref.dtype)


def _block_diag_weight(w):
    nb, d, _ = w.shape
    eye = jnp.eye(nb, dtype=w.dtype)
    return (eye[:, None, :, None] * w[:, :, None, :]).reshape(nb * d, nb * d)


def _mixer_rglru(p_c, conv_w, conv_b, ga_w, ga_b, gx_w, gx_b, lam, out_g):
    bsz, seq, _ = p_c.shape
    tt = min(LRU_TILE, seq)
    row = lambda v: v.reshape(1, GROUP)
    return pl.pallas_call(
        _rglru_kernel,
        out_shape=jax.ShapeDtypeStruct((bsz, seq, GROUP), BF16),
        grid=(bsz, seq // tt),
        in_specs=[pl.BlockSpec((1, tt, N_C), lambda b, i: (b, i, 0)),
                  _full((C_CONV, GROUP)), _full((1, GROUP)),
                  _full((GROUP, GROUP)), _full((1, GROUP)),
                  _full((GROUP, GROUP)), _full((1, GROUP)),
                  _full((1, GROUP)), _full((1, GROUP)), _full((GROUP, GROUP))],
        out_specs=pl.BlockSpec((1, tt, GROUP), lambda b, i: (b, i, 0)),
        scratch_shapes=[pltpu.VMEM((HIST, GROUP), F32), pltpu.VMEM((HIST, GROUP), F32)],
        compiler_params=pltpu.CompilerParams(dimension_semantics=("arbitrary", "arbitrary"),
                                             vmem_limit_bytes=VMEM_LIMIT),
        name="rg_lru",
    )(p_c, conv_w, row(conv_b), _block_diag_weight(ga_w).astype(BF16), row(ga_b),
      _block_diag_weight(gx_w).astype(BF16), row(gx_b), row(lam), row(out_g),
      _block_diag_ones(GROUP, HEAD))


def _seg_sum(x, bd):
    return _dot_split2(x, bd)


def _unit_lower_inverses(mats, row, col):
    n = mats[0].shape[0]

    def same_block(size):
        shift = size.bit_length() - 1
        return (row >> shift) == (col >> shift)

    base = same_block(INV_BASE)
    eye = (row == col).astype(F32)
    powers = [jnp.where(base, -a, 0.0) for a in mats]
    ts = [eye + p for p in powers]
    k = 1
    while 2 * k < INV_BASE:
        powers = [_dot(p, p) for p in powers]
        ts = [t + _dot(t, p) for t, p in zip(ts, powers)]
        k *= 2
    size = INV_BASE
    while size < n:
        sel = same_block(2 * size) & jnp.logical_not(same_block(size))
        halves = [_dot(t, jnp.where(sel, a, 0.0)) for t, a in zip(ts, mats)]
        ts = [t - _dot(half, t) for t, half in zip(ts, halves)]
        size *= 2
    return ts


def _rwkv_kernel(*refs, has_vres):
    if has_vres:
        (p_ref, vf_ref, mu_ref, w0_ref, wup_ref, a0_ref, aup_ref, gup_ref, kk_ref, ka_ref, rk_ref,
         lw_ref, lb_ref, bd64_ref, v0_ref, vdn_ref, vup_ref, y_ref, state_ref, hist_ref) = refs
    else:
        (p_ref, mu_ref, w0_ref, wup_ref, a0_ref, aup_ref, gup_ref, kk_ref, ka_ref, rk_ref,
         lw_ref, lb_ref, bd64_ref, y_ref, v_ref, state_ref, hist_ref) = refs
    nb, cs = y_ref.shape[0], y_ref.shape[1]
    heads = GROUP // HEAD
    ci = pl.program_id(1)

    @pl.when(ci == 0)
    def _():
        state_ref[...] = jnp.zeros_like(state_ref)
        hist_ref[...] = jnp.zeros_like(hist_ref)

    row = lax.broadcasted_iota(jnp.int32, (cs, cs), 0)
    col = lax.broadcasted_iota(jnp.int32, (cs, cs), 1)
    lower_incl = col <= row
    lower_strict = col < row
    masks = [_lane_mask(GROUP, h * HEAD, (h + 1) * HEAD) for h in range(heads)]
    batch = range(nb)
    rows = lambda x, i: x[i * cs:(i + 1) * cs, :]

    ps = [p_ref[i] for i in batch]
    shifted = [_shift_rows(hist_ref[i], ps[i], 1) for i in batch]
    for i in batch:
        hist_ref[i] = ps[i][cs - HIST:, :]
    p = jnp.concatenate(ps, axis=0)
    xs = p + (jnp.concatenate(shifted, axis=0) - p) * mu_ref[...]

    r = xs[:, 0:GROUP]
    k = xs[:, GROUP:2 * GROUP]
    v = xs[:, 2 * GROUP:3 * GROUP]
    lora = xs[:, 3 * GROUP:]
    w = -_softplus(-(w0_ref[...] + _dot(jnp.tanh(lora), wup_ref[...]))) - 0.5
    a = _sigmoid(a0_ref[...] + _dot(lora, aup_ref[...]))
    g = _dot(_sigmoid(lora), gup_ref[...])
    if has_vres:
        mix = _sigmoid(v0_ref[...] + _dot(_dot(v, vdn_ref[...]), vup_ref[...]))
        v = v + (jnp.concatenate([vf_ref[i] for i in batch], axis=0) - v) * mix
    else:
        for i in batch:
            v_ref[i] = rows(v, i)

    bd64 = bd64_ref[...]
    kk = k * kk_ref[...]
    kk = kk / jnp.maximum(jnp.sqrt(_seg_sum(kk * kk, bd64)), 1e-12)
    kmod = k * (1.0 + (a - 1.0) * ka_ref[...])
    log_decay = -jnp.exp(w)

    cum_wide = _dot_split3_lhs_exact(jnp.where(lower_incl, 1.0, 0.0).astype(BF16),
                                     jnp.concatenate([rows(log_decay, i) for i in batch], axis=1))
    cum = jnp.concatenate([cum_wide[:, i * GROUP:(i + 1) * GROUP] for i in batch], axis=0)
    g_inc = jnp.exp(cum)
    g_inv = jnp.exp(-cum)
    kap = kk * jnp.exp(cum - log_decay)
    rt = r * g_inc
    bet = kk * a * g_inv
    kt = kmod * g_inv

    lhs = [jnp.concatenate([rows(kap, i), rows(rt, i)], axis=0) for i in batch]
    rhs = [jnp.concatenate([rows(bet, i), rows(kt, i)], axis=0).astype(BF16) for i in batch]
    pairs = [(i, h) for i in batch for h in range(heads)]
    grams = [_dot_nt(jnp.where(masks[h], lhs[i], 0.0), rhs[i]) for i, h in pairs]
    a_b = [jnp.where(lower_strict, gm[0:cs, 0:cs], 0.0) for gm in grams]
    a_k = [jnp.where(lower_strict, gm[0:cs, cs:], 0.0).astype(BF16) for gm in grams]
    b_r = [jnp.where(lower_incl, gm[cs:, 0:cs], 0.0).astype(BF16) for gm in grams]
    k_r = [jnp.where(lower_incl, gm[cs:, cs:], 0.0).astype(BF16) for gm in grams]
    t_inv = [t.astype(BF16) for t in _unit_lower_inverses(a_b, row, col)]

    def stack_heads(x):
        return jnp.concatenate([jnp.where(m, x, 0.0) for m in masks], axis=0).astype(BF16)

    per_row = lambda mats, i: jnp.concatenate(mats[i * heads:(i + 1) * heads], axis=1)
    dotf = lambda x, y: jnp.dot(x, y, preferred_element_type=F32)
    states = [state_ref[i] for i in batch]
    states_bf = [s.astype(BF16) for s in states]
    v_stack = [stack_heads(rows(v, i)) for i in batch]
    rhs_u = [_dot(rows(kap, i), states_bf[i]) + dotf(per_row(a_k, i), v_stack[i]) for i in batch]
    u = [dotf(per_row(t_inv, i), stack_heads(rhs_u[i])) for i in batch]
    y = [_dot(rows(rt, i), states_bf[i]) + dotf(per_row(k_r, i), v_stack[i])
         - dotf(per_row(b_r, i), stack_heads(u[i])) for i in batch]

    upd = [_dot(jnp.transpose(rows(kt, i)), rows(v, i)) - _dot(jnp.transpose(rows(bet, i)), u[i])
           for i in batch]
    bd_mask = bd64 > 0
    for i in batch:
        g_end = jnp.transpose(rows(g_inc, i))[:, cs - 1:cs]
        state_ref[i] = g_end * (states[i] + jnp.where(bd_mask, upd[i], 0.0))

    y = jnp.concatenate(y, axis=0)
    mean = _seg_sum(y, bd64) * (1.0 / HEAD)
    yc = y - mean
    var = _seg_sum(yc * yc, bd64) * (1.0 / HEAD)
    yn = yc * lax.rsqrt(var + LNX_EPS) * lw_ref[...] + lb_ref[...]
    bonus = _seg_sum(r * kmod * rk_ref[...], bd64) * v
    out = ((yn + bonus) * g).astype(y_ref.dtype)
    for i in batch:
        y_ref[i] = rows(out, i)


def _pad_rows(w, start, total):
    out = jnp.zeros((total, w.shape[1]), w.dtype)
    return lax.dynamic_update_slice(out, w, (start, 0))


def _mixer_rwkv7(p_d, mu, w0, w_up, a0, a_up, g_up, k_k, k_a, r_k, lnx_w, lnx_b, v_first, v_res):
    bsz, seq, _ = p_d.shape
    cs = min(RWKV_CHUNK, seq)
    lora = LORA_W + LORA_A + LORA_G
    row = lambda v: v.reshape(1, -1)
    has_vres = v_res is not None
    nb = math.gcd(RWKV_BATCH, bsz)
    chunk = lambda n: pl.BlockSpec((nb, cs, n), lambda b, i: (b, i, 0))
    args = [p_d]
    specs = [chunk(N_D)]
    if has_vres:
        args.append(v_first)
        specs.append(chunk(GROUP))
    args += [row(mu), row(w0), _pad_rows(w_up, 0, lora).astype(BF16),
             row(a0), _pad_rows(a_up, LORA_W, lora).astype(BF16),
             _pad_rows(g_up, LORA_W + LORA_A, lora).astype(BF16),
             row(k_k), row(k_a), row(r_k), row(lnx_w), row(lnx_b), _block_diag_ones(GROUP, HEAD)]
    specs += [_full((1, N_D)), _full((1, GROUP)), _full((lora, GROUP)),
              _full((1, GROUP)), _full((lora, GROUP)), _full((lora, GROUP)),
              _full((1, GROUP)), _full((1, GROUP)), _full((1, GROUP)), _full((1, GROUP)),
              _full((1, GROUP)), _full((GROUP, GROUP))]
    if has_vres:
        v0, v_down, v_up = v_res
        rank = v_down.shape[1]
        vdn = jnp.zeros((GROUP, lora), F32).at[:, :rank].set(v_down).astype(BF16)
        args += [row(v0), vdn, _pad_rows(v_up, 0, lora).astype(BF16)]
        specs += [_full((1, GROUP)), _full((GROUP, lora)), _full((lora, GROUP))]
        out_shape = jax.ShapeDtypeStruct((bsz, seq, GROUP), BF16)
        out_specs = chunk(GROUP)
    else:
        out_shape = [jax.ShapeDtypeStruct((bsz, seq, GROUP), BF16),
                     jax.ShapeDtypeStruct((bsz, seq, GROUP), F32)]
        out_specs = [chunk(GROUP), chunk(GROUP)]
    out = pl.pallas_call(
        functools.partial(_rwkv_kernel, has_vres=has_vres),
        out_shape=out_shape,
        grid=(bsz // nb, seq // cs),
        in_specs=specs,
        out_specs=out_specs,
        scratch_shapes=[pltpu.VMEM((nb, GROUP, GROUP), F32), pltpu.VMEM((nb, HIST, N_D), F32)],
        compiler_params=pltpu.CompilerParams(dimension_semantics=("arbitrary", "arbitrary"),
                                             vmem_limit_bytes=VMEM_LIMIT),
        name="rwkv7_chunked",
    )(*args)
    if has_vres:
        return out, None
    return out[0], out[1]


def _out_ffn_kernel(x_ref, ya_ref, yb_ref, yc_ref, yd_ref, mod_ref, g_ref, wo_ref, wup_ref,
                    cw_ref, cb_ref, wdn_ref, o_ref, hist_ref):
    tm, d = x_ref.shape[1], x_ref.shape[2]
    d_ff = wdn_ref.shape[0]
    ti = pl.program_id(1)

    @pl.when(ti == 0)
    def _():
        hist_ref[...] = jnp.zeros_like(hist_ref)

    mod = mod_ref[0]
    gate1 = mod[:, 2 * d:3 * d]
    shift2, scale2, gate2 = mod[:, 3 * d:4 * d], mod[:, 4 * d:5 * d], mod[:, 5 * d:6 * d]
    mix = jnp.dot(ya_ref[0], wo_ref[0:GROUP, :], preferred_element_type=F32)
    for j, y_ref in enumerate((yb_ref, yc_ref, yd_ref), start=1):
        mix = mix + jnp.dot(y_ref[0], wo_ref[j * GROUP:(j + 1) * GROUP, :],
                            preferred_element_type=F32)
    x1 = x_ref[0] + gate1 * mix
    h = _modulated_norm(x1, g_ref[...], shift2, scale2).astype(BF16)

    def conv(u, lo):
        cw = cw_ref[:, lo:lo + FF_CHUNK]
        hist = hist_ref[:, lo:lo + FF_CHUNK]
        out = cw[FF_CONV - 1:FF_CONV] * u + cb_ref[:, lo:lo + FF_CHUNK]
        for dly in range(1, FF_CONV):
            out = out + cw[FF_CONV - 1 - dly:FF_CONV - dly] * _shift_rows(hist, u, dly)
        hist_ref[:, lo:lo + FF_CHUNK] = u[tm - HIST:, :]
        return out

    def up_proj(j):
        lo_g, lo_v = j * FF_CHUNK, d_ff + j * FF_CHUNK
        return (jnp.dot(h, wup_ref[:, lo_g:lo_g + FF_CHUNK], preferred_element_type=F32),
                jnp.dot(h, wup_ref[:, lo_v:lo_v + FF_CHUNK], preferred_element_type=F32))

    n_chunks = d_ff // FF_CHUNK
    acc = jnp.zeros((tm, d), F32)
    ahead = up_proj(0)
    act_prev = None
    for j in range(n_chunks):
        raw_g, raw_v = ahead
        if j + 1 < n_chunks:
            ahead = up_proj(j + 1)
        if act_prev is not None:
            lo_p = (j - 1) * FF_CHUNK
            acc = acc + jnp.dot(act_prev, wdn_ref[lo_p:lo_p + FF_CHUNK, :],
                                preferred_element_type=F32)
        lo_g, lo_v = j * FF_CHUNK, d_ff + j * FF_CHUNK
        u_g = conv(raw_g, lo_g)
        u_v = conv(raw_v, lo_v)
        act_prev = (u_g * _sigmoid(u_g) * u_v).astype(BF16)
    lo_p = (n_chunks - 1) * FF_CHUNK
    acc = acc + jnp.dot(act_prev, wdn_ref[lo_p:lo_p + FF_CHUNK, :], preferred_element_type=F32)
    o_ref[0] = x1 + gate2 * acc


def _out_ffn(x, ys, mod, g2, w_out, w_up, conv_w, conv_b, w_down, layer=0):
    bsz, seq, d = x.shape
    w_out, w_up, w_down = _stacked(w_out), _stacked(w_up), _stacked(w_down)
    d_ff = w_down.shape[-2]
    tm = min(FFN_ROW_TILE, seq)
    tile = lambda n: pl.BlockSpec((1, tm, n), lambda b, i: (b, i, 0))
    return pl.pallas_call(
        _out_ffn_kernel,
        out_shape=jax.ShapeDtypeStruct((bsz, seq, d), F32),
        grid=(bsz, seq // tm),
        in_specs=[tile(d), tile(GROUP), tile(GROUP), tile(GROUP), tile(GROUP),
                  pl.BlockSpec((1, 1, mod.shape[-1]), lambda b, i: (b, 0, 0)),
                  _full((1, d)), _resident_layer((4 * GROUP, d), layer),
                  _resident_layer((d, 2 * d_ff), layer),
                  _full((FF_CONV, 2 * d_ff)), _full((1, 2 * d_ff)),
                  _resident_layer((d_ff, d), layer)],
        out_specs=tile(d),
        scratch_shapes=[pltpu.VMEM((HIST, 2 * d_ff), F32)],
        compiler_params=pltpu.CompilerParams(dimension_semantics=("arbitrary", "arbitrary"),
                                             vmem_limit_bytes=VMEM_LIMIT),
        name="out_proj_ffn",
    )(x, *ys, mod, g2.reshape(1, d), w_out, w_up, conv_w, conv_b.reshape(1, -1), w_down)


def kernel(x, c, w_ada, b_ada, norm1_g, norm2_g, w_in, w_out, a_qnorm_g, a_knorm_g, a_lam_q, a_lam_k, a_out_g, b_out_g, c_conv_w, c_conv_b, c_gate_a_w, c_gate_a_b, c_gate_x_w, c_gate_x_b, c_lambda, c_out_g, d_mu, d_w0, d_w_up, d_a0, d_a_up, d_g_up, d_k_k, d_k_a, d_r_k, d_lnx_w, d_lnx_b, d_v0, d_v_down, d_v_up, ff_w_up, ff_conv_w, ff_conv_b, ff_w_down):
    depth = w_in.shape[0]
    bsz = x.shape[0]
    mods = _ada_modulation(c, w_ada, b_ada)
    w_in_bf, w_out_bf = w_in.astype(BF16), w_out.astype(BF16)
    ff_w_up_bf, ff_w_down_bf = ff_w_up.astype(BF16), ff_w_down.astype(BF16)
    v_first = None
    for l in range(depth):
        mod = mods[l].reshape(bsz, 1, -1)
        p_a, p_b, p_c, p_d = _in_proj(x, mod, norm1_g[l], w_in_bf, l)
        lam_init = 0.8 - 0.6 * math.exp(-0.3 * l)
        y_a, y_b = _mixer_attention_pair(p_a, a_qnorm_g[l], a_knorm_g[l], a_lam_q[l], a_lam_k[l],
                                         a_out_g[l], lam_init, p_b, b_out_g[l])
        y_c = _mixer_rglru(p_c, c_conv_w[l], c_conv_b[l], c_gate_a_w[l], c_gate_a_b[l],
                           c_gate_x_w[l], c_gate_x_b[l], c_lambda[l], c_out_g[l])
        v_res = None if l == 0 else (d_v0[l - 1], d_v_down[l - 1], d_v_up[l - 1])
        y_d, v_d = _mixer_rwkv7(p_d, d_mu[l], d_w0[l], d_w_up[l], d_a0[l], d_a_up[l], d_g_up[l],
                                d_k_k[l], d_k_a[l], d_r_k[l].reshape(-1), d_lnx_w[l], d_lnx_b[l],
                                v_first, v_res)
        if l == 0:
            v_first = v_d
        x = _out_ffn(x, (y_a, y_b, y_c, y_d), mod, norm2_g[l], w_out_bf, ff_w_up_bf,
                     ff_conv_w[l], ff_conv_b[l], ff_w_down_bf, l)
    return x
```

```python
import functools
import math

import numpy as np
import jax
import jax.numpy as jnp
from jax import lax
from jax.experimental import pallas as pl
from jax.experimental.pallas import tpu as pltpu

F32 = jnp.float32
BF16 = jnp.bfloat16

GROUP = 256
A_HEADS = 4
A_QK = 32
HEAD = 64
N_A = 3 * GROUP
N_B = 3 * GROUP
N_C = 2 * GROUP
LORA_W = 32
LORA_A = 32
LORA_G = 64
N_D = 3 * GROUP + LORA_W + LORA_A + LORA_G
C_CONV = 4
C_EXP = 8.0
FF_CONV = 3
LOG2_E = 1.4426950408889634
RMS_EPS = 1e-6
LNX_EPS = 64e-5
ADA_CHUNKS = 6

ROW_TILE = 512
FFN_ROW_TILE = 512
ATT_TILE = 256
LRU_TILE = 256
RWKV_CHUNK = 128
RWKV_BATCH = 4
FF_CHUNK = 256
FF_DOWN_GROUP = 4
SUM_ROWS = 16
INV_BASE = 8
HIST = 8
VMEM_LIMIT = 56 * 1024 * 1024


def _dot(a, b):
    return jnp.dot(a.astype(BF16), b.astype(BF16), preferred_element_type=F32)


def _dot_nt(a, b):
    return lax.dot_general(a.astype(BF16), b.astype(BF16), (((1,), (1,)), ((), ())),
                           preferred_element_type=F32)


def _split2(x):
    hi = x.astype(BF16)
    lo = (x - hi.astype(F32)).astype(BF16)
    return hi, lo


def _dot_split2(x, m):
    hi, lo = _split2(x)
    return (jnp.dot(hi, m, preferred_element_type=F32)
            + jnp.dot(lo, m, preferred_element_type=F32))


def _dot_split3_lhs_exact(m, x):
    hi = x.astype(BF16)
    r1 = x - hi.astype(F32)
    mid = r1.astype(BF16)
    lo = (r1 - mid.astype(F32)).astype(BF16)
    return (jnp.dot(m, hi, preferred_element_type=F32)
            + jnp.dot(m, mid, preferred_element_type=F32)
            + jnp.dot(m, lo, preferred_element_type=F32))


def _sigmoid(x):
    return 1.0 / (1.0 + jnp.exp(-x))


def _softplus(x):
    return jnp.maximum(x, 0.0) + jnp.log1p(jnp.exp(-jnp.abs(x)))


def _lane_mask(width, lo, hi):
    lane = lax.broadcasted_iota(jnp.int32, (1, width), 1)
    return (lane >= lo) & (lane < hi)


def _shift_rows(hist, x, d):
    ext = jnp.concatenate([hist, x], axis=0)
    return pltpu.roll(ext, d, axis=0)[HIST:, :]


def _full(shape):
    nd = len(shape)
    return pl.BlockSpec(shape, lambda *_: (0,) * nd)


def _resident(shape):
    nd = len(shape)
    return pl.BlockSpec(shape, lambda *_: (0,) * nd, pipeline_mode=pl.Buffered(1))


def _resident_layer(shape, layer):
    return pl.BlockSpec((None,) + tuple(shape), lambda *_: (layer, 0, 0),
                        pipeline_mode=pl.Buffered(1))


def _stacked(w):
    return w[None] if w.ndim == 2 else w


def _block_diag_ones(width, block):
    idx = np.arange(width) // block
    return jnp.asarray((idx[:, None] == idx[None, :]).astype(np.float32), dtype=BF16)


def _ada_kernel(c_ref, w_ref, b_ref, o_ref):
    c = c_ref[...]
    cond = c * _sigmoid(c)
    o_ref[0] = _dot(cond, w_ref[0]) + b_ref[0]


def _ada_modulation(c, w_ada, b_ada):
    depth, d, n = w_ada.shape
    bsz = c.shape[0]
    tn = 1536
    return pl.pallas_call(
        _ada_kernel,
        out_shape=jax.ShapeDtypeStruct((depth, bsz, n), F32),
        grid=(depth, n // tn),
        in_specs=[pl.BlockSpec((bsz, d), lambda l, j: (0, 0)),
                  pl.BlockSpec((1, d, tn), lambda l, j: (l, 0, j)),
                  pl.BlockSpec((1, 1, tn), lambda l, j: (l, 0, j))],
        out_specs=pl.BlockSpec((1, bsz, tn), lambda l, j: (l, 0, j)),
        compiler_params=pltpu.CompilerParams(dimension_semantics=("arbitrary", "arbitrary"),
                                             vmem_limit_bytes=VMEM_LIMIT),
        name="ada_modulation",
    )(c, w_ada, b_ada.reshape(depth, 1, n))


def _modulated_norm(x, g, shift, scale):
    ms = jnp.mean(x * x, axis=-1, keepdims=True)
    return (x * lax.rsqrt(ms + RMS_EPS) * g) * (1.0 + scale) + shift


def _in_proj_kernel(x_ref, mod_ref, g_ref, w_ref, pa_ref, pb_ref, pc_ref, pd_ref):
    d = x_ref.shape[-1]
    mod = mod_ref[0]
    h = _modulated_norm(x_ref[0], g_ref[...], mod[:, 0:d], mod[:, d:2 * d]).astype(BF16)
    pa_ref[0] = jnp.dot(h, w_ref[:, 0:N_A], preferred_element_type=F32)
    pb_ref[0] = jnp.dot(h, w_ref[:, N_A:N_A + N_B], preferred_element_type=F32)
    pc_ref[0] = jnp.dot(h, w_ref[:, N_A + N_B:N_A + N_B + N_C], preferred_element_type=F32)
    pd_ref[0] = jnp.dot(h, w_ref[:, N_A + N_B + N_C:], preferred_element_type=F32)


def _in_proj(x, mod, g, w_bf16, layer=0):
    bsz, seq, d = x.shape
    w_bf16 = _stacked(w_bf16)
    n_in = w_bf16.shape[-1]
    tm = min(ROW_TILE, seq)
    widths = (N_A, N_B, N_C, N_D)
    return pl.pallas_call(
        _in_proj_kernel,
        out_shape=[jax.ShapeDtypeStruct((bsz, seq, n), F32) for n in widths],
        grid=(bsz, seq // tm),
        in_specs=[pl.BlockSpec((1, tm, d), lambda b, i: (b, i, 0)),
                  pl.BlockSpec((1, 1, mod.shape[-1]), lambda b, i: (b, 0, 0)),
                  _full((1, d)),
                  _resident_layer((d, n_in), layer)],
        out_specs=[pl.BlockSpec((1, tm, n), lambda b, i: (b, i, 0)) for n in widths],
        compiler_params=pltpu.CompilerParams(dimension_semantics=("arbitrary", "arbitrary"),
                                             vmem_limit_bytes=VMEM_LIMIT),
        name="in_proj",
    )(x, mod, g.reshape(1, d), w_bf16)


def _group_rms(x, bd, group, gain):
    ms = _dot_split2(x * x, bd) * (1.0 / group)
    return x * lax.rsqrt(ms + RMS_EPS) * gain


def _attn_a_setup(p_ref, qg_ref, kg_ref, lq_ref, lk_ref, og_ref, bd32_ref,
                  o_ref, kn_ref, vt_ref, lam_init):
    tq = o_ref.shape[1]
    n_kt = vt_ref.shape[0]
    qi = pl.program_id(1)
    bd32 = bd32_ref[...]

    @pl.when(qi == 0)
    def _():
        k = p_ref[0, :, GROUP:2 * GROUP]
        kn_ref[...] = _group_rms(k, bd32, A_QK, kg_ref[...]).astype(BF16)
        ones_rows = jnp.ones((SUM_ROWS, tq), BF16)
        for j in range(n_kt):
            vt = jnp.transpose(p_ref[0, j * tq:(j + 1) * tq, 2 * GROUP:3 * GROUP])
            for h in range(A_HEADS):
                vt_ref[j, h, 0:HEAD, :] = vt[h * HEAD:(h + 1) * HEAD, :].astype(BF16)
                vt_ref[j, h, HEAD:, :] = ones_rows

    lq = lq_ref[...]
    lk = lk_ref[...]
    lam = (jnp.exp(jnp.sum(lq[0:1] * lk[0:1], axis=-1, keepdims=True))
           - jnp.exp(jnp.sum(lq[1:2] * lk[1:2], axis=-1, keepdims=True)) + lam_init)

    q0 = pl.multiple_of(qi * tq, tq)
    q = p_ref[0, pl.ds(q0, tq), 0:GROUP]
    qn = _group_rms(q, bd32, A_QK, qg_ref[...]) * (A_QK ** -0.5 * LOG2_E)

    key_idx = lax.broadcasted_iota(jnp.int32, (tq, tq), 0)
    query_idx = lax.broadcasted_iota(jnp.int32, (tq, tq), 1)
    causal = key_idx <= query_idx

    qt = jnp.transpose(qn)
    feat = lax.broadcasted_iota(jnp.int32, (GROUP, 1), 0)
    qms = [[jnp.where((feat >= h * HEAD + c * A_QK) & (feat < h * HEAD + (c + 1) * A_QK), qt, 0.0
                      ).astype(BF16) for h in range(A_HEADS)] for c in range(2)]

    def key_tile(kj, carry, diag):
        kb = kn_ref[pl.ds(pl.multiple_of(kj * tq, tq), tq), :]
        dotf = lambda x, y: jnp.dot(x, y, preferred_element_type=F32)
        chains = [(c, h) for c in range(2) for h in range(A_HEADS)]
        ms = [carry[c][0][h] for c, h in chains]
        ls = [carry[c][1][h] for c, h in chains]
        accs = [carry[c][2][h] for c, h in chains]
        ss = [dotf(kb, qms[c][h]) for c, h in chains]
        yield
        if diag:
            ss = [jnp.where(causal, s, -jnp.inf) for s in ss]
        ms_new = [jnp.maximum(m, jnp.max(s, axis=0, keepdims=True)) for m, s in zip(ms, ss)]
        alphas = [jnp.exp2(m - m_new) for m, m_new in zip(ms, ms_new)]
        ps = [jnp.exp2(s - m_new).astype(BF16) for s, m_new in zip(ss, ms_new)]
        yield
        res = [dotf(vt_ref[kj, h], p) for (c, h), p in zip(chains, ps)]
        yield
        accs_new = [alpha * acc + r[0:HEAD, :] for alpha, acc, r in zip(alphas, accs, res)]
        ls_new = [alpha * l + r[HEAD:HEAD + 1, :] for alpha, l, r in zip(alphas, ls, res)]
        n = A_HEADS
        return tuple((tuple(ms_new[c * n:(c + 1) * n]), tuple(ls_new[c * n:(c + 1) * n]),
                      tuple(accs_new[c * n:(c + 1) * n])) for c in range(2))

    init_c = (tuple(jnp.full((1, tq), -jnp.inf, F32) for _ in range(A_HEADS)),
              tuple(jnp.zeros((1, tq), F32) for _ in range(A_HEADS)),
              tuple(jnp.zeros((HEAD, tq), F32) for _ in range(A_HEADS)))
    def finish(carry):
        (_, l0, acc0), (_, l1, acc1) = carry
        og = og_ref[...]
        ys = []
        for h in range(A_HEADS):
            o = acc0[h] * (1.0 / l0[h]) - lam * (acc1[h] * (1.0 / l1[h]))
            ms = jnp.mean(o * o, axis=0, keepdims=True)
            ys.append(o * lax.rsqrt(ms + RMS_EPS) * og[h * HEAD:(h + 1) * HEAD, :])
        y = jnp.transpose(jnp.concatenate(ys, axis=0)) * (1.0 - lam_init)
        o_ref[0] = y.astype(o_ref.dtype)

    return (init_c, init_c), key_tile, finish


def _run_steps(*gens):
    results = [None] * len(gens)
    live = list(range(len(gens)))
    while live:
        for idx in list(live):
            try:
                next(gens[idx])
            except StopIteration as done:
                results[idx] = done.value
                live.remove(idx)
    return results


def _attn_a_kernel(*refs, lam_init):
    qi = pl.program_id(1)
    init, key_tile, finish = _attn_a_setup(*refs, lam_init)
    carry = lax.fori_loop(0, qi, lambda kj, carry: _run_steps(key_tile(kj, carry, False))[0], init)
    finish(_run_steps(key_tile(qi, carry, True))[0])


def _mixer_diff_attn(p_a, q_g, k_g, lam_q, lam_k, out_g, lam_init):
    bsz, seq, _ = p_a.shape
    tq = min(ATT_TILE, seq)
    reps = GROUP // A_QK
    return pl.pallas_call(
        functools.partial(_attn_a_kernel, lam_init=lam_init),
        out_shape=jax.ShapeDtypeStruct((bsz, seq, GROUP), BF16),
        grid=(bsz, seq // tq),
        in_specs=[pl.BlockSpec((1, seq, N_A), lambda b, i: (b, 0, 0)),
                  _full((1, GROUP)), _full((1, GROUP)),
                  _full((2, A_QK)), _full((2, A_QK)),
                  _full((GROUP, 1)),
                  _full((GROUP, GROUP))],
        out_specs=pl.BlockSpec((1, tq, GROUP), lambda b, i: (b, i, 0)),
        scratch_shapes=[pltpu.VMEM((seq, GROUP), BF16),
                        pltpu.VMEM((seq // tq, A_HEADS, HEAD + SUM_ROWS, tq), BF16)],
        compiler_params=pltpu.CompilerParams(dimension_semantics=("arbitrary", "arbitrary"),
                                             vmem_limit_bytes=VMEM_LIMIT),
        name="diff_attention",
    )(p_a, jnp.tile(q_g, reps).reshape(1, GROUP), jnp.tile(k_g, reps).reshape(1, GROUP),
      lam_q, lam_k, out_g.reshape(GROUP, 1), _block_diag_ones(GROUP, A_QK))


def _attn_b_setup(p_ref, og_ref, tri_ref, o_ref, kb_ref, vt_ref):
    tq = o_ref.shape[1]
    n_kt = vt_ref.shape[0]
    heads = GROUP // HEAD
    qi = pl.program_id(1)
    feat = lax.broadcasted_iota(jnp.int32, (GROUP, 1), 0)
    head_rows = [(feat >= h * HEAD) & (feat < (h + 1) * HEAD) for h in range(heads)]

    @pl.when(qi == 0)
    def _():
        kb_ref[...] = p_ref[0, :, GROUP:2 * GROUP].astype(BF16)
        for j in range(n_kt):
            vt = jnp.transpose(p_ref[0, j * tq:(j + 1) * tq, 2 * GROUP:3 * GROUP])
            vt_ref[j] = vt.astype(BF16)

    q0 = pl.multiple_of(qi * tq, tq)
    qt = jnp.transpose(p_ref[0, pl.ds(q0, tq), 0:GROUP] * (HEAD ** -0.5 * LOG2_E))
    qms = [jnp.where(hr, qt, 0.0).astype(BF16) for hr in head_rows]
    tri = tri_ref[...]
    key_idx = lax.broadcasted_iota(jnp.int32, (tq, tq), 0)
    query_idx = lax.broadcasted_iota(jnp.int32, (tq, tq), 1)
    strict = key_idx < query_idx
    dotf = lambda x, y: jnp.dot(x, y, preferred_element_type=F32)

    def key_tile(kj, carry, diag):
        laters, accs = carry
        kb = kb_ref[pl.ds(pl.multiple_of(kj * tq, tq), tq), :]
        zs = [dotf(kb, qm) for qm in qms]
        yield
        neg_keeps = [jnp.maximum(z, 0.0) + jnp.log2(1.0 + jnp.exp2(-jnp.abs(z))) for z in zs]
        if diag:
            neg_keeps = [jnp.where(strict, nk, 0.0) for nk in neg_keeps]
        splits = [jnp.concatenate(_split2(nk), axis=0) for nk in neg_keeps]
        yield
        incls = [dotf(tri, sp) for sp in splits]
        yield
        ws = [jnp.exp2(z + incl + later) for z, incl, later in zip(zs, incls, laters)]
        if diag:
            ws = [jnp.where(strict, w, 0.0) for w in ws]
        yield
        accs = tuple(acc + dotf(vt_ref[kj, h * HEAD:(h + 1) * HEAD, :], w.astype(BF16))
                     for h, (acc, w) in enumerate(zip(accs, ws)))
        return tuple(later + incl[0:1, :] for later, incl in zip(laters, incls)), accs

    init = (tuple(jnp.zeros((1, tq), F32) for _ in range(heads)),
            tuple(jnp.zeros((HEAD, tq), F32) for _ in range(heads)))

    def finish(carry):
        _, accs = carry
        og = og_ref[...]
        ys = []
        for h in range(heads):
            o = accs[h]
            ms = jnp.mean(o * o, axis=0, keepdims=True)
            ys.append(o * lax.rsqrt(ms + RMS_EPS) * og[h * HEAD:(h + 1) * HEAD, :])
        o_ref[0] = jnp.transpose(jnp.concatenate(ys, axis=0)).astype(o_ref.dtype)

    return init, key_tile, finish


def _attn_b_kernel(*refs):
    qi = pl.program_id(1)
    init, key_tile, finish = _attn_b_setup(*refs)
    carry = _run_steps(key_tile(qi, init, True))[0]
    finish(lax.fori_loop(
        0, qi, lambda i, carry: _run_steps(key_tile(qi - 1 - i, carry, False))[0], carry))


def _attn_ab_kernel(pa_ref, qg_ref, kg_ref, lq_ref, lk_ref, oga_ref, bd32_ref, pb_ref, ogb_ref, tri_ref,
                    oa_ref, ob_ref, kn_ref, vta_ref, kb_ref, vtb_ref, *, lam_init):
    qi = pl.program_id(1)
    a_init, a_tile, a_finish = _attn_a_setup(pa_ref, qg_ref, kg_ref, lq_ref, lk_ref, oga_ref, bd32_ref,
                                             oa_ref, kn_ref, vta_ref, lam_init)
    b_init, b_tile, b_finish = _attn_b_setup(pb_ref, ogb_ref, tri_ref, ob_ref, kb_ref, vtb_ref)
    both = lambda kj, c, diag: tuple(_run_steps(a_tile(kj, c[0], diag), b_tile(kj, c[1], diag)))
    carry = both(qi, (a_init, b_init), True)
    carry = lax.fori_loop(0, qi, lambda i, c: both(qi - 1 - i, c, False), carry)
    a_finish(carry[0])
    b_finish(carry[1])


def _mixer_stick_breaking(p_b, out_g):
    bsz, seq, _ = p_b.shape
    tq = min(ATT_TILE, seq)
    neg_upper = -np.triu(np.ones((tq, tq), np.float32))
    tri = jnp.asarray(np.concatenate([neg_upper, neg_upper], axis=1), dtype=BF16)
    return pl.pallas_call(
        _attn_b_kernel,
        out_shape=jax.ShapeDtypeStruct((bsz, seq, GROUP), BF16),
        grid=(bsz, seq // tq),
        in_specs=[pl.BlockSpec((1, seq, N_B), lambda b, i: (b, 0, 0)),
                  _full((GROUP, 1)), _full((tq, 2 * tq))],
        out_specs=pl.BlockSpec((1, tq, GROUP), lambda b, i: (b, i, 0)),
        scratch_shapes=[pltpu.VMEM((seq, GROUP), BF16),
                        pltpu.VMEM((seq // tq, GROUP, tq), BF16)],
        compiler_params=pltpu.CompilerParams(dimension_semantics=("arbitrary", "arbitrary"),
                                             vmem_limit_bytes=VMEM_LIMIT),
        name="stick_breaking_attention",
    )(p_b, out_g.reshape(GROUP, 1), tri)


def _mixer_attention_pair(p_a, q_g, k_g, lam_q, lam_k, a_out_g, lam_init, p_b, b_out_g):
    bsz, seq, _ = p_a.shape
    tq = min(ATT_TILE, seq)
    reps = GROUP // A_QK
    neg_upper = -np.triu(np.ones((tq, tq), np.float32))
    tri = jnp.asarray(np.concatenate([neg_upper, neg_upper], axis=1), dtype=BF16)
    whole = lambda n: pl.BlockSpec((1, seq, n), lambda b, i: (b, 0, 0))
    tile = pl.BlockSpec((1, tq, GROUP), lambda b, i: (b, i, 0))
    return pl.pallas_call(
        functools.partial(_attn_ab_kernel, lam_init=lam_init),
        out_shape=[jax.ShapeDtypeStruct((bsz, seq, GROUP), BF16)] * 2,
        grid=(bsz, seq // tq),
        in_specs=[whole(N_A), _full((1, GROUP)), _full((1, GROUP)),
                  _full((2, A_QK)), _full((2, A_QK)), _full((GROUP, 1)), _full((GROUP, GROUP)),
                  whole(N_B), _full((GROUP, 1)), _full((tq, 2 * tq))],
        out_specs=[tile, tile],
        scratch_shapes=[pltpu.VMEM((seq, GROUP), BF16),
                        pltpu.VMEM((seq // tq, A_HEADS, HEAD + SUM_ROWS, tq), BF16),
                        pltpu.VMEM((seq, GROUP), BF16),
                        pltpu.VMEM((seq // tq, GROUP, tq), BF16)],
        compiler_params=pltpu.CompilerParams(dimension_semantics=("arbitrary", "arbitrary"),
                                             vmem_limit_bytes=VMEM_LIMIT),
        name="attention_pair",
    )(p_a, jnp.tile(q_g, reps).reshape(1, GROUP), jnp.tile(k_g, reps).reshape(1, GROUP),
      lam_q, lam_k, a_out_g.reshape(GROUP, 1), _block_diag_ones(GROUP, A_QK),
      p_b, b_out_g.reshape(GROUP, 1), tri)


def _gelu_tanh(x):
    return 0.5 * x * (1.0 + jnp.tanh(math.sqrt(2.0 / math.pi) * (x + 0.044715 * (x * x * x))))


def _rglru_kernel(p_ref, cw_ref, cb_ref, gaw_ref, gab_ref, gxw_ref, gxb_ref, lam_ref, og_ref,
                  bd64_ref, o_ref, hist_ref, h_ref):
    tt = o_ref.shape[1]
    ti = pl.program_id(1)

    @pl.when(ti == 0)
    def _():
        hist_ref[...] = jnp.zeros_like(hist_ref)
        h_ref[...] = jnp.zeros_like(h_ref)

    x_raw = p_ref[0, :, 0:GROUP]
    x_gate = p_ref[0, :, GROUP:2 * GROUP]
    hist = hist_ref[...]
    cw = cw_ref[...]
    x = cw[C_CONV - 1:C_CONV] * x_raw + cb_ref[...]
    for d in range(1, C_CONV):
        x = x + cw[C_CONV - 1 - d:C_CONV - d] * _shift_rows(hist, x_raw, d)
    hist_ref[...] = x_raw[tt - HIST:, :]

    r = _sigmoid(_dot(x, gaw_ref[...]) + gab_ref[...])
    i = _sigmoid(_dot(x, gxw_ref[...]) + gxb_ref[...])
    log_a = (-C_EXP) * r * _softplus(-lam_ref[...])
    a = jnp.exp(log_a)
    mult = jnp.sqrt(jnp.tanh(-log_a) * (a * a + 1.0))
    row = lax.broadcasted_iota(jnp.int32, (tt, 1), 0)
    mult = jnp.where((row == 0) & (ti == 0), 1.0, mult)
    u = mult * i * x

    d = 1
    while d < tt:
        a_prev = pltpu.roll(a, d, axis=0)
        u_prev = pltpu.roll(u, d, axis=0)
        keep = row >= d
        u = jnp.where(keep, a * u_prev + u, u)
        a = jnp.where(keep, a * a_prev, a)
        d *= 2
    h = u + a * h_ref[0:1, :]
    h_ref[...] = jnp.broadcast_to(h[tt - 1:tt, :], h_ref.shape)

    y = h * _gelu_tanh(x_gate)
    o_ref[0] = _group_rms(y, bd64_ref[...], HEAD, og_ref[...]).astype(o_ref.dtype)


def _block_diag_weight(w):
    nb, d, _ = w.shape
    eye = jnp.eye(nb, dtype=w.dtype)
    return (eye[:, None, :, None] * w[:, :, None, :]).reshape(nb * d, nb * d)


def _mixer_rglru(p_c, conv_w, conv_b, ga_w, ga_b, gx_w, gx_b, lam, out_g):
    bsz, seq, _ = p_c.shape
    tt = min(LRU_TILE, seq)
    row = lambda v: v.reshape(1, GROUP)
    return pl.pallas_call(
        _rglru_kernel,
        out_shape=jax.ShapeDtypeStruct((bsz, seq, GROUP), BF16),
        grid=(bsz, seq // tt),
        in_specs=[pl.BlockSpec((1, tt, N_C), lambda b, i: (b, i, 0)),
                  _full((C_CONV, GROUP)), _full((1, GROUP)),
                  _full((GROUP, GROUP)), _full((1, GROUP)),
                  _full((GROUP, GROUP)), _full((1, GROUP)),
                  _full((1, GROUP)), _full((1, GROUP)), _full((GROUP, GROUP))],
        out_specs=pl.BlockSpec((1, tt, GROUP), lambda b, i: (b, i, 0)),
        scratch_shapes=[pltpu.VMEM((HIST, GROUP), F32), pltpu.VMEM((HIST, GROUP), F32)],
        compiler_params=pltpu.CompilerParams(dimension_semantics=("arbitrary", "arbitrary"),
                                             vmem_limit_bytes=VMEM_LIMIT),
        name="rg_lru",
    )(p_c, conv_w, row(conv_b), _block_diag_weight(ga_w).astype(BF16), row(ga_b),
      _block_diag_weight(gx_w).astype(BF16), row(gx_b), row(lam), row(out_g),
      _block_diag_ones(GROUP, HEAD))


def _seg_sum(x, bd):
    return jnp.dot(x.astype(BF16), bd, preferred_element_type=F32)


def _unit_lower_inverses(mats, row, col):
    n = mats[0].shape[0]

    def same_block(size):
        shift = size.bit_length() - 1
        return (row >> shift) == (col >> shift)

    base = same_block(INV_BASE)
    eye = (row == col).astype(F32)
    powers = [jnp.where(base, -a, 0.0) for a in mats]
    ts = [eye + p for p in powers]
    k = 1
    while 2 * k < INV_BASE:
        powers = [_dot(p, p) for p in powers]
        ts = [t + _dot(t, p) for t, p in zip(ts, powers)]
        k *= 2
    size = INV_BASE
    while size < n:
        sel = same_block(2 * size) & jnp.logical_not(same_block(size))
        halves = [_dot(t, jnp.where(sel, a, 0.0)) for t, a in zip(ts, mats)]
        ts = [t - _dot(half, t) for t, half in zip(ts, halves)]
        size *= 2
    return ts


def _rwkv_kernel(*refs, has_vres):
    if has_vres:
        (p_ref, vf_ref, mu_ref, w0_ref, wup_ref, a0_ref, aup_ref, gup_ref, kk_ref, ka_ref, rk_ref,
         lw_ref, lb_ref, bd64_ref, v0_ref, vdn_ref, vup_ref, y_ref, state_ref, hist_ref) = refs
    else:
        (p_ref, mu_ref, w0_ref, wup_ref, a0_ref, aup_ref, gup_ref, kk_ref, ka_ref, rk_ref,
         lw_ref, lb_ref, bd64_ref, y_ref, v_ref, state_ref, hist_ref) = refs
    nb, cs = y_ref.shape[0], y_ref.shape[1]
    heads = GROUP // HEAD
    ci = pl.program_id(1)

    @pl.when(ci == 0)
    def _():
        state_ref[...] = jnp.zeros_like(state_ref)
        hist_ref[...] = jnp.zeros_like(hist_ref)

    row = lax.broadcasted_iota(jnp.int32, (cs, cs), 0)
    col = lax.broadcasted_iota(jnp.int32, (cs, cs), 1)
    lower_incl = col <= row
    lower_strict = col < row
    masks = [_lane_mask(GROUP, h * HEAD, (h + 1) * HEAD) for h in range(heads)]
    batch = range(nb)
    rows = lambda x, i: x[i * cs:(i + 1) * cs, :]

    ps = [p_ref[i] for i in batch]
    shifted = [_shift_rows(hist_ref[i], ps[i], 1) for i in batch]
    for i in batch:
        hist_ref[i] = ps[i][cs - HIST:, :]
    p = jnp.concatenate(ps, axis=0)
    xs = p + (jnp.concatenate(shifted, axis=0) - p) * mu_ref[...]

    r = xs[:, 0:GROUP]
    k = xs[:, GROUP:2 * GROUP]
    v = xs[:, 2 * GROUP:3 * GROUP]
    lora = xs[:, 3 * GROUP:]
    w = -_softplus(-(w0_ref[...] + _dot(jnp.tanh(lora), wup_ref[...]))) - 0.5
    a = _sigmoid(a0_ref[...] + _dot(lora, aup_ref[...]))
    g = _dot(_sigmoid(lora), gup_ref[...])
    if has_vres:
        mix = _sigmoid(v0_ref[...] + _dot(_dot(v, vdn_ref[...]), vup_ref[...]))
        v = v + (jnp.concatenate([vf_ref[i] for i in batch], axis=0) - v) * mix
    else:
        for i in batch:
            v_ref[i] = rows(v, i)

    bd64 = bd64_ref[...]
    kk = k * kk_ref[...]
    kk = kk / jnp.maximum(jnp.sqrt(_seg_sum(kk * kk, bd64)), 1e-12)
    kmod = k * (1.0 + (a - 1.0) * ka_ref[...])
    log_decay = -jnp.exp(w)

    cum_wide = _dot_split3_lhs_exact(jnp.where(lower_incl, 1.0, 0.0).astype(BF16),
                                     jnp.concatenate([rows(log_decay, i) for i in batch], axis=1))
    cum = jnp.concatenate([cum_wide[:, i * GROUP:(i + 1) * GROUP] for i in batch], axis=0)
    g_inc = jnp.exp(cum)
    g_inv = jnp.exp(-cum)
    kap = kk * jnp.exp(cum - log_decay)
    rt = r * g_inc
    bet = kk * a * g_inv
    kt = kmod * g_inv

    lhs = [jnp.concatenate([rows(kap, i), rows(rt, i)], axis=0) for i in batch]
    rhs = [jnp.concatenate([rows(bet, i), rows(kt, i)], axis=0).astype(BF16) for i in batch]
    pairs = [(i, h) for i in batch for h in range(heads)]
    grams = [_dot_nt(jnp.where(masks[h], lhs[i], 0.0), rhs[i]) for i, h in pairs]
    a_b = [jnp.where(lower_strict, gm[0:cs, 0:cs], 0.0) for gm in grams]
    a_k = [jnp.where(lower_strict, gm[0:cs, cs:], 0.0).astype(BF16) for gm in grams]
    b_r = [jnp.where(lower_incl, gm[cs:, 0:cs], 0.0).astype(BF16) for gm in grams]
    k_r = [jnp.where(lower_incl, gm[cs:, cs:], 0.0).astype(BF16) for gm in grams]
    t_inv = [t.astype(BF16) for t in _unit_lower_inverses(a_b, row, col)]

    def stack_heads(x):
        return jnp.concatenate([jnp.where(m, x, 0.0) for m in masks], axis=0).astype(BF16)

    per_row = lambda mats, i: jnp.concatenate(mats[i * heads:(i + 1) * heads], axis=1)
    dotf = lambda x, y: jnp.dot(x, y, preferred_element_type=F32)
    states = [state_ref[i] for i in batch]
    states_bf = [s.astype(BF16) for s in states]
    v_stack = [stack_heads(rows(v, i)) for i in batch]
    rhs_u = [_dot(rows(kap, i), states_bf[i]) + dotf(per_row(a_k, i), v_stack[i]) for i in batch]
    u = [dotf(per_row(t_inv, i), stack_heads(rhs_u[i])) for i in batch]
    y = [_dot(rows(rt, i), states_bf[i]) + dotf(per_row(k_r, i), v_stack[i])
         - dotf(per_row(b_r, i), stack_heads(u[i])) for i in batch]

    upd = [_dot(jnp.transpose(rows(kt, i)), rows(v, i)) - _dot(jnp.transpose(rows(bet, i)), u[i])
           for i in batch]
    bd_mask = bd64 > 0
    for i in batch:
        g_end = jnp.transpose(rows(g_inc, i))[:, cs - 1:cs]
        state_ref[i] = g_end * (states[i] + jnp.where(bd_mask, upd[i], 0.0))

    y = jnp.concatenate(y, axis=0)
    mean = _seg_sum(y, bd64) * (1.0 / HEAD)
    yc = y - mean
    var = _seg_sum(yc * yc, bd64) * (1.0 / HEAD)
    yn = yc * lax.rsqrt(var + LNX_EPS) * lw_ref[...] + lb_ref[...]
    bonus = _seg_sum(r * kmod * rk_ref[...], bd64) * v
    out = ((yn + bonus) * g).astype(y_ref.dtype)
    for i in batch:
        y_ref[i] = rows(out, i)


def _pad_rows(w, start, total):
    out = jnp.zeros((total, w.shape[1]), w.dtype)
    return lax.dynamic_update_slice(out, w, (start, 0))


def _mixer_rwkv7(p_d, mu, w0, w_up, a0, a_up, g_up, k_k, k_a, r_k, lnx_w, lnx_b, v_first, v_res):
    bsz, seq, _ = p_d.shape
    cs = min(RWKV_CHUNK, seq)
    lora = LORA_W + LORA_A + LORA_G
    row = lambda v: v.reshape(1, -1)
    has_vres = v_res is not None
    nb = math.gcd(RWKV_BATCH, bsz)
    chunk = lambda n: pl.BlockSpec((nb, cs, n), lambda b, i: (b, i, 0))
    args = [p_d]
    specs = [chunk(N_D)]
    if has_vres:
        args.append(v_first)
        specs.append(chunk(GROUP))
    args += [row(mu), row(w0), _pad_rows(w_up, 0, lora).astype(BF16),
             row(a0), _pad_rows(a_up, LORA_W, lora).astype(BF16),
             _pad_rows(g_up, LORA_W + LORA_A, lora).astype(BF16),
             row(k_k), row(k_a), row(r_k), row(lnx_w), row(lnx_b), _block_diag_ones(GROUP, HEAD)]
    specs += [_full((1, N_D)), _full((1, GROUP)), _full((lora, GROUP)),
              _full((1, GROUP)), _full((lora, GROUP)), _full((lora, GROUP)),
              _full((1, GROUP)), _full((1, GROUP)), _full((1, GROUP)), _full((1, GROUP)),
              _full((1, GROUP)), _full((GROUP, GROUP))]
    if has_vres:
        v0, v_down, v_up = v_res
        rank = v_down.shape[1]
        vdn = jnp.zeros((GROUP, lora), F32).at[:, :rank].set(v_down).astype(BF16)
        args += [row(v0), vdn, _pad_rows(v_up, 0, lora).astype(BF16)]
        specs += [_full((1, GROUP)), _full((GROUP, lora)), _full((lora, GROUP))]
        out_shape = jax.ShapeDtypeStruct((bsz, seq, GROUP), BF16)
        out_specs = chunk(GROUP)
    else:
        out_shape = [jax.ShapeDtypeStruct((bsz, seq, GROUP), BF16),
                     jax.ShapeDtypeStruct((bsz, seq, GROUP), F32)]
        out_specs = [chunk(GROUP), chunk(GROUP)]
    out = pl.pallas_call(
        functools.partial(_rwkv_kernel, has_vres=has_vres),
        out_shape=out_shape,
        grid=(bsz // nb, seq // cs),
        in_specs=specs,
        out_specs=out_specs,
        scratch_shapes=[pltpu.VMEM((nb, GROUP, GROUP), F32), pltpu.VMEM((nb, HIST, N_D), F32)],
        compiler_params=pltpu.CompilerParams(dimension_semantics=("arbitrary", "arbitrary"),
                                             vmem_limit_bytes=VMEM_LIMIT),
        name="rwkv7_chunked",
    )(*args)
    if has_vres:
        return out, None
    return out[0], out[1]


def _out_ffn_kernel(x_ref, ya_ref, yb_ref, yc_ref, yd_ref, mod_ref, g_ref, wo_ref, wup_ref,
                    cw_ref, cb_ref, wdn_ref, o_ref, hist_ref):
    tm, d = x_ref.shape[1], x_ref.shape[2]
    d_ff = wdn_ref.shape[0]
    ti = pl.program_id(1)

    @pl.when(ti == 0)
    def _():
        hist_ref[...] = jnp.zeros_like(hist_ref)

    mod = mod_ref[0]
    gate1 = mod[:, 2 * d:3 * d]
    shift2, scale2, gate2 = mod[:, 3 * d:4 * d], mod[:, 4 * d:5 * d], mod[:, 5 * d:6 * d]
    y_cat = jnp.concatenate([ya_ref[0], yb_ref[0], yc_ref[0], yd_ref[0]], axis=1)
    mix = jnp.dot(y_cat, wo_ref[...], preferred_element_type=F32)
    x1 = x_ref[0] + gate1 * mix
    h = _modulated_norm(x1, g_ref[...], shift2, scale2).astype(BF16)

    def conv(u, lo):
        cw = cw_ref[:, lo:lo + FF_CHUNK]
        hist = hist_ref[:, lo:lo + FF_CHUNK]
        out = cw[FF_CONV - 1:FF_CONV] * u + cb_ref[:, lo:lo + FF_CHUNK]
        for dly in range(1, FF_CONV):
            out = out + cw[FF_CONV - 1 - dly:FF_CONV - dly] * _shift_rows(hist, u, dly)
        hist_ref[:, lo:lo + FF_CHUNK] = u[tm - HIST:, :]
        return out

    def up_proj(j):
        lo_g, lo_v = j * FF_CHUNK, d_ff + j * FF_CHUNK
        return (jnp.dot(h, wup_ref[:, lo_g:lo_g + FF_CHUNK], preferred_element_type=F32),
                jnp.dot(h, wup_ref[:, lo_v:lo_v + FF_CHUNK], preferred_element_type=F32))

    n_chunks = d_ff // FF_CHUNK
    acc = None
    ahead = up_proj(0)
    acts, first, pending = [], 0, None
    for j in range(n_chunks):
        raw_g, raw_v = ahead
        if j + 1 < n_chunks:
            ahead = up_proj(j + 1)
        if pending is not None:
            lo_p, act_group = pending
            part = jnp.dot(act_group, wdn_ref[lo_p:lo_p + act_group.shape[1], :],
                           preferred_element_type=F32)
            acc = part if acc is None else acc + part
            pending = None
        lo_g, lo_v = j * FF_CHUNK, d_ff + j * FF_CHUNK
        u_g = conv(raw_g, lo_g)
        u_v = conv(raw_v, lo_v)
        acts.append((u_g * _sigmoid(u_g) * u_v).astype(BF16))
        if len(acts) == FF_DOWN_GROUP or j + 1 == n_chunks:
            pending = (first * FF_CHUNK, jnp.concatenate(acts, axis=1))
            acts, first = [], j + 1
    lo_p, act_group = pending
    part = jnp.dot(act_group, wdn_ref[lo_p:lo_p + act_group.shape[1], :], preferred_element_type=F32)
    acc = part if acc is None else acc + part
    o_ref[0] = x1 + gate2 * acc


def _out_ffn(x, ys, mod, g2, w_out, w_up, conv_w, conv_b, w_down, layer=0):
    bsz, seq, d = x.shape
    w_out, w_up, w_down = _stacked(w_out), _stacked(w_up), _stacked(w_down)
    d_ff = w_down.shape[-2]
    tm = min(FFN_ROW_TILE, seq)
    tile = lambda n: pl.BlockSpec((1, tm, n), lambda b, i: (b, i, 0))
    return pl.pallas_call(
        _out_ffn_kernel,
        out_shape=jax.ShapeDtypeStruct((bsz, seq, d), F32),
        grid=(bsz, seq // tm),
        in_specs=[tile(d), tile(GROUP), tile(GROUP), tile(GROUP), tile(GROUP),
                  pl.BlockSpec((1, 1, mod.shape[-1]), lambda b, i: (b, 0, 0)),
                  _full((1, d)), _resident_layer((4 * GROUP, d), layer),
                  _resident_layer((d, 2 * d_ff), layer),
                  _full((FF_CONV, 2 * d_ff)), _full((1, 2 * d_ff)),
                  _resident_layer((d_ff, d), layer)],
        out_specs=tile(d),
        scratch_shapes=[pltpu.VMEM((HIST, 2 * d_ff), F32)],
        compiler_params=pltpu.CompilerParams(dimension_semantics=("arbitrary", "arbitrary"),
                                             vmem_limit_bytes=VMEM_LIMIT),
        name="out_proj_ffn",
    )(x, *ys, mod, g2.reshape(1, d), w_out, w_up, conv_w, conv_b.reshape(1, -1), w_down)


def kernel(x, c, w_ada, b_ada, norm1_g, norm2_g, w_in, w_out, a_qnorm_g, a_knorm_g, a_lam_q, a_lam_k, a_out_g, b_out_g, c_conv_w, c_conv_b, c_gate_a_w, c_gate_a_b, c_gate_x_w, c_gate_x_b, c_lambda, c_out_g, d_mu, d_w0, d_w_up, d_a0, d_a_up, d_g_up, d_k_k, d_k_a, d_r_k, d_lnx_w, d_lnx_b, d_v0, d_v_down, d_v_up, ff_w_up, ff_conv_w, ff_conv_b, ff_w_down):
    depth = w_in.shape[0]
    bsz = x.shape[0]
    mods = _ada_modulation(c, w_ada, b_ada)
    w_in_bf, w_out_bf = w_in.astype(BF16), w_out.astype(BF16)
    ff_w_up_bf, ff_w_down_bf = ff_w_up.astype(BF16), ff_w_down.astype(BF16)
    v_first = None
    for l in range(depth):
        mod = mods[l].reshape(bsz, 1, -1)
        p_a, p_b, p_c, p_d = _in_proj(x, mod, norm1_g[l], w_in_bf, l)
        lam_init = 0.8 - 0.6 * math.exp(-0.3 * l)
        y_a, y_b = _mixer_attention_pair(p_a, a_qnorm_g[l], a_knorm_g[l], a_lam_q[l], a_lam_k[l],
                                         a_out_g[l], lam_init, p_b, b_out_g[l])
        y_c = _mixer_rglru(p_c, c_conv_w[l], c_conv_b[l], c_gate_a_w[l], c_gate_a_b[l],
                           c_gate_x_w[l], c_gate_x_b[l], c_lambda[l], c_out_g[l])
        v_res = None if l == 0 else (d_v0[l - 1], d_v_down[l - 1], d_v_up[l - 1])
        y_d, v_d = _mixer_rwkv7(p_d, d_mu[l], d_w0[l], d_w_up[l], d_a0[l], d_a_up[l], d_g_up[l],
                                d_k_k[l], d_k_a[l], d_r_k[l].reshape(-1), d_lnx_w[l], d_lnx_b[l],
                                v_first, v_res)
        if l == 0:
            v_first = v_d
        x = _out_ffn(x, (y_a, y_b, y_c, y_d), mod, norm2_g[l], w_out_bf, ff_w_up_bf,
                     ff_conv_w[l], ff_conv_b[l], ff_w_down_bf, l)
    return x
```

```python
import functools
import math

import numpy as np
import jax
import jax.numpy as jnp
from jax import lax
from jax.experimental import pallas as pl
from jax.experimental.pallas import tpu as pltpu

F32 = jnp.float32
BF16 = jnp.bfloat16

GROUP = 256
A_HEADS = 4
A_QK = 32
HEAD = 64
N_A = 3 * GROUP
N_B = 3 * GROUP
N_C = 2 * GROUP
LORA_W = 32
LORA_A = 32
LORA_G = 64
N_D = 3 * GROUP + LORA_W + LORA_A + LORA_G
C_CONV = 4
C_EXP = 8.0
FF_CONV = 3
LOG2_E = 1.4426950408889634
RMS_EPS = 1e-6
LNX_EPS = 64e-5
ADA_CHUNKS = 6

ROW_TILE = 512
PROJ_PIECE = 256
LRU_PHASES = 5
FFN_ROW_TILE = 512
ATT_TILE = 256
LRU_TILE = 256
RWKV_CHUNK = 128
RWKV_BATCH = 4
FF_CHUNK = 256
FF_DOWN_GROUP = 4
SUM_ROWS = 16
INV_BASE = 8
HIST = 8
VMEM_LIMIT = 56 * 1024 * 1024


def _dot(a, b):
    return jnp.dot(a.astype(BF16), b.astype(BF16), preferred_element_type=F32)


def _dot_nt(a, b):
    return lax.dot_general(a.astype(BF16), b.astype(BF16), (((1,), (1,)), ((), ())),
                           preferred_element_type=F32)


def _split2(x):
    hi = x.astype(BF16)
    lo = (x - hi.astype(F32)).astype(BF16)
    return hi, lo


def _dot_split2(x, m):
    hi, lo = _split2(x)
    return (jnp.dot(hi, m, preferred_element_type=F32)
            + jnp.dot(lo, m, preferred_element_type=F32))


def _dot_split3_lhs_exact(m, x):
    hi = x.astype(BF16)
    r1 = x - hi.astype(F32)
    mid = r1.astype(BF16)
    lo = (r1 - mid.astype(F32)).astype(BF16)
    return (jnp.dot(m, hi, preferred_element_type=F32)
            + jnp.dot(m, mid, preferred_element_type=F32)
            + jnp.dot(m, lo, preferred_element_type=F32))


def _sigmoid(x):
    return 1.0 / (1.0 + jnp.exp(-x))


def _softplus(x):
    return jnp.maximum(x, 0.0) + jnp.log1p(jnp.exp(-jnp.abs(x)))


def _lane_mask(width, lo, hi):
    lane = lax.broadcasted_iota(jnp.int32, (1, width), 1)
    return (lane >= lo) & (lane < hi)


def _shift_rows(hist, x, d):
    ext = jnp.concatenate([hist, x], axis=0)
    return pltpu.roll(ext, d, axis=0)[HIST:, :]


def _full(shape):
    nd = len(shape)
    return pl.BlockSpec(shape, lambda *_: (0,) * nd)


def _resident(shape):
    nd = len(shape)
    return pl.BlockSpec(shape, lambda *_: (0,) * nd, pipeline_mode=pl.Buffered(1))


def _resident_layer(shape, layer):
    return pl.BlockSpec((None,) + tuple(shape), lambda *_: (layer, 0, 0),
                        pipeline_mode=pl.Buffered(1))


def _stacked(w):
    return w[None] if w.ndim == 2 else w


def _block_diag_ones(width, block):
    idx = np.arange(width) // block
    return jnp.asarray((idx[:, None] == idx[None, :]).astype(np.float32), dtype=BF16)


def _ada_kernel(c_ref, w_ref, b_ref, o_ref):
    c = c_ref[...]
    cond = c * _sigmoid(c)
    o_ref[0] = _dot(cond, w_ref[0]) + b_ref[0]


def _ada_modulation(c, w_ada, b_ada):
    depth, d, n = w_ada.shape
    bsz = c.shape[0]
    tn = 1536
    return pl.pallas_call(
        _ada_kernel,
        out_shape=jax.ShapeDtypeStruct((depth, bsz, n), F32),
        grid=(depth, n // tn),
        in_specs=[pl.BlockSpec((bsz, d), lambda l, j: (0, 0)),
                  pl.BlockSpec((1, d, tn), lambda l, j: (l, 0, j)),
                  pl.BlockSpec((1, 1, tn), lambda l, j: (l, 0, j))],
        out_specs=pl.BlockSpec((1, bsz, tn), lambda l, j: (l, 0, j)),
        compiler_params=pltpu.CompilerParams(dimension_semantics=("arbitrary", "arbitrary"),
                                             vmem_limit_bytes=VMEM_LIMIT),
        name="ada_modulation",
    )(c, w_ada, b_ada.reshape(depth, 1, n))


def _modulated_norm(x, g, shift, scale):
    ms = jnp.mean(x * x, axis=-1, keepdims=True)
    return (x * lax.rsqrt(ms + RMS_EPS) * g) * (1.0 + scale) + shift


def _in_proj_kernel(x_ref, mod_ref, g_ref, w_ref, pa_ref, pb_ref, pc_ref, pd_ref):
    d = x_ref.shape[-1]
    mod = mod_ref[0]
    h = _modulated_norm(x_ref[0], g_ref[...], mod[:, 0:d], mod[:, d:2 * d]).astype(BF16)
    pa_ref[0] = jnp.dot(h, w_ref[:, 0:N_A], preferred_element_type=F32)
    pb_ref[0] = jnp.dot(h, w_ref[:, N_A:N_A + N_B], preferred_element_type=F32)
    pc_ref[0] = jnp.dot(h, w_ref[:, N_A + N_B:N_A + N_B + N_C], preferred_element_type=F32)
    pd_ref[0] = jnp.dot(h, w_ref[:, N_A + N_B + N_C:], preferred_element_type=F32)


def _in_proj(x, mod, g, w_bf16, layer=0):
    bsz, seq, d = x.shape
    w_bf16 = _stacked(w_bf16)
    n_in = w_bf16.shape[-1]
    tm = min(ROW_TILE, seq)
    widths = (N_A, N_B, N_C, N_D)
    return pl.pallas_call(
        _in_proj_kernel,
        out_shape=[jax.ShapeDtypeStruct((bsz, seq, n), F32) for n in widths],
        grid=(bsz, seq // tm),
        in_specs=[pl.BlockSpec((1, tm, d), lambda b, i: (b, i, 0)),
                  pl.BlockSpec((1, 1, mod.shape[-1]), lambda b, i: (b, 0, 0)),
                  _full((1, d)),
                  _resident_layer((d, n_in), layer)],
        out_specs=[pl.BlockSpec((1, tm, n), lambda b, i: (b, i, 0)) for n in widths],
        compiler_params=pltpu.CompilerParams(dimension_semantics=("arbitrary", "arbitrary"),
                                             vmem_limit_bytes=VMEM_LIMIT),
        name="in_proj",
    )(x, mod, g.reshape(1, d), w_bf16)


def _group_rms(x, bd, group, gain):
    ms = _dot_split2(x * x, bd) * (1.0 / group)
    return x * lax.rsqrt(ms + RMS_EPS) * gain


def _attn_a_setup(p_ref, qg_ref, kg_ref, lq_ref, lk_ref, og_ref, bd32_ref,
                  o_ref, kn_ref, vt_ref, lam_init):
    tq = o_ref.shape[1]
    n_kt = vt_ref.shape[0]
    qi = pl.program_id(1)
    bd32 = bd32_ref[...]

    @pl.when(qi == 0)
    def _():
        k = p_ref[0, :, GROUP:2 * GROUP]
        kn_ref[...] = _group_rms(k, bd32, A_QK, kg_ref[...]).astype(BF16)
        ones_rows = jnp.ones((SUM_ROWS, tq), BF16)
        for j in range(n_kt):
            vt = jnp.transpose(p_ref[0, j * tq:(j + 1) * tq, 2 * GROUP:3 * GROUP])
            for h in range(A_HEADS):
                vt_ref[j, h, 0:HEAD, :] = vt[h * HEAD:(h + 1) * HEAD, :].astype(BF16)
                vt_ref[j, h, HEAD:, :] = ones_rows

    lq = lq_ref[...]
    lk = lk_ref[...]
    lam = (jnp.exp(jnp.sum(lq[0:1] * lk[0:1], axis=-1, keepdims=True))
           - jnp.exp(jnp.sum(lq[1:2] * lk[1:2], axis=-1, keepdims=True)) + lam_init)

    q0 = pl.multiple_of(qi * tq, tq)
    q = p_ref[0, pl.ds(q0, tq), 0:GROUP]
    qn = _group_rms(q, bd32, A_QK, qg_ref[...]) * (A_QK ** -0.5 * LOG2_E)

    key_idx = lax.broadcasted_iota(jnp.int32, (tq, tq), 0)
    query_idx = lax.broadcasted_iota(jnp.int32, (tq, tq), 1)
    causal = key_idx <= query_idx

    qt = jnp.transpose(qn)
    feat = lax.broadcasted_iota(jnp.int32, (GROUP, 1), 0)
    qms = [[jnp.where((feat >= h * HEAD + c * A_QK) & (feat < h * HEAD + (c + 1) * A_QK), qt, 0.0
                      ).astype(BF16) for h in range(A_HEADS)] for c in range(2)]

    def key_tile(kj, carry, diag):
        kb = kn_ref[pl.ds(pl.multiple_of(kj * tq, tq), tq), :]
        dotf = lambda x, y: jnp.dot(x, y, preferred_element_type=F32)
        new = []
        for c in range(2):
            ms, ls, accs = carry[c]
            ss = [dotf(kb, qm) for qm in qms[c]]
            yield
            if diag:
                ss = [jnp.where(causal, s, -jnp.inf) for s in ss]
            ms_new = [jnp.maximum(m, jnp.max(s, axis=0, keepdims=True)) for m, s in zip(ms, ss)]
            alphas = [jnp.exp2(m - m_new) for m, m_new in zip(ms, ms_new)]
            ps = [jnp.exp2(s - m_new).astype(BF16) for s, m_new in zip(ss, ms_new)]
            yield
            res = [dotf(vt_ref[kj, h], p) for h, p in enumerate(ps)]
            yield
            accs_new = [alpha * acc + r[0:HEAD, :] for alpha, acc, r in zip(alphas, accs, res)]
            ls_new = [alpha * l + r[HEAD:HEAD + 1, :] for alpha, l, r in zip(alphas, ls, res)]
            new.append((tuple(ms_new), tuple(ls_new), tuple(accs_new)))
        return tuple(new)

    init_c = (tuple(jnp.full((1, tq), -jnp.inf, F32) for _ in range(A_HEADS)),
              tuple(jnp.zeros((1, tq), F32) for _ in range(A_HEADS)),
              tuple(jnp.zeros((HEAD, tq), F32) for _ in range(A_HEADS)))
    def finish(carry):
        (_, l0, acc0), (_, l1, acc1) = carry
        og = og_ref[...]
        ys = []
        for h in range(A_HEADS):
            o = acc0[h] * (1.0 / l0[h]) - lam * (acc1[h] * (1.0 / l1[h]))
            ms = jnp.mean(o * o, axis=0, keepdims=True)
            ys.append(o * lax.rsqrt(ms + RMS_EPS) * og[h * HEAD:(h + 1) * HEAD, :])
        y = jnp.transpose(jnp.concatenate(ys, axis=0)) * (1.0 - lam_init)
        o_ref[0] = y.astype(o_ref.dtype)

    return (init_c, init_c), key_tile, finish


def _run_steps(*gens):
    results = [None] * len(gens)
    live = list(range(len(gens)))
    while live:
        for idx in list(live):
            try:
                next(gens[idx])
            except StopIteration as done:
                results[idx] = done.value
                live.remove(idx)
    return results


def _attn_a_kernel(*refs, lam_init):
    qi = pl.program_id(1)
    init, key_tile, finish = _attn_a_setup(*refs, lam_init)
    carry = lax.fori_loop(0, qi, lambda kj, carry: _run_steps(key_tile(kj, carry, False))[0], init)
    finish(_run_steps(key_tile(qi, carry, True))[0])


def _mixer_diff_attn(p_a, q_g, k_g, lam_q, lam_k, out_g, lam_init):
    bsz, seq, _ = p_a.shape
    tq = min(ATT_TILE, seq)
    reps = GROUP // A_QK
    return pl.pallas_call(
        functools.partial(_attn_a_kernel, lam_init=lam_init),
        out_shape=jax.ShapeDtypeStruct((bsz, seq, GROUP), BF16),
        grid=(bsz, seq // tq),
        in_specs=[pl.BlockSpec((1, seq, N_A), lambda b, i: (b, 0, 0)),
                  _full((1, GROUP)), _full((1, GROUP)),
                  _full((2, A_QK)), _full((2, A_QK)),
                  _full((GROUP, 1)),
                  _full((GROUP, GROUP))],
        out_specs=pl.BlockSpec((1, tq, GROUP), lambda b, i: (b, i, 0)),
        scratch_shapes=[pltpu.VMEM((seq, GROUP), BF16),
                        pltpu.VMEM((seq // tq, A_HEADS, HEAD + SUM_ROWS, tq), BF16)],
        compiler_params=pltpu.CompilerParams(dimension_semantics=("arbitrary", "arbitrary"),
                                             vmem_limit_bytes=VMEM_LIMIT),
        name="diff_attention",
    )(p_a, jnp.tile(q_g, reps).reshape(1, GROUP), jnp.tile(k_g, reps).reshape(1, GROUP),
      lam_q, lam_k, out_g.reshape(GROUP, 1), _block_diag_ones(GROUP, A_QK))


def _attn_b_setup(p_ref, og_ref, tri_ref, o_ref, kb_ref, vt_ref):
    tq = o_ref.shape[1]
    n_kt = vt_ref.shape[0]
    heads = GROUP // HEAD
    qi = pl.program_id(1)
    feat = lax.broadcasted_iota(jnp.int32, (GROUP, 1), 0)
    head_rows = [(feat >= h * HEAD) & (feat < (h + 1) * HEAD) for h in range(heads)]

    @pl.when(qi == 0)
    def _():
        kb_ref[...] = p_ref[0, :, GROUP:2 * GROUP].astype(BF16)
        for j in range(n_kt):
            vt = jnp.transpose(p_ref[0, j * tq:(j + 1) * tq, 2 * GROUP:3 * GROUP])
            vt_ref[j] = vt.astype(BF16)

    q0 = pl.multiple_of(qi * tq, tq)
    qt = jnp.transpose(p_ref[0, pl.ds(q0, tq), 0:GROUP] * (HEAD ** -0.5 * LOG2_E))
    qms = [jnp.where(hr, qt, 0.0).astype(BF16) for hr in head_rows]
    tri = tri_ref[...]
    key_idx = lax.broadcasted_iota(jnp.int32, (tq, tq), 0)
    query_idx = lax.broadcasted_iota(jnp.int32, (tq, tq), 1)
    strict = key_idx < query_idx
    dotf = lambda x, y: jnp.dot(x, y, preferred_element_type=F32)

    def key_tile(kj, carry, diag):
        laters, accs = carry
        kb = kb_ref[pl.ds(pl.multiple_of(kj * tq, tq), tq), :]
        zs = [dotf(kb, qm) for qm in qms]
        yield
        neg_keeps = [jnp.maximum(z, 0.0) + jnp.log2(1.0 + jnp.exp2(-jnp.abs(z))) for z in zs]
        if diag:
            neg_keeps = [jnp.where(strict, nk, 0.0) for nk in neg_keeps]
        splits = [jnp.concatenate(_split2(nk), axis=0) for nk in neg_keeps]
        yield
        incls = [dotf(tri, sp) for sp in splits]
        yield
        ws = [jnp.exp2(z + incl + later) for z, incl, later in zip(zs, incls, laters)]
        if diag:
            ws = [jnp.where(strict, w, 0.0) for w in ws]
        yield
        accs = tuple(acc + dotf(vt_ref[kj, h * HEAD:(h + 1) * HEAD, :], w.astype(BF16))
                     for h, (acc, w) in enumerate(zip(accs, ws)))
        return tuple(later + incl[0:1, :] for later, incl in zip(laters, incls)), accs

    init = (tuple(jnp.zeros((1, tq), F32) for _ in range(heads)),
            tuple(jnp.zeros((HEAD, tq), F32) for _ in range(heads)))

    def finish(carry):
        _, accs = carry
        og = og_ref[...]
        ys = []
        for h in range(heads):
            o = accs[h]
            ms = jnp.mean(o * o, axis=0, keepdims=True)
            ys.append(o * lax.rsqrt(ms + RMS_EPS) * og[h * HEAD:(h + 1) * HEAD, :])
        o_ref[0] = jnp.transpose(jnp.concatenate(ys, axis=0)).astype(o_ref.dtype)

    return init, key_tile, finish


def _attn_b_kernel(*refs):
    qi = pl.program_id(1)
    init, key_tile, finish = _attn_b_setup(*refs)
    carry = _run_steps(key_tile(qi, init, True))[0]
    finish(lax.fori_loop(
        0, qi, lambda i, carry: _run_steps(key_tile(qi - 1 - i, carry, False))[0], carry))


def _attn_ab_kernel(pa_ref, qg_ref, kg_ref, lq_ref, lk_ref, oga_ref, bd32_ref, pb_ref, ogb_ref, tri_ref,
                    oa_ref, ob_ref, kn_ref, vta_ref, kb_ref, vtb_ref, *, lam_init):
    qi = pl.program_id(1)
    a_init, a_tile, a_finish = _attn_a_setup(pa_ref, qg_ref, kg_ref, lq_ref, lk_ref, oga_ref, bd32_ref,
                                             oa_ref, kn_ref, vta_ref, lam_init)
    b_init, b_tile, b_finish = _attn_b_setup(pb_ref, ogb_ref, tri_ref, ob_ref, kb_ref, vtb_ref)
    both = lambda kj, c, diag: tuple(_run_steps(a_tile(kj, c[0], diag), b_tile(kj, c[1], diag)))
    carry = both(qi, (a_init, b_init), True)
    carry = lax.fori_loop(0, qi, lambda i, c: both(qi - 1 - i, c, False), carry)
    a_finish(carry[0])
    b_finish(carry[1])


def _mixer_stick_breaking(p_b, out_g):
    bsz, seq, _ = p_b.shape
    tq = min(ATT_TILE, seq)
    neg_upper = -np.triu(np.ones((tq, tq), np.float32))
    tri = jnp.asarray(np.concatenate([neg_upper, neg_upper], axis=1), dtype=BF16)
    return pl.pallas_call(
        _attn_b_kernel,
        out_shape=jax.ShapeDtypeStruct((bsz, seq, GROUP), BF16),
        grid=(bsz, seq // tq),
        in_specs=[pl.BlockSpec((1, seq, N_B), lambda b, i: (b, 0, 0)),
                  _full((GROUP, 1)), _full((tq, 2 * tq))],
        out_specs=pl.BlockSpec((1, tq, GROUP), lambda b, i: (b, i, 0)),
        scratch_shapes=[pltpu.VMEM((seq, GROUP), BF16),
                        pltpu.VMEM((seq // tq, GROUP, tq), BF16)],
        compiler_params=pltpu.CompilerParams(dimension_semantics=("arbitrary", "arbitrary"),
                                             vmem_limit_bytes=VMEM_LIMIT),
        name="stick_breaking_attention",
    )(p_b, out_g.reshape(GROUP, 1), tri)


def _mixer_attention_pair(p_a, q_g, k_g, lam_q, lam_k, a_out_g, lam_init, p_b, b_out_g):
    bsz, seq, _ = p_a.shape
    tq = min(ATT_TILE, seq)
    reps = GROUP // A_QK
    neg_upper = -np.triu(np.ones((tq, tq), np.float32))
    tri = jnp.asarray(np.concatenate([neg_upper, neg_upper], axis=1), dtype=BF16)
    whole = lambda n: pl.BlockSpec((1, seq, n), lambda b, i: (b, 0, 0))
    tile = pl.BlockSpec((1, tq, GROUP), lambda b, i: (b, i, 0))
    return pl.pallas_call(
        functools.partial(_attn_ab_kernel, lam_init=lam_init),
        out_shape=[jax.ShapeDtypeStruct((bsz, seq, GROUP), BF16)] * 2,
        grid=(bsz, seq // tq),
        in_specs=[whole(N_A), _full((1, GROUP)), _full((1, GROUP)),
                  _full((2, A_QK)), _full((2, A_QK)), _full((GROUP, 1)), _full((GROUP, GROUP)),
                  whole(N_B), _full((GROUP, 1)), _full((tq, 2 * tq))],
        out_specs=[tile, tile],
        scratch_shapes=[pltpu.VMEM((seq, GROUP), BF16),
                        pltpu.VMEM((seq // tq, A_HEADS, HEAD + SUM_ROWS, tq), BF16),
                        pltpu.VMEM((seq, GROUP), BF16),
                        pltpu.VMEM((seq // tq, GROUP, tq), BF16)],
        compiler_params=pltpu.CompilerParams(dimension_semantics=("arbitrary", "arbitrary"),
                                             vmem_limit_bytes=VMEM_LIMIT),
        name="attention_pair",
    )(p_a, jnp.tile(q_g, reps).reshape(1, GROUP), jnp.tile(k_g, reps).reshape(1, GROUP),
      lam_q, lam_k, a_out_g.reshape(GROUP, 1), _block_diag_ones(GROUP, A_QK),
      p_b, b_out_g.reshape(GROUP, 1), tri)


def _gelu_tanh(x):
    return 0.5 * x * (1.0 + jnp.tanh(math.sqrt(2.0 / math.pi) * (x + 0.044715 * (x * x * x))))


def _rglru_kernel(p_ref, cw_ref, cb_ref, gaw_ref, gab_ref, gxw_ref, gxb_ref, lam_ref, og_ref,
                  bd64_ref, o_ref, hist_ref, h_ref):
    y = _rglru_rows(p_ref[0], pl.program_id(1), cw_ref, cb_ref, gaw_ref, gab_ref, gxw_ref, gxb_ref,
                    lam_ref, og_ref, bd64_ref, hist_ref, h_ref)
    o_ref[0] = y.astype(o_ref.dtype)


def _rglru_rows(*args):
    return _run_steps(_rglru_steps(*args))[0]


def _rglru_steps(p, ti, cw_ref, cb_ref, gaw_ref, gab_ref, gxw_ref, gxb_ref, lam_ref, og_ref,
                 bd64_ref, hist_ref, h_ref):
    tt = p.shape[0]

    @pl.when(ti == 0)
    def _():
        hist_ref[...] = jnp.zeros_like(hist_ref)
        h_ref[...] = jnp.zeros_like(h_ref)

    x_raw = p[:, 0:GROUP]
    x_gate = p[:, GROUP:2 * GROUP]
    hist = hist_ref[...]
    cw = cw_ref[...]
    x = cw[C_CONV - 1:C_CONV] * x_raw + cb_ref[...]
    for d in range(1, C_CONV):
        x = x + cw[C_CONV - 1 - d:C_CONV - d] * _shift_rows(hist, x_raw, d)
    hist_ref[...] = x_raw[tt - HIST:, :]
    yield

    r = _sigmoid(_dot(x, gaw_ref[...]) + gab_ref[...])
    i = _sigmoid(_dot(x, gxw_ref[...]) + gxb_ref[...])
    log_a = (-C_EXP) * r * _softplus(-lam_ref[...])
    a = jnp.exp(log_a)
    mult = jnp.sqrt(jnp.tanh(-log_a) * (a * a + 1.0))
    row = lax.broadcasted_iota(jnp.int32, (tt, 1), 0)
    mult = jnp.where((row == 0) & (ti == 0), 1.0, mult)
    u = mult * i * x
    yield

    d = 1
    while d < HIST:
        a_prev = pltpu.roll(a, d, axis=0)
        u_prev = pltpu.roll(u, d, axis=0)
        keep = (row & (HIST - 1)) >= d
        u = jnp.where(keep, a * u_prev + u, u)
        a = jnp.where(keep, a * a_prev, a)
        d *= 2
    yield
    state = h_ref[0:1, :]
    blocks = []
    for k in range(tt // HIST):
        blk = u[k * HIST:(k + 1) * HIST, :] + a[k * HIST:(k + 1) * HIST, :] * state
        state = blk[HIST - 1:HIST, :]
        blocks.append(blk)
    h = jnp.concatenate(blocks, axis=0)
    h_ref[...] = jnp.broadcast_to(state, h_ref.shape)
    yield

    y = h * _gelu_tanh(x_gate)
    return _group_rms(y, bd64_ref[...], HEAD, og_ref[...])


def _block_diag_weight(w):
    nb, d, _ = w.shape
    eye = jnp.eye(nb, dtype=w.dtype)
    return (eye[:, None, :, None] * w[:, :, None, :]).reshape(nb * d, nb * d)


def _mixer_rglru(p_c, conv_w, conv_b, ga_w, ga_b, gx_w, gx_b, lam, out_g):
    bsz, seq, _ = p_c.shape
    tt = min(LRU_TILE, seq)
    row = lambda v: v.reshape(1, GROUP)
    return pl.pallas_call(
        _rglru_kernel,
        out_shape=jax.ShapeDtypeStruct((bsz, seq, GROUP), BF16),
        grid=(bsz, seq // tt),
        in_specs=[pl.BlockSpec((1, tt, N_C), lambda b, i: (b, i, 0)),
                  _full((C_CONV, GROUP)), _full((1, GROUP)),
                  _full((GROUP, GROUP)), _full((1, GROUP)),
                  _full((GROUP, GROUP)), _full((1, GROUP)),
                  _full((1, GROUP)), _full((1, GROUP)), _full((GROUP, GROUP))],
        out_specs=pl.BlockSpec((1, tt, GROUP), lambda b, i: (b, i, 0)),
        scratch_shapes=[pltpu.VMEM((HIST, GROUP), F32), pltpu.VMEM((HIST, GROUP), F32)],
        compiler_params=pltpu.CompilerParams(dimension_semantics=("arbitrary", "arbitrary"),
                                             vmem_limit_bytes=VMEM_LIMIT),
        name="rg_lru",
    )(p_c, conv_w, row(conv_b), _block_diag_weight(ga_w).astype(BF16), row(ga_b),
      _block_diag_weight(gx_w).astype(BF16), row(gx_b), row(lam), row(out_g),
      _block_diag_ones(GROUP, HEAD))


def _in_proj_lru_kernel(x_ref, mod_ref, g_ref, w_ref, cw_ref, cb_ref, gaw_ref, gab_ref, gxw_ref,
                        gxb_ref, lam_ref, og_ref, bd64_ref, pa_ref, pb_ref, yc_ref, pd_ref,
                        hist_ref, h_ref):
    d = x_ref.shape[-1]
    mod = mod_ref[0]
    h = _modulated_norm(x_ref[0], g_ref[...], mod[:, 0:d], mod[:, d:2 * d]).astype(BF16)
    p_c = jnp.dot(h, w_ref[:, N_A + N_B:N_A + N_B + N_C], preferred_element_type=F32)
    lru = _rglru_steps(p_c, pl.program_id(1), cw_ref, cb_ref, gaw_ref, gab_ref, gxw_ref, gxb_ref,
                       lam_ref, og_ref, bd64_ref, hist_ref, h_ref)
    pieces = [(ref, lo, col0 + lo, min(PROJ_PIECE, width - lo))
              for ref, col0, width in ((pa_ref, 0, N_A), (pb_ref, N_A, N_B),
                                       (pd_ref, N_A + N_B + N_C, N_D))
              for lo in range(0, width, PROJ_PIECE)]
    per_phase = -(-len(pieces) // LRU_PHASES)
    y = None
    for k, (ref, lo, col, n) in enumerate(pieces):
        ref[0, :, lo:lo + n] = jnp.dot(h, w_ref[:, col:col + n], preferred_element_type=F32)
        if (k + 1) % per_phase == 0 and y is None:
            try:
                next(lru)
            except StopIteration as done:
                y = done.value
    if y is None:
        y = _run_steps(lru)[0]
    yc_ref[0] = y.astype(yc_ref.dtype)


def _in_proj_lru(x, mod, g, w_bf16, layer, conv_w, conv_b, ga_w, ga_b, gx_w, gx_b, lam, out_g):
    bsz, seq, d = x.shape
    w_bf16 = _stacked(w_bf16)
    n_in = w_bf16.shape[-1]
    tm = min(ROW_TILE, seq)
    row = lambda v: v.reshape(1, GROUP)
    rows_f32 = lambda n: jax.ShapeDtypeStruct((bsz, seq, n), F32)
    tile = lambda n: pl.BlockSpec((1, tm, n), lambda b, i: (b, i, 0))
    return pl.pallas_call(
        _in_proj_lru_kernel,
        out_shape=[rows_f32(N_A), rows_f32(N_B), jax.ShapeDtypeStruct((bsz, seq, GROUP), BF16),
                   rows_f32(N_D)],
        grid=(bsz, seq // tm),
        in_specs=[tile(d),
                  pl.BlockSpec((1, 1, mod.shape[-1]), lambda b, i: (b, 0, 0)),
                  _full((1, d)),
                  _resident_layer((d, n_in), layer),
                  _full((C_CONV, GROUP)), _full((1, GROUP)),
                  _full((GROUP, GROUP)), _full((1, GROUP)),
                  _full((GROUP, GROUP)), _full((1, GROUP)),
                  _full((1, GROUP)), _full((1, GROUP)), _full((GROUP, GROUP))],
        out_specs=[tile(N_A), tile(N_B), tile(GROUP), tile(N_D)],
        scratch_shapes=[pltpu.VMEM((HIST, GROUP), F32), pltpu.VMEM((HIST, GROUP), F32)],
        compiler_params=pltpu.CompilerParams(dimension_semantics=("arbitrary", "arbitrary"),
                                             vmem_limit_bytes=VMEM_LIMIT),
        name="in_proj_rg_lru",
    )(x, mod, g.reshape(1, d), w_bf16, conv_w, row(conv_b),
      _block_diag_weight(ga_w).astype(BF16), row(ga_b),
      _block_diag_weight(gx_w).astype(BF16), row(gx_b), row(lam), row(out_g),
      _block_diag_ones(GROUP, HEAD))


def _seg_sum(x, bd):
    return jnp.dot(x.astype(BF16), bd, preferred_element_type=F32)


def _unit_lower_inverses(mats, row, col):
    n = mats[0].shape[0]

    def same_block(size):
        shift = size.bit_length() - 1
        return (row >> shift) == (col >> shift)

    base = same_block(INV_BASE)
    eye = (row == col).astype(F32)
    powers = [jnp.where(base, -a, 0.0) for a in mats]
    ts = [eye + p for p in powers]
    k = 1
    while 2 * k < INV_BASE:
        powers = [_dot(p, p) for p in powers]
        ts = [t + _dot(t, p) for t, p in zip(ts, powers)]
        k *= 2
    size = INV_BASE
    while size < n:
        sel = same_block(2 * size) & jnp.logical_not(same_block(size))
        halves = [_dot(t, jnp.where(sel, a, 0.0)) for t, a in zip(ts, mats)]
        ts = [t - _dot(half, t) for t, half in zip(ts, halves)]
        size *= 2
    return ts


def _rwkv_kernel(*refs, has_vres):
    if has_vres:
        (p_ref, vf_ref, mu_ref, w0_ref, wup_ref, a0_ref, aup_ref, gup_ref, kk_ref, ka_ref, rk_ref,
         lw_ref, lb_ref, bd64_ref, v0_ref, vdn_ref, vup_ref, y_ref, state_ref, hist_ref) = refs
    else:
        (p_ref, mu_ref, w0_ref, wup_ref, a0_ref, aup_ref, gup_ref, kk_ref, ka_ref, rk_ref,
         lw_ref, lb_ref, bd64_ref, y_ref, v_ref, state_ref, hist_ref) = refs
    nb, cs = y_ref.shape[0], y_ref.shape[1]
    heads = GROUP // HEAD
    ci = pl.program_id(1)

    @pl.when(ci == 0)
    def _():
        state_ref[...] = jnp.zeros_like(state_ref)
        hist_ref[...] = jnp.zeros_like(hist_ref)

    row = lax.broadcasted_iota(jnp.int32, (cs, cs), 0)
    col = lax.broadcasted_iota(jnp.int32, (cs, cs), 1)
    lower_incl = col <= row
    lower_strict = col < row
    masks = [_lane_mask(GROUP, h * HEAD, (h + 1) * HEAD) for h in range(heads)]
    batch = range(nb)
    rows = lambda x, i: x[i * cs:(i + 1) * cs, :]

    ps = [p_ref[i] for i in batch]
    shifted = [_shift_rows(hist_ref[i], ps[i], 1) for i in batch]
    for i in batch:
        hist_ref[i] = ps[i][cs - HIST:, :]
    p = jnp.concatenate(ps, axis=0)
    xs = p + (jnp.concatenate(shifted, axis=0) - p) * mu_ref[...]

    r = xs[:, 0:GROUP]
    k = xs[:, GROUP:2 * GROUP]
    v = xs[:, 2 * GROUP:3 * GROUP]
    lora = xs[:, 3 * GROUP:]
    w = -_softplus(-(w0_ref[...] + _dot(jnp.tanh(lora), wup_ref[...]))) - 0.5
    a = _sigmoid(a0_ref[...] + _dot(lora, aup_ref[...]))
    g = _dot(_sigmoid(lora), gup_ref[...])
    if has_vres:
        mix = _sigmoid(v0_ref[...] + _dot(_dot(v, vdn_ref[...]), vup_ref[...]))
        v = v + (jnp.concatenate([vf_ref[i] for i in batch], axis=0) - v) * mix
    else:
        for i in batch:
            v_ref[i] = rows(v, i)

    bd64 = bd64_ref[...]
    kk = k * kk_ref[...]
    kk = kk / jnp.maximum(jnp.sqrt(_seg_sum(kk * kk, bd64)), 1e-12)
    kmod = k * (1.0 + (a - 1.0) * ka_ref[...])
    log_decay = -jnp.exp(w)

    cum_wide = _dot_split3_lhs_exact(jnp.where(lower_incl, 1.0, 0.0).astype(BF16),
                                     jnp.concatenate([rows(log_decay, i) for i in batch], axis=1))
    cum = jnp.concatenate([cum_wide[:, i * GROUP:(i + 1) * GROUP] for i in batch], axis=0)
    g_inc = jnp.exp(cum)
    g_inv = jnp.exp(-cum)
    kap = kk * jnp.exp(cum - log_decay)
    rt = r * g_inc
    bet = kk * a * g_inv
    kt = kmod * g_inv

    lhs = [jnp.concatenate([rows(kap, i), rows(rt, i)], axis=0) for i in batch]
    rhs = [jnp.concatenate([rows(bet, i), rows(kt, i)], axis=0).astype(BF16) for i in batch]
    pairs = [(i, h) for i in batch for h in range(heads)]
    grams = [_dot_nt(jnp.where(masks[h], lhs[i], 0.0), rhs[i]) for i, h in pairs]
    a_b = [jnp.where(lower_strict, gm[0:cs, 0:cs], 0.0) for gm in grams]
    a_k = [jnp.where(lower_strict, gm[0:cs, cs:], 0.0).astype(BF16) for gm in grams]
    b_r = [jnp.where(lower_incl, gm[cs:, 0:cs], 0.0).astype(BF16) for gm in grams]
    k_r = [jnp.where(lower_incl, gm[cs:, cs:], 0.0).astype(BF16) for gm in grams]
    t_inv = [t.astype(BF16) for t in _unit_lower_inverses(a_b, row, col)]

    def stack_heads(x):
        return jnp.concatenate([jnp.where(m, x, 0.0) for m in masks], axis=0).astype(BF16)

    per_row = lambda mats, i: jnp.concatenate(mats[i * heads:(i + 1) * heads], axis=1)
    dotf = lambda x, y: jnp.dot(x, y, preferred_element_type=F32)
    states = [state_ref[i] for i in batch]
    states_bf = [s.astype(BF16) for s in states]
    v_stack = [stack_heads(rows(v, i)) for i in batch]
    rhs_u = [_dot(rows(kap, i), states_bf[i]) + dotf(per_row(a_k, i), v_stack[i]) for i in batch]
    u = [dotf(per_row(t_inv, i), stack_heads(rhs_u[i])) for i in batch]
    y = [_dot(rows(rt, i), states_bf[i]) + dotf(per_row(k_r, i), v_stack[i])
         - dotf(per_row(b_r, i), stack_heads(u[i])) for i in batch]

    upd = [_dot(jnp.transpose(rows(kt, i)), rows(v, i)) - _dot(jnp.transpose(rows(bet, i)), u[i])
           for i in batch]
    bd_mask = bd64 > 0
    for i in batch:
        g_end = jnp.transpose(rows(g_inc, i))[:, cs - 1:cs]
        state_ref[i] = g_end * (states[i] + jnp.where(bd_mask, upd[i], 0.0))

    y = jnp.concatenate(y, axis=0)
    mean = _seg_sum(y, bd64) * (1.0 / HEAD)
    yc = y - mean
    var = _seg_sum(yc * yc, bd64) * (1.0 / HEAD)
    yn = yc * lax.rsqrt(var + LNX_EPS) * lw_ref[...] + lb_ref[...]
    bonus = _seg_sum(r * kmod * rk_ref[...], bd64) * v
    out = ((yn + bonus) * g).astype(y_ref.dtype)
    for i in batch:
        y_ref[i] = rows(out, i)


def _pad_rows(w, start, total):
    out = jnp.zeros((total, w.shape[1]), w.dtype)
    return lax.dynamic_update_slice(out, w, (start, 0))


def _mixer_rwkv7(p_d, mu, w0, w_up, a0, a_up, g_up, k_k, k_a, r_k, lnx_w, lnx_b, v_first, v_res):
    bsz, seq, _ = p_d.shape
    cs = min(RWKV_CHUNK, seq)
    lora = LORA_W + LORA_A + LORA_G
    row = lambda v: v.reshape(1, -1)
    has_vres = v_res is not None
    nb = math.gcd(RWKV_BATCH, bsz)
    chunk = lambda n: pl.BlockSpec((nb, cs, n), lambda b, i: (b, i, 0))
    args = [p_d]
    specs = [chunk(N_D)]
    if has_vres:
        args.append(v_first)
        specs.append(chunk(GROUP))
    args += [row(mu), row(w0), _pad_rows(w_up, 0, lora).astype(BF16),
             row(a0), _pad_rows(a_up, LORA_W, lora).astype(BF16),
             _pad_rows(g_up, LORA_W + LORA_A, lora).astype(BF16),
             row(k_k), row(k_a), row(r_k), row(lnx_w), row(lnx_b), _block_diag_ones(GROUP, HEAD)]
    specs += [_full((1, N_D)), _full((1, GROUP)), _full((lora, GROUP)),
              _full((1, GROUP)), _full((lora, GROUP)), _full((lora, GROUP)),
              _full((1, GROUP)), _full((1, GROUP)), _full((1, GROUP)), _full((1, GROUP)),
              _full((1, GROUP)), _full((GROUP, GROUP))]
    if has_vres:
        v0, v_down, v_up = v_res
        rank = v_down.shape[1]
        vdn = jnp.zeros((GROUP, lora), F32).at[:, :rank].set(v_down).astype(BF16)
        args += [row(v0), vdn, _pad_rows(v_up, 0, lora).astype(BF16)]
        specs += [_full((1, GROUP)), _full((GROUP, lora)), _full((lora, GROUP))]
        out_shape = jax.ShapeDtypeStruct((bsz, seq, GROUP), BF16)
        out_specs = chunk(GROUP)
    else:
        out_shape = [jax.ShapeDtypeStruct((bsz, seq, GROUP), BF16),
                     jax.ShapeDtypeStruct((bsz, seq, GROUP), F32)]
        out_specs = [chunk(GROUP), chunk(GROUP)]
    out = pl.pallas_call(
        functools.partial(_rwkv_kernel, has_vres=has_vres),
        out_shape=out_shape,
        grid=(bsz // nb, seq // cs),
        in_specs=specs,
        out_specs=out_specs,
        scratch_shapes=[pltpu.VMEM((nb, GROUP, GROUP), F32), pltpu.VMEM((nb, HIST, N_D), F32)],
        compiler_params=pltpu.CompilerParams(dimension_semantics=("arbitrary", "arbitrary"),
                                             vmem_limit_bytes=VMEM_LIMIT),
        name="rwkv7_chunked",
    )(*args)
    if has_vres:
        return out, None
    return out[0], out[1]


def _out_ffn_kernel(x_ref, ya_ref, yb_ref, yc_ref, yd_ref, mod_ref, g_ref, wo_ref, wup_ref,
                    cw_ref, cb_ref, wdn_ref, o_ref, hist_ref):
    tm, d = x_ref.shape[1], x_ref.shape[2]
    d_ff = wdn_ref.shape[0]
    ti = pl.program_id(1)

    @pl.when(ti == 0)
    def _():
        hist_ref[...] = jnp.zeros_like(hist_ref)

    mod = mod_ref[0]
    gate1 = mod[:, 2 * d:3 * d]
    shift2, scale2, gate2 = mod[:, 3 * d:4 * d], mod[:, 4 * d:5 * d], mod[:, 5 * d:6 * d]
    y_cat = jnp.concatenate([ya_ref[0], yb_ref[0], yc_ref[0], yd_ref[0]], axis=1)
    mix = jnp.dot(y_cat, wo_ref[...], preferred_element_type=F32)
    x1 = x_ref[0] + gate1 * mix
    h = _modulated_norm(x1, g_ref[...], shift2, scale2).astype(BF16)

    def conv(u, lo):
        cw = cw_ref[:, lo:lo + FF_CHUNK]
        hist = hist_ref[:, lo:lo + FF_CHUNK]
        out = cw[FF_CONV - 1:FF_CONV] * u + cb_ref[:, lo:lo + FF_CHUNK]
        for dly in range(1, FF_CONV):
            out = out + cw[FF_CONV - 1 - dly:FF_CONV - dly] * _shift_rows(hist, u, dly)
        hist_ref[:, lo:lo + FF_CHUNK] = u[tm - HIST:, :]
        return out

    def up_proj(j):
        lo_g, lo_v = j * FF_CHUNK, d_ff + j * FF_CHUNK
        return (jnp.dot(h, wup_ref[:, lo_g:lo_g + FF_CHUNK], preferred_element_type=F32),
                jnp.dot(h, wup_ref[:, lo_v:lo_v + FF_CHUNK], preferred_element_type=F32))

    n_chunks = d_ff // FF_CHUNK
    acc = None
    ahead = up_proj(0)
    acts, first, pending = [], 0, None
    for j in range(n_chunks):
        raw_g, raw_v = ahead
        if j + 1 < n_chunks:
            ahead = up_proj(j + 1)
        if pending is not None:
            lo_p, act_group = pending
            part = jnp.dot(act_group, wdn_ref[lo_p:lo_p + act_group.shape[1], :],
                           preferred_element_type=F32)
            acc = part if acc is None else acc + part
            pending = None
        lo_g, lo_v = j * FF_CHUNK, d_ff + j * FF_CHUNK
        u_g = conv(raw_g, lo_g)
        u_v = conv(raw_v, lo_v)
        acts.append((u_g * _sigmoid(u_g) * u_v).astype(BF16))
        if len(acts) == FF_DOWN_GROUP or j + 1 == n_chunks:
            pending = (first * FF_CHUNK, jnp.concatenate(acts, axis=1))
            acts, first = [], j + 1
    lo_p, act_group = pending
    part = jnp.dot(act_group, wdn_ref[lo_p:lo_p + act_group.shape[1], :], preferred_element_type=F32)
    acc = part if acc is None else acc + part
    o_ref[0] = x1 + gate2 * acc


def _out_ffn(x, ys, mod, g2, w_out, w_up, conv_w, conv_b, w_down, layer=0):
    bsz, seq, d = x.shape
    w_out, w_up, w_down = _stacked(w_out), _stacked(w_up), _stacked(w_down)
    d_ff = w_down.shape[-2]
    tm = min(FFN_ROW_TILE, seq)
    tile = lambda n: pl.BlockSpec((1, tm, n), lambda b, i: (b, i, 0))
    return pl.pallas_call(
        _out_ffn_kernel,
        out_shape=jax.ShapeDtypeStruct((bsz, seq, d), F32),
        grid=(bsz, seq // tm),
        in_specs=[tile(d), tile(GROUP), tile(GROUP), tile(GROUP), tile(GROUP),
                  pl.BlockSpec((1, 1, mod.shape[-1]), lambda b, i: (b, 0, 0)),
                  _full((1, d)), _resident_layer((4 * GROUP, d), layer),
                  _resident_layer((d, 2 * d_ff), layer),
                  _full((FF_CONV, 2 * d_ff)), _full((1, 2 * d_ff)),
                  _resident_layer((d_ff, d), layer)],
        out_specs=tile(d),
        scratch_shapes=[pltpu.VMEM((HIST, 2 * d_ff), F32)],
        compiler_params=pltpu.CompilerParams(dimension_semantics=("arbitrary", "arbitrary"),
                                             vmem_limit_bytes=VMEM_LIMIT),
        name="out_proj_ffn",
    )(x, *ys, mod, g2.reshape(1, d), w_out, w_up, conv_w, conv_b.reshape(1, -1), w_down)


def kernel(x, c, w_ada, b_ada, norm1_g, norm2_g, w_in, w_out, a_qnorm_g, a_knorm_g, a_lam_q, a_lam_k, a_out_g, b_out_g, c_conv_w, c_conv_b, c_gate_a_w, c_gate_a_b, c_gate_x_w, c_gate_x_b, c_lambda, c_out_g, d_mu, d_w0, d_w_up, d_a0, d_a_up, d_g_up, d_k_k, d_k_a, d_r_k, d_lnx_w, d_lnx_b, d_v0, d_v_down, d_v_up, ff_w_up, ff_conv_w, ff_conv_b, ff_w_down):
    depth = w_in.shape[0]
    bsz = x.shape[0]
    mods = _ada_modulation(c, w_ada, b_ada)
    w_in_bf, w_out_bf = w_in.astype(BF16), w_out.astype(BF16)
    ff_w_up_bf, ff_w_down_bf = ff_w_up.astype(BF16), ff_w_down.astype(BF16)
    v_first = None
    for l in range(depth):
        mod = mods[l].reshape(bsz, 1, -1)
        p_a, p_b, y_c, p_d = _in_proj_lru(x, mod, norm1_g[l], w_in_bf, l, c_conv_w[l], c_conv_b[l],
                                          c_gate_a_w[l], c_gate_a_b[l], c_gate_x_w[l],
                                          c_gate_x_b[l], c_lambda[l], c_out_g[l])
        lam_init = 0.8 - 0.6 * math.exp(-0.3 * l)
        y_a, y_b = _mixer_attention_pair(p_a, a_qnorm_g[l], a_knorm_g[l], a_lam_q[l], a_lam_k[l],
                                         a_out_g[l], lam_init, p_b, b_out_g[l])
        v_res = None if l == 0 else (d_v0[l - 1], d_v_down[l - 1], d_v_up[l - 1])
        y_d, v_d = _mixer_rwkv7(p_d, d_mu[l], d_w0[l], d_w_up[l], d_a0[l], d_a_up[l], d_g_up[l],
                                d_k_k[l], d_k_a[l], d_r_k[l].reshape(-1), d_lnx_w[l], d_lnx_b[l],
                                v_first, v_res)
        if l == 0:
            v_first = v_d
        x = _out_ffn(x, (y_a, y_b, y_c, y_d), mod, norm2_g[l], w_out_bf, ff_w_up_bf,
                     ff_conv_w[l], ff_conv_b[l], ff_w_down_bf, l)
    return x
```

```python
import functools
import math

import numpy as np
import jax
import jax.numpy as jnp
from jax import lax
from jax.experimental import pallas as pl
from jax.experimental.pallas import tpu as pltpu

F32 = jnp.float32
BF16 = jnp.bfloat16

GROUP = 256
A_HEADS = 4
A_QK = 32
HEAD = 64
N_A = 3 * GROUP
N_B = 3 * GROUP
N_C = 2 * GROUP
LORA_W = 32
LORA_A = 32
LORA_G = 64
N_D = 3 * GROUP + LORA_W + LORA_A + LORA_G
C_CONV = 4
C_EXP = 8.0
FF_CONV = 3
LOG2_E = 1.4426950408889634
RMS_EPS = 1e-6
LNX_EPS = 64e-5
ADA_CHUNKS = 6

ROW_TILE = 512
PROJ_PIECE = 256
LRU_PHASES = 5
FFN_ROW_TILE = 512
ATT_TILE = 256
LRU_TILE = 256
RWKV_CHUNK = 128
RWKV_BATCH = 4
FF_CHUNK = 256
FF_DOWN_GROUP = 4
SUM_ROWS = 16
INV_BASE = 8
HIST = 8
VMEM_LIMIT = 56 * 1024 * 1024


def _dot(a, b):
    return jnp.dot(a.astype(BF16), b.astype(BF16), preferred_element_type=F32)


def _dot_nt(a, b):
    return lax.dot_general(a.astype(BF16), b.astype(BF16), (((1,), (1,)), ((), ())),
                           preferred_element_type=F32)


def _split2(x):
    hi = x.astype(BF16)
    lo = (x - hi.astype(F32)).astype(BF16)
    return hi, lo


def _dot_split2(x, m):
    hi, lo = _split2(x)
    return (jnp.dot(hi, m, preferred_element_type=F32)
            + jnp.dot(lo, m, preferred_element_type=F32))


def _dot_split3_lhs_exact(m, x):
    hi = x.astype(BF16)
    r1 = x - hi.astype(F32)
    mid = r1.astype(BF16)
    lo = (r1 - mid.astype(F32)).astype(BF16)
    return (jnp.dot(m, hi, preferred_element_type=F32)
            + jnp.dot(m, mid, preferred_element_type=F32)
            + jnp.dot(m, lo, preferred_element_type=F32))


def _sigmoid(x):
    return 1.0 / (1.0 + jnp.exp(-x))


def _softplus(x):
    return jnp.maximum(x, 0.0) + jnp.log1p(jnp.exp(-jnp.abs(x)))


def _lane_mask(width, lo, hi):
    lane = lax.broadcasted_iota(jnp.int32, (1, width), 1)
    return (lane >= lo) & (lane < hi)


def _shift_rows(hist, x, d):
    ext = jnp.concatenate([hist, x], axis=0)
    return pltpu.roll(ext, d, axis=0)[HIST:, :]


def _full(shape):
    nd = len(shape)
    return pl.BlockSpec(shape, lambda *_: (0,) * nd)


def _resident(shape):
    nd = len(shape)
    return pl.BlockSpec(shape, lambda *_: (0,) * nd, pipeline_mode=pl.Buffered(1))


def _resident_layer(shape, layer):
    return pl.BlockSpec((None,) + tuple(shape), lambda *_: (layer, 0, 0),
                        pipeline_mode=pl.Buffered(1))


def _stacked(w):
    return w[None] if w.ndim == 2 else w


def _block_diag_ones(width, block):
    idx = np.arange(width) // block
    return jnp.asarray((idx[:, None] == idx[None, :]).astype(np.float32), dtype=BF16)


def _ada_kernel(c_ref, w_ref, b_ref, o_ref):
    c = c_ref[...]
    cond = c * _sigmoid(c)
    o_ref[0] = _dot(cond, w_ref[0]) + b_ref[0]


def _ada_modulation(c, w_ada, b_ada):
    depth, d, n = w_ada.shape
    bsz = c.shape[0]
    tn = 1536
    return pl.pallas_call(
        _ada_kernel,
        out_shape=jax.ShapeDtypeStruct((depth, bsz, n), F32),
        grid=(depth, n // tn),
        in_specs=[pl.BlockSpec((bsz, d), lambda l, j: (0, 0)),
                  pl.BlockSpec((1, d, tn), lambda l, j: (l, 0, j)),
                  pl.BlockSpec((1, 1, tn), lambda l, j: (l, 0, j))],
        out_specs=pl.BlockSpec((1, bsz, tn), lambda l, j: (l, 0, j)),
        compiler_params=pltpu.CompilerParams(dimension_semantics=("arbitrary", "arbitrary"),
                                             vmem_limit_bytes=VMEM_LIMIT),
        name="ada_modulation",
    )(c, w_ada, b_ada.reshape(depth, 1, n))


def _modulated_norm(x, g, shift, scale):
    ms = jnp.mean(x * x, axis=-1, keepdims=True)
    return (x * lax.rsqrt(ms + RMS_EPS) * g) * (1.0 + scale) + shift


def _in_proj_kernel(x_ref, mod_ref, g_ref, w_ref, pa_ref, pb_ref, pc_ref, pd_ref):
    d = x_ref.shape[-1]
    mod = mod_ref[0]
    h = _modulated_norm(x_ref[0], g_ref[...], mod[:, 0:d], mod[:, d:2 * d]).astype(BF16)
    pa_ref[0] = jnp.dot(h, w_ref[:, 0:N_A], preferred_element_type=F32)
    pb_ref[0] = jnp.dot(h, w_ref[:, N_A:N_A + N_B], preferred_element_type=F32)
    pc_ref[0] = jnp.dot(h, w_ref[:, N_A + N_B:N_A + N_B + N_C], preferred_element_type=F32)
    pd_ref[0] = jnp.dot(h, w_ref[:, N_A + N_B + N_C:], preferred_element_type=F32)


def _in_proj(x, mod, g, w_bf16, layer=0):
    bsz, seq, d = x.shape
    w_bf16 = _stacked(w_bf16)
    n_in = w_bf16.shape[-1]
    tm = min(ROW_TILE, seq)
    widths = (N_A, N_B, N_C, N_D)
    return pl.pallas_call(
        _in_proj_kernel,
        out_shape=[jax.ShapeDtypeStruct((bsz, seq, n), F32) for n in widths],
        grid=(bsz, seq // tm),
        in_specs=[pl.BlockSpec((1, tm, d), lambda b, i: (b, i, 0)),
                  pl.BlockSpec((1, 1, mod.shape[-1]), lambda b, i: (b, 0, 0)),
                  _full((1, d)),
                  _resident_layer((d, n_in), layer)],
        out_specs=[pl.BlockSpec((1, tm, n), lambda b, i: (b, i, 0)) for n in widths],
        compiler_params=pltpu.CompilerParams(dimension_semantics=("arbitrary", "arbitrary"),
                                             vmem_limit_bytes=VMEM_LIMIT),
        name="in_proj",
    )(x, mod, g.reshape(1, d), w_bf16)


def _group_rms(x, bd, group, gain):
    ms = _dot_split2(x * x, bd) * (1.0 / group)
    return x * lax.rsqrt(ms + RMS_EPS) * gain


def _attn_a_setup(p_ref, qg_ref, kg_ref, lq_ref, lk_ref, og_ref, bd32_ref,
                  o_ref, kn_ref, vt_ref, lam_init):
    tq = o_ref.shape[1]
    n_kt = vt_ref.shape[0]
    qi = pl.program_id(1)
    bd32 = bd32_ref[...]

    @pl.when(qi == 0)
    def _():
        k = p_ref[0, :, GROUP:2 * GROUP]
        kn_ref[...] = _group_rms(k, bd32, A_QK, kg_ref[...]).astype(BF16)
        ones_rows = jnp.ones((SUM_ROWS, tq), BF16)
        for j in range(n_kt):
            vt = jnp.transpose(p_ref[0, j * tq:(j + 1) * tq, 2 * GROUP:3 * GROUP])
            for h in range(A_HEADS):
                vt_ref[j, h, 0:HEAD, :] = vt[h * HEAD:(h + 1) * HEAD, :].astype(BF16)
                vt_ref[j, h, HEAD:, :] = ones_rows

    lq = lq_ref[...]
    lk = lk_ref[...]
    lam = (jnp.exp(jnp.sum(lq[0:1] * lk[0:1], axis=-1, keepdims=True))
           - jnp.exp(jnp.sum(lq[1:2] * lk[1:2], axis=-1, keepdims=True)) + lam_init)

    q0 = pl.multiple_of(qi * tq, tq)
    q = p_ref[0, pl.ds(q0, tq), 0:GROUP]
    qn = _group_rms(q, bd32, A_QK, qg_ref[...]) * (A_QK ** -0.5 * LOG2_E)

    key_idx = lax.broadcasted_iota(jnp.int32, (tq, tq), 0)
    query_idx = lax.broadcasted_iota(jnp.int32, (tq, tq), 1)
    causal = key_idx <= query_idx

    qt = jnp.transpose(qn)
    feat = lax.broadcasted_iota(jnp.int32, (GROUP, 1), 0)
    qms = [[jnp.where((feat >= h * HEAD + c * A_QK) & (feat < h * HEAD + (c + 1) * A_QK), qt, 0.0
                      ).astype(BF16) for h in range(A_HEADS)] for c in range(2)]

    def key_tile(kj, carry, diag):
        kb = kn_ref[pl.ds(pl.multiple_of(kj * tq, tq), tq), :]
        dotf = lambda x, y: jnp.dot(x, y, preferred_element_type=F32)
        chains = [(c, h) for c in range(2) for h in range(A_HEADS)]
        ms = [carry[c][0][h] for c, h in chains]
        ls = [carry[c][1][h] for c, h in chains]
        accs = [carry[c][2][h] for c, h in chains]
        ss = [dotf(kb, qms[c][h]) for c, h in chains]
        yield
        if diag:
            ss = [jnp.where(causal, s, -jnp.inf) for s in ss]
        ms_new = [jnp.maximum(m, jnp.max(s, axis=0, keepdims=True)) for m, s in zip(ms, ss)]
        alphas = [jnp.exp2(m - m_new) for m, m_new in zip(ms, ms_new)]
        ps = [jnp.exp2(s - m_new).astype(BF16) for s, m_new in zip(ss, ms_new)]
        yield
        res = [dotf(vt_ref[kj, h], p) for (c, h), p in zip(chains, ps)]
        yield
        accs_new = [alpha * acc + r[0:HEAD, :] for alpha, acc, r in zip(alphas, accs, res)]
        ls_new = [alpha * l + r[HEAD:HEAD + 1, :] for alpha, l, r in zip(alphas, ls, res)]
        n = A_HEADS
        return tuple((tuple(ms_new[c * n:(c + 1) * n]), tuple(ls_new[c * n:(c + 1) * n]),
                      tuple(accs_new[c * n:(c + 1) * n])) for c in range(2))

    init_c = (tuple(jnp.full((1, tq), -jnp.inf, F32) for _ in range(A_HEADS)),
              tuple(jnp.zeros((1, tq), F32) for _ in range(A_HEADS)),
              tuple(jnp.zeros((HEAD, tq), F32) for _ in range(A_HEADS)))
    def finish(carry):
        (_, l0, acc0), (_, l1, acc1) = carry
        og = og_ref[...]
        ys = []
        for h in range(A_HEADS):
            o = acc0[h] * (1.0 / l0[h]) - lam * (acc1[h] * (1.0 / l1[h]))
            ms = jnp.mean(o * o, axis=0, keepdims=True)
            ys.append(o * lax.rsqrt(ms + RMS_EPS) * og[h * HEAD:(h + 1) * HEAD, :])
        y = jnp.transpose(jnp.concatenate(ys, axis=0)) * (1.0 - lam_init)
        o_ref[0] = y.astype(o_ref.dtype)

    return (init_c, init_c), key_tile, finish


def _run_steps(*gens):
    results = [None] * len(gens)
    live = list(range(len(gens)))
    while live:
        for idx in list(live):
            try:
                next(gens[idx])
            except StopIteration as done:
                results[idx] = done.value
                live.remove(idx)
    return results


def _attn_a_kernel(*refs, lam_init):
    qi = pl.program_id(1)
    init, key_tile, finish = _attn_a_setup(*refs, lam_init)
    carry = lax.fori_loop(0, qi, lambda kj, carry: _run_steps(key_tile(kj, carry, False))[0], init)
    finish(_run_steps(key_tile(qi, carry, True))[0])


def _mixer_diff_attn(p_a, q_g, k_g, lam_q, lam_k, out_g, lam_init):
    bsz, seq, _ = p_a.shape
    tq = min(ATT_TILE, seq)
    reps = GROUP // A_QK
    return pl.pallas_call(
        functools.partial(_attn_a_kernel, lam_init=lam_init),
        out_shape=jax.ShapeDtypeStruct((bsz, seq, GROUP), BF16),
        grid=(bsz, seq // tq),
        in_specs=[pl.BlockSpec((1, seq, N_A), lambda b, i: (b, 0, 0)),
                  _full((1, GROUP)), _full((1, GROUP)),
                  _full((2, A_QK)), _full((2, A_QK)),
                  _full((GROUP, 1)),
                  _full((GROUP, GROUP))],
        out_specs=pl.BlockSpec((1, tq, GROUP), lambda b, i: (b, i, 0)),
        scratch_shapes=[pltpu.VMEM((seq, GROUP), BF16),
                        pltpu.VMEM((seq // tq, A_HEADS, HEAD + SUM_ROWS, tq), BF16)],
        compiler_params=pltpu.CompilerParams(dimension_semantics=("arbitrary", "arbitrary"),
                                             vmem_limit_bytes=VMEM_LIMIT),
        name="diff_attention",
    )(p_a, jnp.tile(q_g, reps).reshape(1, GROUP), jnp.tile(k_g, reps).reshape(1, GROUP),
      lam_q, lam_k, out_g.reshape(GROUP, 1), _block_diag_ones(GROUP, A_QK))


def _attn_b_setup(p_ref, og_ref, tri_ref, o_ref, kb_ref, vt_ref):
    tq = o_ref.shape[1]
    n_kt = vt_ref.shape[0]
    heads = GROUP // HEAD
    qi = pl.program_id(1)
    feat = lax.broadcasted_iota(jnp.int32, (GROUP, 1), 0)
    head_rows = [(feat >= h * HEAD) & (feat < (h + 1) * HEAD) for h in range(heads)]

    @pl.when(qi == 0)
    def _():
        kb_ref[...] = p_ref[0, :, GROUP:2 * GROUP].astype(BF16)
        for j in range(n_kt):
            vt = jnp.transpose(p_ref[0, j * tq:(j + 1) * tq, 2 * GROUP:3 * GROUP])
            vt_ref[j] = vt.astype(BF16)

    q0 = pl.multiple_of(qi * tq, tq)
    qt = jnp.transpose(p_ref[0, pl.ds(q0, tq), 0:GROUP] * (HEAD ** -0.5 * LOG2_E))
    qms = [jnp.where(hr, qt, 0.0).astype(BF16) for hr in head_rows]
    tri = tri_ref[...]
    key_idx = lax.broadcasted_iota(jnp.int32, (tq, tq), 0)
    query_idx = lax.broadcasted_iota(jnp.int32, (tq, tq), 1)
    strict = key_idx < query_idx
    dotf = lambda x, y: jnp.dot(x, y, preferred_element_type=F32)

    def key_tile(kj, carry, diag):
        laters, accs = carry
        kb = kb_ref[pl.ds(pl.multiple_of(kj * tq, tq), tq), :]
        zs = [dotf(kb, qm) for qm in qms]
        yield
        neg_keeps = [jnp.maximum(z, 0.0) + jnp.log2(1.0 + jnp.exp2(-jnp.abs(z))) for z in zs]
        if diag:
            neg_keeps = [jnp.where(strict, nk, 0.0) for nk in neg_keeps]
        splits = [jnp.concatenate(_split2(nk), axis=0) for nk in neg_keeps]
        yield
        incls = [dotf(tri, sp) for sp in splits]
        yield
        ws = [jnp.exp2(z + incl + later) for z, incl, later in zip(zs, incls, laters)]
        if diag:
            ws = [jnp.where(strict, w, 0.0) for w in ws]
        yield
        accs = tuple(acc + dotf(vt_ref[kj, h * HEAD:(h + 1) * HEAD, :], w.astype(BF16))
                     for h, (acc, w) in enumerate(zip(accs, ws)))
        return tuple(later + incl[0:1, :] for later, incl in zip(laters, incls)), accs

    init = (tuple(jnp.zeros((1, tq), F32) for _ in range(heads)),
            tuple(jnp.zeros((HEAD, tq), F32) for _ in range(heads)))

    def finish(carry):
        _, accs = carry
        og = og_ref[...]
        ys = []
        for h in range(heads):
            o = accs[h]
            ms = jnp.mean(o * o, axis=0, keepdims=True)
            ys.append(o * lax.rsqrt(ms + RMS_EPS) * og[h * HEAD:(h + 1) * HEAD, :])
        o_ref[0] = jnp.transpose(jnp.concatenate(ys, axis=0)).astype(o_ref.dtype)

    return init, key_tile, finish


def _attn_b_kernel(*refs):
    qi = pl.program_id(1)
    init, key_tile, finish = _attn_b_setup(*refs)
    carry = _run_steps(key_tile(qi, init, True))[0]
    finish(lax.fori_loop(
        0, qi, lambda i, carry: _run_steps(key_tile(qi - 1 - i, carry, False))[0], carry))


def _attn_ab_kernel(pa_ref, qg_ref, kg_ref, lq_ref, lk_ref, oga_ref, bd32_ref, pb_ref, ogb_ref, tri_ref,
                    oa_ref, ob_ref, kn_ref, vta_ref, kb_ref, vtb_ref, *, lam_init):
    qi = pl.program_id(1)
    a_init, a_tile, a_finish = _attn_a_setup(pa_ref, qg_ref, kg_ref, lq_ref, lk_ref, oga_ref, bd32_ref,
                                             oa_ref, kn_ref, vta_ref, lam_init)
    b_init, b_tile, b_finish = _attn_b_setup(pb_ref, ogb_ref, tri_ref, ob_ref, kb_ref, vtb_ref)
    both = lambda kj, c, diag: tuple(_run_steps(a_tile(kj, c[0], diag), b_tile(kj, c[1], diag)))
    carry = both(qi, (a_init, b_init), True)
    carry = lax.fori_loop(0, qi, lambda i, c: both(qi - 1 - i, c, False), carry)
    a_finish(carry[0])
    b_finish(carry[1])


def _mixer_stick_breaking(p_b, out_g):
    bsz, seq, _ = p_b.shape
    tq = min(ATT_TILE, seq)
    neg_upper = -np.triu(np.ones((tq, tq), np.float32))
    tri = jnp.asarray(np.concatenate([neg_upper, neg_upper], axis=1), dtype=BF16)
    return pl.pallas_call(
        _attn_b_kernel,
        out_shape=jax.ShapeDtypeStruct((bsz, seq, GROUP), BF16),
        grid=(bsz, seq // tq),
        in_specs=[pl.BlockSpec((1, seq, N_B), lambda b, i: (b, 0, 0)),
                  _full((GROUP, 1)), _full((tq, 2 * tq))],
        out_specs=pl.BlockSpec((1, tq, GROUP), lambda b, i: (b, i, 0)),
        scratch_shapes=[pltpu.VMEM((seq, GROUP), BF16),
                        pltpu.VMEM((seq // tq, GROUP, tq), BF16)],
        compiler_params=pltpu.CompilerParams(dimension_semantics=("arbitrary", "arbitrary"),
                                             vmem_limit_bytes=VMEM_LIMIT),
        name="stick_breaking_attention",
    )(p_b, out_g.reshape(GROUP, 1), tri)


def _mixer_attention_pair(p_a, q_g, k_g, lam_q, lam_k, a_out_g, lam_init, p_b, b_out_g):
    bsz, seq, _ = p_a.shape
    tq = min(ATT_TILE, seq)
    reps = GROUP // A_QK
    neg_upper = -np.triu(np.ones((tq, tq), np.float32))
    tri = jnp.asarray(np.concatenate([neg_upper, neg_upper], axis=1), dtype=BF16)
    whole = lambda n: pl.BlockSpec((1, seq, n), lambda b, i: (b, 0, 0))
    tile = pl.BlockSpec((1, tq, GROUP), lambda b, i: (b, i, 0))
    return pl.pallas_call(
        functools.partial(_attn_ab_kernel, lam_init=lam_init),
        out_shape=[jax.ShapeDtypeStruct((bsz, seq, GROUP), BF16)] * 2,
        grid=(bsz, seq // tq),
        in_specs=[whole(N_A), _full((1, GROUP)), _full((1, GROUP)),
                  _full((2, A_QK)), _full((2, A_QK)), _full((GROUP, 1)), _full((GROUP, GROUP)),
                  whole(N_B), _full((GROUP, 1)), _full((tq, 2 * tq))],
        out_specs=[tile, tile],
        scratch_shapes=[pltpu.VMEM((seq, GROUP), BF16),
                        pltpu.VMEM((seq // tq, A_HEADS, HEAD + SUM_ROWS, tq), BF16),
                        pltpu.VMEM((seq, GROUP), BF16),
                        pltpu.VMEM((seq // tq, GROUP, tq), BF16)],
        compiler_params=pltpu.CompilerParams(dimension_semantics=("arbitrary", "arbitrary"),
                                             vmem_limit_bytes=VMEM_LIMIT),
        name="attention_pair",
    )(p_a, jnp.tile(q_g, reps).reshape(1, GROUP), jnp.tile(k_g, reps).reshape(1, GROUP),
      lam_q, lam_k, a_out_g.reshape(GROUP, 1), _block_diag_ones(GROUP, A_QK),
      p_b, b_out_g.reshape(GROUP, 1), tri)


def _gelu_tanh(x):
    return 0.5 * x * (1.0 + jnp.tanh(math.sqrt(2.0 / math.pi) * (x + 0.044715 * (x * x * x))))


def _rglru_kernel(p_ref, cw_ref, cb_ref, gaw_ref, gab_ref, gxw_ref, gxb_ref, lam_ref, og_ref,
                  bd64_ref, o_ref, hist_ref, h_ref):
    y = _rglru_rows(p_ref[0], pl.program_id(1), cw_ref, cb_ref, gaw_ref, gab_ref, gxw_ref, gxb_ref,
                    lam_ref, og_ref, bd64_ref, hist_ref, h_ref)
    o_ref[0] = y.astype(o_ref.dtype)


def _rglru_rows(*args):
    return _run_steps(_rglru_steps(*args))[0]


def _rglru_steps(p, ti, cw_ref, cb_ref, gaw_ref, gab_ref, gxw_ref, gxb_ref, lam_ref, og_ref,
                 bd64_ref, hist_ref, h_ref):
    tt = p.shape[0]

    @pl.when(ti == 0)
    def _():
        hist_ref[...] = jnp.zeros_like(hist_ref)
        h_ref[...] = jnp.zeros_like(h_ref)

    x_raw = p[:, 0:GROUP]
    x_gate = p[:, GROUP:2 * GROUP]
    hist = hist_ref[...]
    cw = cw_ref[...]
    x = cw[C_CONV - 1:C_CONV] * x_raw + cb_ref[...]
    for d in range(1, C_CONV):
        x = x + cw[C_CONV - 1 - d:C_CONV - d] * _shift_rows(hist, x_raw, d)
    hist_ref[...] = x_raw[tt - HIST:, :]
    yield

    r = _sigmoid(_dot(x, gaw_ref[...]) + gab_ref[...])
    i = _sigmoid(_dot(x, gxw_ref[...]) + gxb_ref[...])
    log_a = (-C_EXP) * r * _softplus(-lam_ref[...])
    a = jnp.exp(log_a)
    mult = jnp.sqrt(jnp.tanh(-log_a) * (a * a + 1.0))
    row = lax.broadcasted_iota(jnp.int32, (tt, 1), 0)
    mult = jnp.where((row == 0) & (ti == 0), 1.0, mult)
    u = mult * i * x
    yield

    d = 1
    while d < HIST:
        a_prev = pltpu.roll(a, d, axis=0)
        u_prev = pltpu.roll(u, d, axis=0)
        keep = (row & (HIST - 1)) >= d
        u = jnp.where(keep, a * u_prev + u, u)
        a = jnp.where(keep, a * a_prev, a)
        d *= 2
    yield
    state = h_ref[0:1, :]
    blocks = []
    for k in range(tt // HIST):
        blk = u[k * HIST:(k + 1) * HIST, :] + a[k * HIST:(k + 1) * HIST, :] * state
        state = blk[HIST - 1:HIST, :]
        blocks.append(blk)
    h = jnp.concatenate(blocks, axis=0)
    h_ref[...] = jnp.broadcast_to(state, h_ref.shape)
    yield

    y = h * _gelu_tanh(x_gate)
    return _group_rms(y, bd64_ref[...], HEAD, og_ref[...])


def _block_diag_weight(w):
    nb, d, _ = w.shape
    eye = jnp.eye(nb, dtype=w.dtype)
    return (eye[:, None, :, None] * w[:, :, None, :]).reshape(nb * d, nb * d)


def _mixer_rglru(p_c, conv_w, conv_b, ga_w, ga_b, gx_w, gx_b, lam, out_g):
    bsz, seq, _ = p_c.shape
    tt = min(LRU_TILE, seq)
    row = lambda v: v.reshape(1, GROUP)
    return pl.pallas_call(
        _rglru_kernel,
        out_shape=jax.ShapeDtypeStruct((bsz, seq, GROUP), BF16),
        grid=(bsz, seq // tt),
        in_specs=[pl.BlockSpec((1, tt, N_C), lambda b, i: (b, i, 0)),
                  _full((C_CONV, GROUP)), _full((1, GROUP)),
                  _full((GROUP, GROUP)), _full((1, GROUP)),
                  _full((GROUP, GROUP)), _full((1, GROUP)),
                  _full((1, GROUP)), _full((1, GROUP)), _full((GROUP, GROUP))],
        out_specs=pl.BlockSpec((1, tt, GROUP), lambda b, i: (b, i, 0)),
        scratch_shapes=[pltpu.VMEM((HIST, GROUP), F32), pltpu.VMEM((HIST, GROUP), F32)],
        compiler_params=pltpu.CompilerParams(dimension_semantics=("arbitrary", "arbitrary"),
                                             vmem_limit_bytes=VMEM_LIMIT),
        name="rg_lru",
    )(p_c, conv_w, row(conv_b), _block_diag_weight(ga_w).astype(BF16), row(ga_b),
      _block_diag_weight(gx_w).astype(BF16), row(gx_b), row(lam), row(out_g),
      _block_diag_ones(GROUP, HEAD))


def _in_proj_lru_kernel(x_ref, mod_ref, g_ref, w_ref, cw_ref, cb_ref, gaw_ref, gab_ref, gxw_ref,
                        gxb_ref, lam_ref, og_ref, bd64_ref, pa_ref, pb_ref, yc_ref, pd_ref,
                        hist_ref, h_ref):
    d = x_ref.shape[-1]
    mod = mod_ref[0]
    h = _modulated_norm(x_ref[0], g_ref[...], mod[:, 0:d], mod[:, d:2 * d]).astype(BF16)
    p_c = jnp.dot(h, w_ref[:, N_A + N_B:N_A + N_B + N_C], preferred_element_type=F32)
    lru = _rglru_steps(p_c, pl.program_id(1), cw_ref, cb_ref, gaw_ref, gab_ref, gxw_ref, gxb_ref,
                       lam_ref, og_ref, bd64_ref, hist_ref, h_ref)
    pieces = [(ref, lo, col0 + lo, min(PROJ_PIECE, width - lo))
              for ref, col0, width in ((pa_ref, 0, N_A), (pb_ref, N_A, N_B),
                                       (pd_ref, N_A + N_B + N_C, N_D))
              for lo in range(0, width, PROJ_PIECE)]
    per_phase = -(-len(pieces) // LRU_PHASES)
    y = None
    for k, (ref, lo, col, n) in enumerate(pieces):
        ref[0, :, lo:lo + n] = jnp.dot(h, w_ref[:, col:col + n], preferred_element_type=F32)
        if (k + 1) % per_phase == 0 and y is None:
            try:
                next(lru)
            except StopIteration as done:
                y = done.value
    if y is None:
        y = _run_steps(lru)[0]
    yc_ref[0] = y.astype(yc_ref.dtype)


def _in_proj_lru(x, mod, g, w_bf16, layer, conv_w, conv_b, ga_w, ga_b, gx_w, gx_b, lam, out_g):
    bsz, seq, d = x.shape
    w_bf16 = _stacked(w_bf16)
    n_in = w_bf16.shape[-1]
    tm = min(ROW_TILE, seq)
    row = lambda v: v.reshape(1, GROUP)
    rows_f32 = lambda n: jax.ShapeDtypeStruct((bsz, seq, n), F32)
    tile = lambda n: pl.BlockSpec((1, tm, n), lambda b, i: (b, i, 0))
    return pl.pallas_call(
        _in_proj_lru_kernel,
        out_shape=[rows_f32(N_A), rows_f32(N_B), jax.ShapeDtypeStruct((bsz, seq, GROUP), BF16),
                   rows_f32(N_D)],
        grid=(bsz, seq // tm),
        in_specs=[tile(d),
                  pl.BlockSpec((1, 1, mod.shape[-1]), lambda b, i: (b, 0, 0)),
                  _full((1, d)),
                  _resident_layer((d, n_in), layer),
                  _full((C_CONV, GROUP)), _full((1, GROUP)),
                  _full((GROUP, GROUP)), _full((1, GROUP)),
                  _full((GROUP, GROUP)), _full((1, GROUP)),
                  _full((1, GROUP)), _full((1, GROUP)), _full((GROUP, GROUP))],
        out_specs=[tile(N_A), tile(N_B), tile(GROUP), tile(N_D)],
        scratch_shapes=[pltpu.VMEM((HIST, GROUP), F32), pltpu.VMEM((HIST, GROUP), F32)],
        compiler_params=pltpu.CompilerParams(dimension_semantics=("arbitrary", "arbitrary"),
                                             vmem_limit_bytes=VMEM_LIMIT),
        name="in_proj_rg_lru",
    )(x, mod, g.reshape(1, d), w_bf16, conv_w, row(conv_b),
      _block_diag_weight(ga_w).astype(BF16), row(ga_b),
      _block_diag_weight(gx_w).astype(BF16), row(gx_b), row(lam), row(out_g),
      _block_diag_ones(GROUP, HEAD))


def _seg_sum(x, bd):
    return jnp.dot(x.astype(BF16), bd, preferred_element_type=F32)


def _unit_lower_inverses(mats, row, col):
    n = mats[0].shape[0]

    def same_block(size):
        shift = size.bit_length() - 1
        return (row >> shift) == (col >> shift)

    base = same_block(INV_BASE)
    eye = (row == col).astype(F32)
    powers = [jnp.where(base, -a, 0.0) for a in mats]
    ts = [eye + p for p in powers]
    k = 1
    while 2 * k < INV_BASE:
        powers = [_dot(p, p) for p in powers]
        ts = [t + _dot(t, p) for t, p in zip(ts, powers)]
        k *= 2
    size = INV_BASE
    while size < n:
        sel = same_block(2 * size) & jnp.logical_not(same_block(size))
        halves = [_dot(t, jnp.where(sel, a, 0.0)) for t, a in zip(ts, mats)]
        ts = [t - _dot(half, t) for t, half in zip(ts, halves)]
        size *= 2
    return ts


def _rwkv_kernel(*refs, has_vres):
    if has_vres:
        (p_ref, vf_ref, mu_ref, w0_ref, wup_ref, a0_ref, aup_ref, gup_ref, kk_ref, ka_ref, rk_ref,
         lw_ref, lb_ref, bd64_ref, v0_ref, vdn_ref, vup_ref, y_ref, state_ref, hist_ref) = refs
    else:
        (p_ref, mu_ref, w0_ref, wup_ref, a0_ref, aup_ref, gup_ref, kk_ref, ka_ref, rk_ref,
         lw_ref, lb_ref, bd64_ref, y_ref, v_ref, state_ref, hist_ref) = refs
    nb, cs = y_ref.shape[0], y_ref.shape[1]
    heads = GROUP // HEAD
    ci = pl.program_id(1)

    @pl.when(ci == 0)
    def _():
        state_ref[...] = jnp.zeros_like(state_ref)
        hist_ref[...] = jnp.zeros_like(hist_ref)

    row = lax.broadcasted_iota(jnp.int32, (cs, cs), 0)
    col = lax.broadcasted_iota(jnp.int32, (cs, cs), 1)
    lower_incl = col <= row
    lower_strict = col < row
    masks = [_lane_mask(GROUP, h * HEAD, (h + 1) * HEAD) for h in range(heads)]
    batch = range(nb)
    rows = lambda x, i: x[i * cs:(i + 1) * cs, :]

    ps = [p_ref[i] for i in batch]
    shifted = [_shift_rows(hist_ref[i], ps[i], 1) for i in batch]
    for i in batch:
        hist_ref[i] = ps[i][cs - HIST:, :]
    p = jnp.concatenate(ps, axis=0)
    xs = p + (jnp.concatenate(shifted, axis=0) - p) * mu_ref[...]

    r = xs[:, 0:GROUP]
    k = xs[:, GROUP:2 * GROUP]
    v = xs[:, 2 * GROUP:3 * GROUP]
    lora = xs[:, 3 * GROUP:]
    w = -_softplus(-(w0_ref[...] + _dot(jnp.tanh(lora), wup_ref[...]))) - 0.5
    a = _sigmoid(a0_ref[...] + _dot(lora, aup_ref[...]))
    g = _dot(_sigmoid(lora), gup_ref[...])
    if has_vres:
        mix = _sigmoid(v0_ref[...] + _dot(_dot(v, vdn_ref[...]), vup_ref[...]))
        v = v + (jnp.concatenate([vf_ref[i] for i in batch], axis=0) - v) * mix
    else:
        for i in batch:
            v_ref[i] = rows(v, i)

    bd64 = bd64_ref[...]
    kk = k * kk_ref[...]
    kk = kk / jnp.maximum(jnp.sqrt(_seg_sum(kk * kk, bd64)), 1e-12)
    kmod = k * (1.0 + (a - 1.0) * ka_ref[...])
    log_decay = -jnp.exp(w)

    cum_wide = _dot_split3_lhs_exact(jnp.where(lower_incl, 1.0, 0.0).astype(BF16),
                                     jnp.concatenate([rows(log_decay, i) for i in batch], axis=1))
    cum = jnp.concatenate([cum_wide[:, i * GROUP:(i + 1) * GROUP] for i in batch], axis=0)
    g_inc = jnp.exp(cum)
    g_inv = jnp.exp(-cum)
    kap = kk * jnp.exp(cum - log_decay)
    rt = r * g_inc
    bet = kk * a * g_inv
    kt = kmod * g_inv

    lhs = [jnp.concatenate([rows(kap, i), rows(rt, i)], axis=0) for i in batch]
    rhs = [jnp.concatenate([rows(bet, i), rows(kt, i)], axis=0).astype(BF16) for i in batch]
    pairs = [(i, h) for i in batch for h in range(heads)]
    grams = [_dot_nt(jnp.where(masks[h], lhs[i], 0.0), rhs[i]) for i, h in pairs]
    a_b = [jnp.where(lower_strict, gm[0:cs, 0:cs], 0.0) for gm in grams]
    a_k = [jnp.where(lower_strict, gm[0:cs, cs:], 0.0).astype(BF16) for gm in grams]
    b_r = [jnp.where(lower_incl, gm[cs:, 0:cs], 0.0).astype(BF16) for gm in grams]
    k_r = [jnp.where(lower_incl, gm[cs:, cs:], 0.0).astype(BF16) for gm in grams]
    t_inv = [t.astype(BF16) for t in _unit_lower_inverses(a_b, row, col)]

    def stack_heads(x):
        return jnp.concatenate([jnp.where(m, x, 0.0) for m in masks], axis=0).astype(BF16)

    per_row = lambda mats, i: jnp.concatenate(mats[i * heads:(i + 1) * heads], axis=1)
    dotf = lambda x, y: jnp.dot(x, y, preferred_element_type=F32)
    states = [state_ref[i] for i in batch]
    states_bf = [s.astype(BF16) for s in states]
    v_stack = [stack_heads(rows(v, i)) for i in batch]
    rhs_u = [_dot(rows(kap, i), states_bf[i]) + dotf(per_row(a_k, i), v_stack[i]) for i in batch]
    u = [dotf(per_row(t_inv, i), stack_heads(rhs_u[i])) for i in batch]
    y = [_dot(rows(rt, i), states_bf[i]) + dotf(per_row(k_r, i), v_stack[i])
         - dotf(per_row(b_r, i), stack_heads(u[i])) for i in batch]

    upd = [_dot(jnp.transpose(rows(kt, i)), rows(v, i)) - _dot(jnp.transpose(rows(bet, i)), u[i])
           for i in batch]
    bd_mask = bd64 > 0
    for i in batch:
        g_end = jnp.transpose(rows(g_inc, i))[:, cs - 1:cs]
        state_ref[i] = g_end * (states[i] + jnp.where(bd_mask, upd[i], 0.0))

    y = jnp.concatenate(y, axis=0)
    mean = _seg_sum(y, bd64) * (1.0 / HEAD)
    yc = y - mean
    var = _seg_sum(yc * yc, bd64) * (1.0 / HEAD)
    yn = yc * lax.rsqrt(var + LNX_EPS) * lw_ref[...] + lb_ref[...]
    bonus = _seg_sum(r * kmod * rk_ref[...], bd64) * v
    out = ((yn + bonus) * g).astype(y_ref.dtype)
    for i in batch:
        y_ref[i] = rows(out, i)


def _pad_rows(w, start, total):
    out = jnp.zeros((total, w.shape[1]), w.dtype)
    return lax.dynamic_update_slice(out, w, (start, 0))


def _mixer_rwkv7(p_d, mu, w0, w_up, a0, a_up, g_up, k_k, k_a, r_k, lnx_w, lnx_b, v_first, v_res):
    bsz, seq, _ = p_d.shape
    cs = min(RWKV_CHUNK, seq)
    lora = LORA_W + LORA_A + LORA_G
    row = lambda v: v.reshape(1, -1)
    has_vres = v_res is not None
    nb = math.gcd(RWKV_BATCH, bsz)
    chunk = lambda n: pl.BlockSpec((nb, cs, n), lambda b, i: (b, i, 0))
    args = [p_d]
    specs = [chunk(N_D)]
    if has_vres:
        args.append(v_first)
        specs.append(chunk(GROUP))
    args += [row(mu), row(w0), _pad_rows(w_up, 0, lora).astype(BF16),
             row(a0), _pad_rows(a_up, LORA_W, lora).astype(BF16),
             _pad_rows(g_up, LORA_W + LORA_A, lora).astype(BF16),
             row(k_k), row(k_a), row(r_k), row(lnx_w), row(lnx_b), _block_diag_ones(GROUP, HEAD)]
    specs += [_full((1, N_D)), _full((1, GROUP)), _full((lora, GROUP)),
              _full((1, GROUP)), _full((lora, GROUP)), _full((lora, GROUP)),
              _full((1, GROUP)), _full((1, GROUP)), _full((1, GROUP)), _full((1, GROUP)),
              _full((1, GROUP)), _full((GROUP, GROUP))]
    if has_vres:
        v0, v_down, v_up = v_res
        rank = v_down.shape[1]
        vdn = jnp.zeros((GROUP, lora), F32).at[:, :rank].set(v_down).astype(BF16)
        args += [row(v0), vdn, _pad_rows(v_up, 0, lora).astype(BF16)]
        specs += [_full((1, GROUP)), _full((GROUP, lora)), _full((lora, GROUP))]
        out_shape = jax.ShapeDtypeStruct((bsz, seq, GROUP), BF16)
        out_specs = chunk(GROUP)
    else:
        out_shape = [jax.ShapeDtypeStruct((bsz, seq, GROUP), BF16),
                     jax.ShapeDtypeStruct((bsz, seq, GROUP), F32)]
        out_specs = [chunk(GROUP), chunk(GROUP)]
    out = pl.pallas_call(
        functools.partial(_rwkv_kernel, has_vres=has_vres),
        out_shape=out_shape,
        grid=(bsz // nb, seq // cs),
        in_specs=specs,
        out_specs=out_specs,
        scratch_shapes=[pltpu.VMEM((nb, GROUP, GROUP), F32), pltpu.VMEM((nb, HIST, N_D), F32)],
        compiler_params=pltpu.CompilerParams(dimension_semantics=("arbitrary", "arbitrary"),
                                             vmem_limit_bytes=VMEM_LIMIT),
        name="rwkv7_chunked",
    )(*args)
    if has_vres:
        return out, None
    return out[0], out[1]


def _out_ffn_kernel(x_ref, ya_ref, yb_ref, yc_ref, yd_ref, mod_ref, g_ref, wo_ref, wup_ref,
                    cw_ref, cb_ref, wdn_ref, o_ref, hist_ref):
    tm, d = x_ref.shape[1], x_ref.shape[2]
    d_ff = wdn_ref.shape[0]
    ti = pl.program_id(1)

    @pl.when(ti == 0)
    def _():
        hist_ref[...] = jnp.zeros_like(hist_ref)

    mod = mod_ref[0]
    gate1 = mod[:, 2 * d:3 * d]
    shift2, scale2, gate2 = mod[:, 3 * d:4 * d], mod[:, 4 * d:5 * d], mod[:, 5 * d:6 * d]
    y_cat = jnp.concatenate([ya_ref[0], yb_ref[0], yc_ref[0], yd_ref[0]], axis=1)
    mix = jnp.dot(y_cat, wo_ref[...], preferred_element_type=F32)
    x1 = x_ref[0] + gate1 * mix
    h = _modulated_norm(x1, g_ref[...], shift2, scale2).astype(BF16)

    def conv(u, lo):
        cw = cw_ref[:, lo:lo + FF_CHUNK]
        hist = hist_ref[:, lo:lo + FF_CHUNK]
        out = cw[FF_CONV - 1:FF_CONV] * u + cb_ref[:, lo:lo + FF_CHUNK]
        for dly in range(1, FF_CONV):
            out = out + cw[FF_CONV - 1 - dly:FF_CONV - dly] * _shift_rows(hist, u, dly)
        hist_ref[:, lo:lo + FF_CHUNK] = u[tm - HIST:, :]
        return out

    def up_proj(j):
        lo_g, lo_v = j * FF_CHUNK, d_ff + j * FF_CHUNK
        return (jnp.dot(h, wup_ref[:, lo_g:lo_g + FF_CHUNK], preferred_element_type=F32),
                jnp.dot(h, wup_ref[:, lo_v:lo_v + FF_CHUNK], preferred_element_type=F32))

    n_chunks = d_ff // FF_CHUNK
    acc = None
    ahead = up_proj(0)
    acts, first, pending = [], 0, None
    for j in range(n_chunks):
        raw_g, raw_v = ahead
        if j + 1 < n_chunks:
            ahead = up_proj(j + 1)
        if pending is not None:
            lo_p, act_group = pending
            part = jnp.dot(act_group, wdn_ref[lo_p:lo_p + act_group.shape[1], :],
                           preferred_element_type=F32)
            acc = part if acc is None else acc + part
            pending = None
        lo_g, lo_v = j * FF_CHUNK, d_ff + j * FF_CHUNK
        u_g = conv(raw_g, lo_g)
        u_v = conv(raw_v, lo_v)
        acts.append((u_g * _sigmoid(u_g) * u_v).astype(BF16))
        if len(acts) == FF_DOWN_GROUP or j + 1 == n_chunks:
            pending = (first * FF_CHUNK, jnp.concatenate(acts, axis=1))
            acts, first = [], j + 1
    lo_p, act_group = pending
    part = jnp.dot(act_group, wdn_ref[lo_p:lo_p + act_group.shape[1], :], preferred_element_type=F32)
    acc = part if acc is None else acc + part
    o_ref[0] = x1 + gate2 * acc


def _out_ffn(x, ys, mod, g2, w_out, w_up, conv_w, conv_b, w_down, layer=0):
    bsz, seq, d = x.shape
    w_out, w_up, w_down = _stacked(w_out), _stacked(w_up), _stacked(w_down)
    d_ff = w_down.shape[-2]
    tm = min(FFN_ROW_TILE, seq)
    tile = lambda n: pl.BlockSpec((1, tm, n), lambda b, i: (b, i, 0))
    return pl.pallas_call(
        _out_ffn_kernel,
        out_shape=jax.ShapeDtypeStruct((bsz, seq, d), F32),
        grid=(bsz, seq // tm),
        in_specs=[tile(d), tile(GROUP), tile(GROUP), tile(GROUP), tile(GROUP),
                  pl.BlockSpec((1, 1, mod.shape[-1]), lambda b, i: (b, 0, 0)),
                  _full((1, d)), _resident_layer((4 * GROUP, d), layer),
                  _resident_layer((d, 2 * d_ff), layer),
                  _full((FF_CONV, 2 * d_ff)), _full((1, 2 * d_ff)),
                  _resident_layer((d_ff, d), layer)],
        out_specs=tile(d),
        scratch_shapes=[pltpu.VMEM((HIST, 2 * d_ff), F32)],
        compiler_params=pltpu.CompilerParams(dimension_semantics=("arbitrary", "arbitrary"),
                                             vmem_limit_bytes=VMEM_LIMIT),
        name="out_proj_ffn",
    )(x, *ys, mod, g2.reshape(1, d), w_out, w_up, conv_w, conv_b.reshape(1, -1), w_down)


def kernel(x, c, w_ada, b_ada, norm1_g, norm2_g, w_in, w_out, a_qnorm_g, a_knorm_g, a_lam_q, a_lam_k, a_out_g, b_out_g, c_conv_w, c_conv_b, c_gate_a_w, c_gate_a_b, c_gate_x_w, c_gate_x_b, c_lambda, c_out_g, d_mu, d_w0, d_w_up, d_a0, d_a_up, d_g_up, d_k_k, d_k_a, d_r_k, d_lnx_w, d_lnx_b, d_v0, d_v_down, d_v_up, ff_w_up, ff_conv_w, ff_conv_b, ff_w_down):
    depth = w_in.shape[0]
    bsz = x.shape[0]
    mods = _ada_modulation(c, w_ada, b_ada)
    w_in_bf, w_out_bf = w_in.astype(BF16), w_out.astype(BF16)
    ff_w_up_bf, ff_w_down_bf = ff_w_up.astype(BF16), ff_w_down.astype(BF16)
    v_first = None
    for l in range(depth):
        mod = mods[l].reshape(bsz, 1, -1)
        p_a, p_b, y_c, p_d = _in_proj_lru(x, mod, norm1_g[l], w_in_bf, l, c_conv_w[l], c_conv_b[l],
                                          c_gate_a_w[l], c_gate_a_b[l], c_gate_x_w[l],
                                          c_gate_x_b[l], c_lambda[l], c_out_g[l])
        lam_init = 0.8 - 0.6 * math.exp(-0.3 * l)
        y_a, y_b = _mixer_attention_pair(p_a, a_qnorm_g[l], a_knorm_g[l], a_lam_q[l], a_lam_k[l],
                                         a_out_g[l], lam_init, p_b, b_out_g[l])
        v_res = None if l == 0 else (d_v0[l - 1], d_v_down[l - 1], d_v_up[l - 1])
        y_d, v_d = _mixer_rwkv7(p_d, d_mu[l], d_w0[l], d_w_up[l], d_a0[l], d_a_up[l], d_g_up[l],
                                d_k_k[l], d_k_a[l], d_r_k[l].reshape(-1), d_lnx_w[l], d_lnx_b[l],
                                v_first, v_res)
        if l == 0:
            v_first = v_d
        x = _out_ffn(x, (y_a, y_b, y_c, y_d), mod, norm2_g[l], w_out_bf, ff_w_up_bf,
                     ff_conv_w[l], ff_conv_b[l], ff_w_down_bf, l)
    return x
```

```python
import functools
import math

import numpy as np
import jax
import jax.numpy as jnp
from jax import lax
from jax.experimental import pallas as pl
from jax.experimental.pallas import tpu as pltpu

F32 = jnp.float32
BF16 = jnp.bfloat16

GROUP = 256
A_HEADS = 4
A_QK = 32
HEAD = 64
N_A = 3 * GROUP
N_B = 3 * GROUP
N_C = 2 * GROUP
LORA_W = 32
LORA_A = 32
LORA_G = 64
N_D = 3 * GROUP + LORA_W + LORA_A + LORA_G
C_CONV = 4
C_EXP = 8.0
FF_CONV = 3
LOG2_E = 1.4426950408889634
RMS_EPS = 1e-6
LNX_EPS = 64e-5
ADA_CHUNKS = 6

ROW_TILE = 512
PROJ_PIECE = 256
LRU_PHASES = 7
FFN_ROW_TILE = 512
ATT_TILE = 256
LRU_TILE = 256
RWKV_CHUNK = 128
RWKV_BATCH = 4
FF_CHUNK = 256
FF_DOWN_GROUP = 4
SUM_ROWS = 16
INV_BASE = 8
HIST = 8
VMEM_LIMIT = 56 * 1024 * 1024


def _dot(a, b):
    return jnp.dot(a.astype(BF16), b.astype(BF16), preferred_element_type=F32)


def _dot_nt(a, b):
    return lax.dot_general(a.astype(BF16), b.astype(BF16), (((1,), (1,)), ((), ())),
                           preferred_element_type=F32)


def _split2(x):
    hi = x.astype(BF16)
    lo = (x - hi.astype(F32)).astype(BF16)
    return hi, lo


def _dot_split2(x, m):
    hi, lo = _split2(x)
    return (jnp.dot(hi, m, preferred_element_type=F32)
            + jnp.dot(lo, m, preferred_element_type=F32))


def _dot_split3_lhs_exact(m, x):
    hi = x.astype(BF16)
    r1 = x - hi.astype(F32)
    mid = r1.astype(BF16)
    lo = (r1 - mid.astype(F32)).astype(BF16)
    return (jnp.dot(m, hi, preferred_element_type=F32)
            + jnp.dot(m, mid, preferred_element_type=F32)
            + jnp.dot(m, lo, preferred_element_type=F32))


def _sigmoid(x):
    return 1.0 / (1.0 + jnp.exp(-x))


def _softplus(x):
    return jnp.maximum(x, 0.0) + jnp.log1p(jnp.exp(-jnp.abs(x)))


def _lane_mask(width, lo, hi):
    lane = lax.broadcasted_iota(jnp.int32, (1, width), 1)
    return (lane >= lo) & (lane < hi)


def _shift_rows(hist, x, d):
    ext = jnp.concatenate([hist, x], axis=0)
    return pltpu.roll(ext, d, axis=0)[HIST:, :]


def _full(shape):
    nd = len(shape)
    return pl.BlockSpec(shape, lambda *_: (0,) * nd)


def _resident(shape):
    nd = len(shape)
    return pl.BlockSpec(shape, lambda *_: (0,) * nd, pipeline_mode=pl.Buffered(1))


def _resident_layer(shape, layer):
    return pl.BlockSpec((None,) + tuple(shape), lambda *_: (layer, 0, 0),
                        pipeline_mode=pl.Buffered(1))


def _stacked(w):
    return w[None] if w.ndim == 2 else w


def _block_diag_ones(width, block):
    idx = np.arange(width) // block
    return jnp.asarray((idx[:, None] == idx[None, :]).astype(np.float32), dtype=BF16)


def _ada_kernel(c_ref, w_ref, b_ref, o_ref):
    c = c_ref[...]
    cond = c * _sigmoid(c)
    o_ref[0] = _dot(cond, w_ref[0]) + b_ref[0]


def _ada_modulation(c, w_ada, b_ada):
    depth, d, n = w_ada.shape
    bsz = c.shape[0]
    tn = 1536
    return pl.pallas_call(
        _ada_kernel,
        out_shape=jax.ShapeDtypeStruct((depth, bsz, n), F32),
        grid=(depth, n // tn),
        in_specs=[pl.BlockSpec((bsz, d), lambda l, j: (0, 0)),
                  pl.BlockSpec((1, d, tn), lambda l, j: (l, 0, j)),
                  pl.BlockSpec((1, 1, tn), lambda l, j: (l, 0, j))],
        out_specs=pl.BlockSpec((1, bsz, tn), lambda l, j: (l, 0, j)),
        compiler_params=pltpu.CompilerParams(dimension_semantics=("arbitrary", "arbitrary"),
                                             vmem_limit_bytes=VMEM_LIMIT),
        name="ada_modulation",
    )(c, w_ada, b_ada.reshape(depth, 1, n))


def _modulated_norm(x, g, shift, scale):
    ms = jnp.mean(x * x, axis=-1, keepdims=True)
    return (x * lax.rsqrt(ms + RMS_EPS) * g) * (1.0 + scale) + shift


def _in_proj_kernel(x_ref, mod_ref, g_ref, w_ref, pa_ref, pb_ref, pc_ref, pd_ref):
    d = x_ref.shape[-1]
    mod = mod_ref[0]
    h = _modulated_norm(x_ref[0], g_ref[...], mod[:, 0:d], mod[:, d:2 * d]).astype(BF16)
    pa_ref[0] = jnp.dot(h, w_ref[:, 0:N_A], preferred_element_type=F32)
    pb_ref[0] = jnp.dot(h, w_ref[:, N_A:N_A + N_B], preferred_element_type=F32)
    pc_ref[0] = jnp.dot(h, w_ref[:, N_A + N_B:N_A + N_B + N_C], preferred_element_type=F32)
    pd_ref[0] = jnp.dot(h, w_ref[:, N_A + N_B + N_C:], preferred_element_type=F32)


def _in_proj(x, mod, g, w_bf16, layer=0):
    bsz, seq, d = x.shape
    w_bf16 = _stacked(w_bf16)
    n_in = w_bf16.shape[-1]
    tm = min(ROW_TILE, seq)
    widths = (N_A, N_B, N_C, N_D)
    return pl.pallas_call(
        _in_proj_kernel,
        out_shape=[jax.ShapeDtypeStruct((bsz, seq, n), F32) for n in widths],
        grid=(bsz, seq // tm),
        in_specs=[pl.BlockSpec((1, tm, d), lambda b, i: (b, i, 0)),
                  pl.BlockSpec((1, 1, mod.shape[-1]), lambda b, i: (b, 0, 0)),
                  _full((1, d)),
                  _resident_layer((d, n_in), layer)],
        out_specs=[pl.BlockSpec((1, tm, n), lambda b, i: (b, i, 0)) for n in widths],
        compiler_params=pltpu.CompilerParams(dimension_semantics=("arbitrary", "arbitrary"),
                                             vmem_limit_bytes=VMEM_LIMIT),
        name="in_proj",
    )(x, mod, g.reshape(1, d), w_bf16)


def _group_rms(x, bd, group, gain):
    ms = _dot_split2(x * x, bd) * (1.0 / group)
    return x * lax.rsqrt(ms + RMS_EPS) * gain


def _attn_a_setup(p_ref, qg_ref, kg_ref, lq_ref, lk_ref, og_ref, bd32_ref,
                  o_ref, kn_ref, vt_ref, lam_init):
    tq = o_ref.shape[1]
    n_kt = vt_ref.shape[0]
    qi = pl.program_id(1)
    bd32 = bd32_ref[...]

    @pl.when(qi == 0)
    def _():
        k = p_ref[0, :, GROUP:2 * GROUP]
        kn_ref[...] = _group_rms(k, bd32, A_QK, kg_ref[...]).astype(BF16)
        ones_rows = jnp.ones((SUM_ROWS, tq), BF16)
        for j in range(n_kt):
            vt = jnp.transpose(p_ref[0, j * tq:(j + 1) * tq, 2 * GROUP:3 * GROUP])
            for h in range(A_HEADS):
                vt_ref[j, h, 0:HEAD, :] = vt[h * HEAD:(h + 1) * HEAD, :].astype(BF16)
                vt_ref[j, h, HEAD:, :] = ones_rows

    lq = lq_ref[...]
    lk = lk_ref[...]
    lam = (jnp.exp(jnp.sum(lq[0:1] * lk[0:1], axis=-1, keepdims=True))
           - jnp.exp(jnp.sum(lq[1:2] * lk[1:2], axis=-1, keepdims=True)) + lam_init)

    q0 = pl.multiple_of(qi * tq, tq)
    q = p_ref[0, pl.ds(q0, tq), 0:GROUP]
    qn = _group_rms(q, bd32, A_QK, qg_ref[...]) * (A_QK ** -0.5 * LOG2_E)

    key_idx = lax.broadcasted_iota(jnp.int32, (tq, tq), 0)
    query_idx = lax.broadcasted_iota(jnp.int32, (tq, tq), 1)
    causal = key_idx <= query_idx

    qt = jnp.transpose(qn)
    feat = lax.broadcasted_iota(jnp.int32, (GROUP, 1), 0)
    qms = [[jnp.where((feat >= h * HEAD + c * A_QK) & (feat < h * HEAD + (c + 1) * A_QK), qt, 0.0
                      ).astype(BF16) for h in range(A_HEADS)] for c in range(2)]

    def key_tile(kj, carry, diag):
        kb = kn_ref[pl.ds(pl.multiple_of(kj * tq, tq), tq), :]
        dotf = lambda x, y: jnp.dot(x, y, preferred_element_type=F32)
        chains = [(c, h) for c in range(2) for h in range(A_HEADS)]
        ms = [carry[c][0][h] for c, h in chains]
        ls = [carry[c][1][h] for c, h in chains]
        accs = [carry[c][2][h] for c, h in chains]
        ss = [dotf(kb, qms[c][h]) for c, h in chains]
        yield
        if diag:
            ss = [jnp.where(causal, s, -jnp.inf) for s in ss]
        ms_new = [jnp.maximum(m, jnp.max(s, axis=0, keepdims=True)) for m, s in zip(ms, ss)]
        alphas = [jnp.exp2(m - m_new) for m, m_new in zip(ms, ms_new)]
        ps = [jnp.exp2(s - m_new).astype(BF16) for s, m_new in zip(ss, ms_new)]
        yield
        res = [dotf(vt_ref[kj, h], p) for (c, h), p in zip(chains, ps)]
        yield
        accs_new = [alpha * acc + r[0:HEAD, :] for alpha, acc, r in zip(alphas, accs, res)]
        ls_new = [alpha * l + r[HEAD:HEAD + 1, :] for alpha, l, r in zip(alphas, ls, res)]
        n = A_HEADS
        return tuple((tuple(ms_new[c * n:(c + 1) * n]), tuple(ls_new[c * n:(c + 1) * n]),
                      tuple(accs_new[c * n:(c + 1) * n])) for c in range(2))

    init_c = (tuple(jnp.full((1, tq), -jnp.inf, F32) for _ in range(A_HEADS)),
              tuple(jnp.zeros((1, tq), F32) for _ in range(A_HEADS)),
              tuple(jnp.zeros((HEAD, tq), F32) for _ in range(A_HEADS)))
    def finish(carry):
        (_, l0, acc0), (_, l1, acc1) = carry
        og = og_ref[...]
        ys = []
        for h in range(A_HEADS):
            o = acc0[h] * (1.0 / l0[h]) - lam * (acc1[h] * (1.0 / l1[h]))
            ms = jnp.mean(o * o, axis=0, keepdims=True)
            ys.append(o * lax.rsqrt(ms + RMS_EPS) * og[h * HEAD:(h + 1) * HEAD, :])
        y = jnp.transpose(jnp.concatenate(ys, axis=0)) * (1.0 - lam_init)
        o_ref[0] = y.astype(o_ref.dtype)

    return (init_c, init_c), key_tile, finish


def _run_steps(*gens):
    results = [None] * len(gens)
    live = list(range(len(gens)))
    while live:
        for idx in list(live):
            try:
                next(gens[idx])
            except StopIteration as done:
                results[idx] = done.value
                live.remove(idx)
    return results


def _attn_a_kernel(*refs, lam_init):
    qi = pl.program_id(1)
    init, key_tile, finish = _attn_a_setup(*refs, lam_init)
    carry = lax.fori_loop(0, qi, lambda kj, carry: _run_steps(key_tile(kj, carry, False))[0], init)
    finish(_run_steps(key_tile(qi, carry, True))[0])


def _mixer_diff_attn(p_a, q_g, k_g, lam_q, lam_k, out_g, lam_init):
    bsz, seq, _ = p_a.shape
    tq = min(ATT_TILE, seq)
    reps = GROUP // A_QK
    return pl.pallas_call(
        functools.partial(_attn_a_kernel, lam_init=lam_init),
        out_shape=jax.ShapeDtypeStruct((bsz, seq, GROUP), BF16),
        grid=(bsz, seq // tq),
        in_specs=[pl.BlockSpec((1, seq, N_A), lambda b, i: (b, 0, 0)),
                  _full((1, GROUP)), _full((1, GROUP)),
                  _full((2, A_QK)), _full((2, A_QK)),
                  _full((GROUP, 1)),
                  _full((GROUP, GROUP))],
        out_specs=pl.BlockSpec((1, tq, GROUP), lambda b, i: (b, i, 0)),
        scratch_shapes=[pltpu.VMEM((seq, GROUP), BF16),
                        pltpu.VMEM((seq // tq, A_HEADS, HEAD + SUM_ROWS, tq), BF16)],
        compiler_params=pltpu.CompilerParams(dimension_semantics=("arbitrary", "arbitrary"),
                                             vmem_limit_bytes=VMEM_LIMIT),
        name="diff_attention",
    )(p_a, jnp.tile(q_g, reps).reshape(1, GROUP), jnp.tile(k_g, reps).reshape(1, GROUP),
      lam_q, lam_k, out_g.reshape(GROUP, 1), _block_diag_ones(GROUP, A_QK))


def _attn_b_setup(p_ref, og_ref, tri_ref, o_ref, kb_ref, vt_ref):
    tq = o_ref.shape[1]
    n_kt = vt_ref.shape[0]
    heads = GROUP // HEAD
    qi = pl.program_id(1)
    feat = lax.broadcasted_iota(jnp.int32, (GROUP, 1), 0)
    head_rows = [(feat >= h * HEAD) & (feat < (h + 1) * HEAD) for h in range(heads)]

    @pl.when(qi == 0)
    def _():
        kb_ref[...] = p_ref[0, :, GROUP:2 * GROUP].astype(BF16)
        for j in range(n_kt):
            vt = jnp.transpose(p_ref[0, j * tq:(j + 1) * tq, 2 * GROUP:3 * GROUP])
            vt_ref[j] = vt.astype(BF16)

    q0 = pl.multiple_of(qi * tq, tq)
    qt = jnp.transpose(p_ref[0, pl.ds(q0, tq), 0:GROUP] * (HEAD ** -0.5 * LOG2_E))
    qms = [jnp.where(hr, qt, 0.0).astype(BF16) for hr in head_rows]
    tri = tri_ref[...]
    key_idx = lax.broadcasted_iota(jnp.int32, (tq, tq), 0)
    query_idx = lax.broadcasted_iota(jnp.int32, (tq, tq), 1)
    strict = key_idx < query_idx
    dotf = lambda x, y: jnp.dot(x, y, preferred_element_type=F32)

    def key_tile(kj, carry, diag):
        laters, accs = carry
        kb = kb_ref[pl.ds(pl.multiple_of(kj * tq, tq), tq), :]
        zs = [dotf(kb, qm) for qm in qms]
        yield
        neg_keeps = [jnp.maximum(z, 0.0) + jnp.log2(1.0 + jnp.exp2(-jnp.abs(z))) for z in zs]
        if diag:
            neg_keeps = [jnp.where(strict, nk, 0.0) for nk in neg_keeps]
        splits = [jnp.concatenate(_split2(nk), axis=0) for nk in neg_keeps]
        yield
        incls = [dotf(tri, sp) for sp in splits]
        yield
        ws = [jnp.exp2(z + incl + later) for z, incl, later in zip(zs, incls, laters)]
        if diag:
            ws = [jnp.where(strict, w, 0.0) for w in ws]
        yield
        accs = tuple(acc + dotf(vt_ref[kj, h * HEAD:(h + 1) * HEAD, :], w.astype(BF16))
                     for h, (acc, w) in enumerate(zip(accs, ws)))
        return tuple(later + incl[0:1, :] for later, incl in zip(laters, incls)), accs

    init = (tuple(jnp.zeros((1, tq), F32) for _ in range(heads)),
            tuple(jnp.zeros((HEAD, tq), F32) for _ in range(heads)))

    def finish(carry):
        _, accs = carry
        og = og_ref[...]
        ys = []
        for h in range(heads):
            o = accs[h]
            ms = jnp.mean(o * o, axis=0, keepdims=True)
            ys.append(o * lax.rsqrt(ms + RMS_EPS) * og[h * HEAD:(h + 1) * HEAD, :])
        o_ref[0] = jnp.transpose(jnp.concatenate(ys, axis=0)).astype(o_ref.dtype)

    return init, key_tile, finish


def _attn_b_kernel(*refs):
    qi = pl.program_id(1)
    init, key_tile, finish = _attn_b_setup(*refs)
    carry = _run_steps(key_tile(qi, init, True))[0]
    finish(lax.fori_loop(
        0, qi, lambda i, carry: _run_steps(key_tile(qi - 1 - i, carry, False))[0], carry))


def _attn_ab_kernel(pa_ref, qg_ref, kg_ref, lq_ref, lk_ref, oga_ref, bd32_ref, pb_ref, ogb_ref, tri_ref,
                    oa_ref, ob_ref, kn_ref, vta_ref, kb_ref, vtb_ref, *, lam_init):
    qi = pl.program_id(1)
    a_init, a_tile, a_finish = _attn_a_setup(pa_ref, qg_ref, kg_ref, lq_ref, lk_ref, oga_ref, bd32_ref,
                                             oa_ref, kn_ref, vta_ref, lam_init)
    b_init, b_tile, b_finish = _attn_b_setup(pb_ref, ogb_ref, tri_ref, ob_ref, kb_ref, vtb_ref)
    both = lambda kj, c, diag: tuple(_run_steps(a_tile(kj, c[0], diag), b_tile(kj, c[1], diag)))
    carry = both(qi, (a_init, b_init), True)
    carry = lax.fori_loop(0, qi, lambda i, c: both(qi - 1 - i, c, False), carry)
    a_finish(carry[0])
    b_finish(carry[1])


def _mixer_stick_breaking(p_b, out_g):
    bsz, seq, _ = p_b.shape
    tq = min(ATT_TILE, seq)
    neg_upper = -np.triu(np.ones((tq, tq), np.float32))
    tri = jnp.asarray(np.concatenate([neg_upper, neg_upper], axis=1), dtype=BF16)
    return pl.pallas_call(
        _attn_b_kernel,
        out_shape=jax.ShapeDtypeStruct((bsz, seq, GROUP), BF16),
        grid=(bsz, seq // tq),
        in_specs=[pl.BlockSpec((1, seq, N_B), lambda b, i: (b, 0, 0)),
                  _full((GROUP, 1)), _full((tq, 2 * tq))],
        out_specs=pl.BlockSpec((1, tq, GROUP), lambda b, i: (b, i, 0)),
        scratch_shapes=[pltpu.VMEM((seq, GROUP), BF16),
                        pltpu.VMEM((seq // tq, GROUP, tq), BF16)],
        compiler_params=pltpu.CompilerParams(dimension_semantics=("arbitrary", "arbitrary"),
                                             vmem_limit_bytes=VMEM_LIMIT),
        name="stick_breaking_attention",
    )(p_b, out_g.reshape(GROUP, 1), tri)


def _mixer_attention_pair(p_a, q_g, k_g, lam_q, lam_k, a_out_g, lam_init, p_b, b_out_g):
    bsz, seq, _ = p_a.shape
    tq = min(ATT_TILE, seq)
    reps = GROUP // A_QK
    neg_upper = -np.triu(np.ones((tq, tq), np.float32))
    tri = jnp.asarray(np.concatenate([neg_upper, neg_upper], axis=1), dtype=BF16)
    whole = lambda n: pl.BlockSpec((1, seq, n), lambda b, i: (b, 0, 0))
    tile = pl.BlockSpec((1, tq, GROUP), lambda b, i: (b, i, 0))
    return pl.pallas_call(
        functools.partial(_attn_ab_kernel, lam_init=lam_init),
        out_shape=[jax.ShapeDtypeStruct((bsz, seq, GROUP), BF16)] * 2,
        grid=(bsz, seq // tq),
        in_specs=[whole(N_A), _full((1, GROUP)), _full((1, GROUP)),
                  _full((2, A_QK)), _full((2, A_QK)), _full((GROUP, 1)), _full((GROUP, GROUP)),
                  whole(N_B), _full((GROUP, 1)), _full((tq, 2 * tq))],
        out_specs=[tile, tile],
        scratch_shapes=[pltpu.VMEM((seq, GROUP), BF16),
                        pltpu.VMEM((seq // tq, A_HEADS, HEAD + SUM_ROWS, tq), BF16),
                        pltpu.VMEM((seq, GROUP), BF16),
                        pltpu.VMEM((seq // tq, GROUP, tq), BF16)],
        compiler_params=pltpu.CompilerParams(dimension_semantics=("arbitrary", "arbitrary"),
                                             vmem_limit_bytes=VMEM_LIMIT),
        name="attention_pair",
    )(p_a, jnp.tile(q_g, reps).reshape(1, GROUP), jnp.tile(k_g, reps).reshape(1, GROUP),
      lam_q, lam_k, a_out_g.reshape(GROUP, 1), _block_diag_ones(GROUP, A_QK),
      p_b, b_out_g.reshape(GROUP, 1), tri)


def _gelu_tanh(x):
    return 0.5 * x * (1.0 + jnp.tanh(math.sqrt(2.0 / math.pi) * (x + 0.044715 * (x * x * x))))


def _rglru_kernel(p_ref, cw_ref, cb_ref, gaw_ref, gab_ref, gxw_ref, gxb_ref, lam_ref, og_ref,
                  bd64_ref, o_ref, hist_ref, h_ref):
    y = _rglru_rows(p_ref[0], pl.program_id(1), cw_ref, cb_ref, gaw_ref, gab_ref, gxw_ref, gxb_ref,
                    lam_ref, og_ref, bd64_ref, hist_ref, h_ref)
    o_ref[0] = y.astype(o_ref.dtype)


def _rglru_rows(*args):
    return _run_steps(_rglru_steps(*args))[0]


def _rglru_steps(p, ti, cw_ref, cb_ref, gaw_ref, gab_ref, gxw_ref, gxb_ref, lam_ref, og_ref,
                 bd64_ref, hist_ref, h_ref):
    tt = p.shape[0]

    @pl.when(ti == 0)
    def _():
        hist_ref[...] = jnp.zeros_like(hist_ref)
        h_ref[...] = jnp.zeros_like(h_ref)

    x_raw = p[:, 0:GROUP]
    x_gate = p[:, GROUP:2 * GROUP]
    hist = hist_ref[...]
    cw = cw_ref[...]
    x = cw[C_CONV - 1:C_CONV] * x_raw + cb_ref[...]
    for d in range(1, C_CONV):
        x = x + cw[C_CONV - 1 - d:C_CONV - d] * _shift_rows(hist, x_raw, d)
    hist_ref[...] = x_raw[tt - HIST:, :]
    yield

    r = _sigmoid(_dot(x, gaw_ref[...]) + gab_ref[...])
    i = _sigmoid(_dot(x, gxw_ref[...]) + gxb_ref[...])
    yield
    log_a = (-C_EXP) * r * _softplus(-lam_ref[...])
    a = jnp.exp(log_a)
    mult = jnp.sqrt(jnp.tanh(-log_a) * (a * a + 1.0))
    row = lax.broadcasted_iota(jnp.int32, (tt, 1), 0)
    mult = jnp.where((row == 0) & (ti == 0), 1.0, mult)
    u = mult * i * x
    yield

    d = 1
    while d < HIST:
        a_prev = pltpu.roll(a, d, axis=0)
        u_prev = pltpu.roll(u, d, axis=0)
        keep = (row & (HIST - 1)) >= d
        u = jnp.where(keep, a * u_prev + u, u)
        a = jnp.where(keep, a * a_prev, a)
        d *= 2
    yield
    state = h_ref[0:1, :]
    blocks = []
    for k in range(tt // HIST):
        blk = u[k * HIST:(k + 1) * HIST, :] + a[k * HIST:(k + 1) * HIST, :] * state
        state = blk[HIST - 1:HIST, :]
        blocks.append(blk)
    h = jnp.concatenate(blocks, axis=0)
    h_ref[...] = jnp.broadcast_to(state, h_ref.shape)
    yield

    y = h * _gelu_tanh(x_gate)
    yield
    return _group_rms(y, bd64_ref[...], HEAD, og_ref[...])


def _block_diag_weight(w):
    nb, d, _ = w.shape
    eye = jnp.eye(nb, dtype=w.dtype)
    return (eye[:, None, :, None] * w[:, :, None, :]).reshape(nb * d, nb * d)


def _mixer_rglru(p_c, conv_w, conv_b, ga_w, ga_b, gx_w, gx_b, lam, out_g):
    bsz, seq, _ = p_c.shape
    tt = min(LRU_TILE, seq)
    row = lambda v: v.reshape(1, GROUP)
    return pl.pallas_call(
        _rglru_kernel,
        out_shape=jax.ShapeDtypeStruct((bsz, seq, GROUP), BF16),
        grid=(bsz, seq // tt),
        in_specs=[pl.BlockSpec((1, tt, N_C), lambda b, i: (b, i, 0)),
                  _full((C_CONV, GROUP)), _full((1, GROUP)),
                  _full((GROUP, GROUP)), _full((1, GROUP)),
                  _full((GROUP, GROUP)), _full((1, GROUP)),
                  _full((1, GROUP)), _full((1, GROUP)), _full((GROUP, GROUP))],
        out_specs=pl.BlockSpec((1, tt, GROUP), lambda b, i: (b, i, 0)),
        scratch_shapes=[pltpu.VMEM((HIST, GROUP), F32), pltpu.VMEM((HIST, GROUP), F32)],
        compiler_params=pltpu.CompilerParams(dimension_semantics=("arbitrary", "arbitrary"),
                                             vmem_limit_bytes=VMEM_LIMIT),
        name="rg_lru",
    )(p_c, conv_w, row(conv_b), _block_diag_weight(ga_w).astype(BF16), row(ga_b),
      _block_diag_weight(gx_w).astype(BF16), row(gx_b), row(lam), row(out_g),
      _block_diag_ones(GROUP, HEAD))


def _in_proj_lru_kernel(x_ref, mod_ref, g_ref, w_ref, cw_ref, cb_ref, gaw_ref, gab_ref, gxw_ref,
                        gxb_ref, lam_ref, og_ref, bd64_ref, pa_ref, pb_ref, yc_ref, pd_ref,
                        hist_ref, h_ref):
    d = x_ref.shape[-1]
    mod = mod_ref[0]
    h = _modulated_norm(x_ref[0], g_ref[...], mod[:, 0:d], mod[:, d:2 * d]).astype(BF16)
    p_c = jnp.dot(h, w_ref[:, N_A + N_B:N_A + N_B + N_C], preferred_element_type=F32)
    lru = _rglru_steps(p_c, pl.program_id(1), cw_ref, cb_ref, gaw_ref, gab_ref, gxw_ref, gxb_ref,
                       lam_ref, og_ref, bd64_ref, hist_ref, h_ref)
    pieces = [(ref, lo, col0 + lo, min(PROJ_PIECE, width - lo))
              for ref, col0, width in ((pa_ref, 0, N_A), (pb_ref, N_A, N_B),
                                       (pd_ref, N_A + N_B + N_C, N_D))
              for lo in range(0, width, PROJ_PIECE)]
    y = None
    for k, (ref, lo, col, n) in enumerate(pieces):
        ref[0, :, lo:lo + n] = jnp.dot(h, w_ref[:, col:col + n], preferred_element_type=F32)
        if (k + 1) * LRU_PHASES // len(pieces) > k * LRU_PHASES // len(pieces) and y is None:
            try:
                next(lru)
            except StopIteration as done:
                y = done.value
    if y is None:
        y = _run_steps(lru)[0]
    yc_ref[0] = y.astype(yc_ref.dtype)


def _in_proj_lru(x, mod, g, w_bf16, layer, conv_w, conv_b, ga_w, ga_b, gx_w, gx_b, lam, out_g):
    bsz, seq, d = x.shape
    w_bf16 = _stacked(w_bf16)
    n_in = w_bf16.shape[-1]
    tm = min(ROW_TILE, seq)
    row = lambda v: v.reshape(1, GROUP)
    rows_f32 = lambda n: jax.ShapeDtypeStruct((bsz, seq, n), F32)
    tile = lambda n: pl.BlockSpec((1, tm, n), lambda b, i: (b, i, 0))
    return pl.pallas_call(
        _in_proj_lru_kernel,
        out_shape=[rows_f32(N_A), rows_f32(N_B), jax.ShapeDtypeStruct((bsz, seq, GROUP), BF16),
                   rows_f32(N_D)],
        grid=(bsz, seq // tm),
        in_specs=[tile(d),
                  pl.BlockSpec((1, 1, mod.shape[-1]), lambda b, i: (b, 0, 0)),
                  _full((1, d)),
                  _resident_layer((d, n_in), layer),
                  _full((C_CONV, GROUP)), _full((1, GROUP)),
                  _full((GROUP, GROUP)), _full((1, GROUP)),
                  _full((GROUP, GROUP)), _full((1, GROUP)),
                  _full((1, GROUP)), _full((1, GROUP)), _full((GROUP, GROUP))],
        out_specs=[tile(N_A), tile(N_B), tile(GROUP), tile(N_D)],
        scratch_shapes=[pltpu.VMEM((HIST, GROUP), F32), pltpu.VMEM((HIST, GROUP), F32)],
        compiler_params=pltpu.CompilerParams(dimension_semantics=("arbitrary", "arbitrary"),
                                             vmem_limit_bytes=VMEM_LIMIT),
        name="in_proj_rg_lru",
    )(x, mod, g.reshape(1, d), w_bf16, conv_w, row(conv_b),
      _block_diag_weight(ga_w).astype(BF16), row(ga_b),
      _block_diag_weight(gx_w).astype(BF16), row(gx_b), row(lam), row(out_g),
      _block_diag_ones(GROUP, HEAD))


def _seg_sum(x, bd):
    return jnp.dot(x.astype(BF16), bd, preferred_element_type=F32)


def _unit_lower_inverses(mats, row, col):
    n = mats[0].shape[0]

    def same_block(size):
        shift = size.bit_length() - 1
        return (row >> shift) == (col >> shift)

    base = same_block(INV_BASE)
    eye = (row == col).astype(F32)
    powers = [jnp.where(base, -a, 0.0) for a in mats]
    ts = [eye + p for p in powers]
    k = 1
    while 2 * k < INV_BASE:
        powers = [_dot(p, p) for p in powers]
        ts = [t + _dot(t, p) for t, p in zip(ts, powers)]
        k *= 2
    size = INV_BASE
    while size < n:
        sel = same_block(2 * size) & jnp.logical_not(same_block(size))
        halves = [_dot(t, jnp.where(sel, a, 0.0)) for t, a in zip(ts, mats)]
        ts = [t - _dot(half, t) for t, half in zip(ts, halves)]
        size *= 2
    return ts


def _rwkv_kernel(*refs, has_vres):
    if has_vres:
        (p_ref, vf_ref, mu_ref, w0_ref, wup_ref, a0_ref, aup_ref, gup_ref, kk_ref, ka_ref, rk_ref,
         lw_ref, lb_ref, bd64_ref, v0_ref, vdn_ref, vup_ref, y_ref, state_ref, hist_ref) = refs
    else:
        (p_ref, mu_ref, w0_ref, wup_ref, a0_ref, aup_ref, gup_ref, kk_ref, ka_ref, rk_ref,
         lw_ref, lb_ref, bd64_ref, y_ref, v_ref, state_ref, hist_ref) = refs
    nb, cs = y_ref.shape[0], y_ref.shape[1]
    heads = GROUP // HEAD
    ci = pl.program_id(1)

    @pl.when(ci == 0)
    def _():
        state_ref[...] = jnp.zeros_like(state_ref)
        hist_ref[...] = jnp.zeros_like(hist_ref)

    row = lax.broadcasted_iota(jnp.int32, (cs, cs), 0)
    col = lax.broadcasted_iota(jnp.int32, (cs, cs), 1)
    lower_incl = col <= row
    lower_strict = col < row
    masks = [_lane_mask(GROUP, h * HEAD, (h + 1) * HEAD) for h in range(heads)]
    batch = range(nb)
    rows = lambda x, i: x[i * cs:(i + 1) * cs, :]

    ps = [p_ref[i] for i in batch]
    shifted = [_shift_rows(hist_ref[i], ps[i], 1) for i in batch]
    for i in batch:
        hist_ref[i] = ps[i][cs - HIST:, :]
    p = jnp.concatenate(ps, axis=0)
    xs = p + (jnp.concatenate(shifted, axis=0) - p) * mu_ref[...]

    r = xs[:, 0:GROUP]
    k = xs[:, GROUP:2 * GROUP]
    v = xs[:, 2 * GROUP:3 * GROUP]
    lora = xs[:, 3 * GROUP:]
    w = -_softplus(-(w0_ref[...] + _dot(jnp.tanh(lora), wup_ref[...]))) - 0.5
    a = _sigmoid(a0_ref[...] + _dot(lora, aup_ref[...]))
    g = _dot(_sigmoid(lora), gup_ref[...])
    if has_vres:
        mix = _sigmoid(v0_ref[...] + _dot(_dot(v, vdn_ref[...]), vup_ref[...]))
        v = v + (jnp.concatenate([vf_ref[i] for i in batch], axis=0) - v) * mix
    else:
        for i in batch:
            v_ref[i] = rows(v, i)

    bd64 = bd64_ref[...]
    kk = k * kk_ref[...]
    kk = kk / jnp.maximum(jnp.sqrt(_seg_sum(kk * kk, bd64)), 1e-12)
    kmod = k * (1.0 + (a - 1.0) * ka_ref[...])
    log_decay = -jnp.exp(w)

    cum_wide = _dot_split3_lhs_exact(jnp.where(lower_incl, 1.0, 0.0).astype(BF16),
                                     jnp.concatenate([rows(log_decay, i) for i in batch], axis=1))
    cum = jnp.concatenate([cum_wide[:, i * GROUP:(i + 1) * GROUP] for i in batch], axis=0)
    g_inc = jnp.exp(cum)
    g_inv = jnp.exp(-cum)
    kap = kk * jnp.exp(cum - log_decay)
    rt = r * g_inc
    bet = kk * a * g_inv
    kt = kmod * g_inv

    lhs = [jnp.concatenate([rows(kap, i), rows(rt, i)], axis=0) for i in batch]
    rhs = [jnp.concatenate([rows(bet, i), rows(kt, i)], axis=0).astype(BF16) for i in batch]
    pairs = [(i, h) for i in batch for h in range(heads)]
    grams = [_dot_nt(jnp.where(masks[h], lhs[i], 0.0), rhs[i]) for i, h in pairs]
    a_b = [jnp.where(lower_strict, gm[0:cs, 0:cs], 0.0) for gm in grams]
    a_k = [jnp.where(lower_strict, gm[0:cs, cs:], 0.0).astype(BF16) for gm in grams]
    b_r = [jnp.where(lower_incl, gm[cs:, 0:cs], 0.0).astype(BF16) for gm in grams]
    k_r = [jnp.where(lower_incl, gm[cs:, cs:], 0.0).astype(BF16) for gm in grams]
    t_inv = [t.astype(BF16) for t in _unit_lower_inverses(a_b, row, col)]

    def stack_heads(x):
        return jnp.concatenate([jnp.where(m, x, 0.0) for m in masks], axis=0).astype(BF16)

    per_row = lambda mats, i: jnp.concatenate(mats[i * heads:(i + 1) * heads], axis=1)
    dotf = lambda x, y: jnp.dot(x, y, preferred_element_type=F32)
    states = [state_ref[i] for i in batch]
    states_bf = [s.astype(BF16) for s in states]
    v_stack = [stack_heads(rows(v, i)) for i in batch]
    rhs_u = [_dot(rows(kap, i), states_bf[i]) + dotf(per_row(a_k, i), v_stack[i]) for i in batch]
    u = [dotf(per_row(t_inv, i), stack_heads(rhs_u[i])) for i in batch]
    y = [_dot(rows(rt, i), states_bf[i]) + dotf(per_row(k_r, i), v_stack[i])
         - dotf(per_row(b_r, i), stack_heads(u[i])) for i in batch]

    upd = [_dot(jnp.transpose(rows(kt, i)), rows(v, i)) - _dot(jnp.transpose(rows(bet, i)), u[i])
           for i in batch]
    bd_mask = bd64 > 0
    for i in batch:
        g_end = jnp.transpose(rows(g_inc, i))[:, cs - 1:cs]
        state_ref[i] = g_end * (states[i] + jnp.where(bd_mask, upd[i], 0.0))

    y = jnp.concatenate(y, axis=0)
    mean = _seg_sum(y, bd64) * (1.0 / HEAD)
    yc = y - mean
    var = _seg_sum(yc * yc, bd64) * (1.0 / HEAD)
    yn = yc * lax.rsqrt(var + LNX_EPS) * lw_ref[...] + lb_ref[...]
    bonus = _seg_sum(r * kmod * rk_ref[...], bd64) * v
    out = ((yn + bonus) * g).astype(y_ref.dtype)
    for i in batch:
        y_ref[i] = rows(out, i)


def _pad_rows(w, start, total):
    out = jnp.zeros((total, w.shape[1]), w.dtype)
    return lax.dynamic_update_slice(out, w, (start, 0))


def _mixer_rwkv7(p_d, mu, w0, w_up, a0, a_up, g_up, k_k, k_a, r_k, lnx_w, lnx_b, v_first, v_res):
    bsz, seq, _ = p_d.shape
    cs = min(RWKV_CHUNK, seq)
    lora = LORA_W + LORA_A + LORA_G
    row = lambda v: v.reshape(1, -1)
    has_vres = v_res is not None
    nb = math.gcd(RWKV_BATCH, bsz)
    chunk = lambda n: pl.BlockSpec((nb, cs, n), lambda b, i: (b, i, 0))
    args = [p_d]
    specs = [chunk(N_D)]
    if has_vres:
        args.append(v_first)
        specs.append(chunk(GROUP))
    args += [row(mu), row(w0), _pad_rows(w_up, 0, lora).astype(BF16),
             row(a0), _pad_rows(a_up, LORA_W, lora).astype(BF16),
             _pad_rows(g_up, LORA_W + LORA_A, lora).astype(BF16),
             row(k_k), row(k_a), row(r_k), row(lnx_w), row(lnx_b), _block_diag_ones(GROUP, HEAD)]
    specs += [_full((1, N_D)), _full((1, GROUP)), _full((lora, GROUP)),
              _full((1, GROUP)), _full((lora, GROUP)), _full((lora, GROUP)),
              _full((1, GROUP)), _full((1, GROUP)), _full((1, GROUP)), _full((1, GROUP)),
              _full((1, GROUP)), _full((GROUP, GROUP))]
    if has_vres:
        v0, v_down, v_up = v_res
        rank = v_down.shape[1]
        vdn = jnp.zeros((GROUP, lora), F32).at[:, :rank].set(v_down).astype(BF16)
        args += [row(v0), vdn, _pad_rows(v_up, 0, lora).astype(BF16)]
        specs += [_full((1, GROUP)), _full((GROUP, lora)), _full((lora, GROUP))]
        out_shape = jax.ShapeDtypeStruct((bsz, seq, GROUP), BF16)
        out_specs = chunk(GROUP)
    else:
        out_shape = [jax.ShapeDtypeStruct((bsz, seq, GROUP), BF16),
                     jax.ShapeDtypeStruct((bsz, seq, GROUP), F32)]
        out_specs = [chunk(GROUP), chunk(GROUP)]
    out = pl.pallas_call(
        functools.partial(_rwkv_kernel, has_vres=has_vres),
        out_shape=out_shape,
        grid=(bsz // nb, seq // cs),
        in_specs=specs,
        out_specs=out_specs,
        scratch_shapes=[pltpu.VMEM((nb, GROUP, GROUP), F32), pltpu.VMEM((nb, HIST, N_D), F32)],
        compiler_params=pltpu.CompilerParams(dimension_semantics=("arbitrary", "arbitrary"),
                                             vmem_limit_bytes=VMEM_LIMIT),
        name="rwkv7_chunked",
    )(*args)
    if has_vres:
        return out, None
    return out[0], out[1]


def _out_ffn_kernel(x_ref, ya_ref, yb_ref, yc_ref, yd_ref, mod_ref, g_ref, wo_ref, wup_ref,
                    cw_ref, cb_ref, wdn_ref, o_ref, hist_ref):
    tm, d = x_ref.shape[1], x_ref.shape[2]
    d_ff = wdn_ref.shape[0]
    ti = pl.program_id(1)

    @pl.when(ti == 0)
    def _():
        hist_ref[...] = jnp.zeros_like(hist_ref)

    mod = mod_ref[0]
    gate1 = mod[:, 2 * d:3 * d]
    shift2, scale2, gate2 = mod[:, 3 * d:4 * d], mod[:, 4 * d:5 * d], mod[:, 5 * d:6 * d]
    y_cat = jnp.concatenate([ya_ref[0], yb_ref[0], yc_ref[0], yd_ref[0]], axis=1)
    mix = jnp.dot(y_cat, wo_ref[...], preferred_element_type=F32)
    x1 = x_ref[0] + gate1 * mix
    h = _modulated_norm(x1, g_ref[...], shift2, scale2).astype(BF16)

    def conv(u, lo):
        cw = cw_ref[:, lo:lo + FF_CHUNK]
        hist = hist_ref[:, lo:lo + FF_CHUNK]
        out = cw[FF_CONV - 1:FF_CONV] * u + cb_ref[:, lo:lo + FF_CHUNK]
        for dly in range(1, FF_CONV):
            out = out + cw[FF_CONV - 1 - dly:FF_CONV - dly] * _shift_rows(hist, u, dly)
        hist_ref[:, lo:lo + FF_CHUNK] = u[tm - HIST:, :]
        return out

    def up_proj(j):
        lo_g, lo_v = j * FF_CHUNK, d_ff + j * FF_CHUNK
        return (jnp.dot(h, wup_ref[:, lo_g:lo_g + FF_CHUNK], preferred_element_type=F32),
                jnp.dot(h, wup_ref[:, lo_v:lo_v + FF_CHUNK], preferred_element_type=F32))

    n_chunks = d_ff // FF_CHUNK
    acc = None
    ahead = up_proj(0)
    acts, first, pending = [], 0, None
    for j in range(n_chunks):
        raw_g, raw_v = ahead
        if j + 1 < n_chunks:
            ahead = up_proj(j + 1)
        if pending is not None:
            lo_p, act_group = pending
            part = jnp.dot(act_group, wdn_ref[lo_p:lo_p + act_group.shape[1], :],
                           preferred_element_type=F32)
            acc = part if acc is None else acc + part
            pending = None
        lo_g, lo_v = j * FF_CHUNK, d_ff + j * FF_CHUNK
        u_g = conv(raw_g, lo_g)
        u_v = conv(raw_v, lo_v)
        acts.append((u_g * _sigmoid(u_g) * u_v).astype(BF16))
        if len(acts) == FF_DOWN_GROUP or j + 1 == n_chunks:
            pending = (first * FF_CHUNK, jnp.concatenate(acts, axis=1))
            acts, first = [], j + 1
    lo_p, act_group = pending
    part = jnp.dot(act_group, wdn_ref[lo_p:lo_p + act_group.shape[1], :], preferred_element_type=F32)
    acc = part if acc is None else acc + part
    o_ref[0] = x1 + gate2 * acc


def _out_ffn(x, ys, mod, g2, w_out, w_up, conv_w, conv_b, w_down, layer=0):
    bsz, seq, d = x.shape
    w_out, w_up, w_down = _stacked(w_out), _stacked(w_up), _stacked(w_down)
    d_ff = w_down.shape[-2]
    tm = min(FFN_ROW_TILE, seq)
    tile = lambda n: pl.BlockSpec((1, tm, n), lambda b, i: (b, i, 0))
    return pl.pallas_call(
        _out_ffn_kernel,
        out_shape=jax.ShapeDtypeStruct((bsz, seq, d), F32),
        grid=(bsz, seq // tm),
        in_specs=[tile(d), tile(GROUP), tile(GROUP), tile(GROUP), tile(GROUP),
                  pl.BlockSpec((1, 1, mod.shape[-1]), lambda b, i: (b, 0, 0)),
                  _full((1, d)), _resident_layer((4 * GROUP, d), layer),
                  _resident_layer((d, 2 * d_ff), layer),
                  _full((FF_CONV, 2 * d_ff)), _full((1, 2 * d_ff)),
                  _resident_layer((d_ff, d), layer)],
        out_specs=tile(d),
        scratch_shapes=[pltpu.VMEM((HIST, 2 * d_ff), F32)],
        compiler_params=pltpu.CompilerParams(dimension_semantics=("arbitrary", "arbitrary"),
                                             vmem_limit_bytes=VMEM_LIMIT),
        name="out_proj_ffn",
    )(x, *ys, mod, g2.reshape(1, d), w_out, w_up, conv_w, conv_b.reshape(1, -1), w_down)


def kernel(x, c, w_ada, b_ada, norm1_g, norm2_g, w_in, w_out, a_qnorm_g, a_knorm_g, a_lam_q, a_lam_k, a_out_g, b_out_g, c_conv_w, c_conv_b, c_gate_a_w, c_gate_a_b, c_gate_x_w, c_gate_x_b, c_lambda, c_out_g, d_mu, d_w0, d_w_up, d_a0, d_a_up, d_g_up, d_k_k, d_k_a, d_r_k, d_lnx_w, d_lnx_b, d_v0, d_v_down, d_v_up, ff_w_up, ff_conv_w, ff_conv_b, ff_w_down):
    depth = w_in.shape[0]
    bsz = x.shape[0]
    mods = _ada_modulation(c, w_ada, b_ada)
    w_in_bf, w_out_bf = w_in.astype(BF16), w_out.astype(BF16)
    ff_w_up_bf, ff_w_down_bf = ff_w_up.astype(BF16), ff_w_down.astype(BF16)
    v_first = None
    for l in range(depth):
        mod = mods[l].reshape(bsz, 1, -1)
        p_a, p_b, y_c, p_d = _in_proj_lru(x, mod, norm1_g[l], w_in_bf, l, c_conv_w[l], c_conv_b[l],
                                          c_gate_a_w[l], c_gate_a_b[l], c_gate_x_w[l],
                                          c_gate_x_b[l], c_lambda[l], c_out_g[l])
        lam_init = 0.8 - 0.6 * math.exp(-0.3 * l)
        y_a, y_b = _mixer_attention_pair(p_a, a_qnorm_g[l], a_knorm_g[l], a_lam_q[l], a_lam_k[l],
                                         a_out_g[l], lam_init, p_b, b_out_g[l])
        v_res = None if l == 0 else (d_v0[l - 1], d_v_down[l - 1], d_v_up[l - 1])
        y_d, v_d = _mixer_rwkv7(p_d, d_mu[l], d_w0[l], d_w_up[l], d_a0[l], d_a_up[l], d_g_up[l],
                                d_k_k[l], d_k_a[l], d_r_k[l].reshape(-1), d_lnx_w[l], d_lnx_b[l],
                                v_first, v_res)
        if l == 0:
            v_first = v_d
        x = _out_ffn(x, (y_a, y_b, y_c, y_d), mod, norm2_g[l], w_out_bf, ff_w_up_bf,
                     ff_conv_w[l], ff_conv_b[l], ff_w_down_bf, l)
    return x
```
